```python
import math
import jax, jax.numpy as jnp
from jax import lax
import numpy as np


D_MODEL = 1024
BATCH = 4
SEQ = 4096
DEPTH = 1
DEC_BATCH = 32
DEC_SEQ = 8
PAST_LEN = 8192
PAGE_SIZE = 128

HEAD_DIM = 64
H_DIFF = 4
H_SB = 8
DIFF_W = H_DIFF * 2 * HEAD_DIM
SB_W = H_SB * HEAD_DIM
MIX_W = DIFF_W + SB_W
N_BUCKETS = 32
MAX_DISTANCE = 128
N_GROUPS = 4
EXPERTS_PER_GROUP = 4
TOP_K_INNER = 2
D_EXPERT = 256
Q_BLOCK = 128
NORM_EPS = 1e-6
SUBLN_EPS = 1e-5

kernel_name = 'hymba_diffattn_stickbreak_hmoe_step'


def lambda_init(layer):
    return 0.8 - 0.6 * math.exp(-0.3 * layer)


def rms_norm(x, g, eps=NORM_EPS):
    xf = x.astype(jnp.float32)
    y = xf * lax.rsqrt(jnp.mean(xf * xf, axis=-1, keepdims=True) + eps)
    return (y * g.astype(jnp.float32)).astype(x.dtype)


def project(h, w_in):
    B, T, _ = h.shape
    p = h @ w_in
    qd, kd, vd, qs, ks, vs = jnp.split(
        p, [DIFF_W, 2 * DIFF_W, 3 * DIFF_W, 3 * DIFF_W + SB_W, 3 * DIFF_W + 2 * SB_W], axis=-1)
    rd = lambda a: a.reshape(B, T, H_DIFF, 2 * HEAD_DIM)
    rs = lambda a: a.reshape(B, T, H_SB, HEAD_DIM)
    return rd(qd), rd(kd), rd(vd), rs(qs), rs(ks), rs(vs)


def rel_bucket(rel):
    n = jnp.maximum(-rel, 0)
    max_exact = N_BUCKETS // 2
    nf = jnp.maximum(n, 1).astype(jnp.float32)
    large = max_exact + (jnp.log(nf / max_exact) / math.log(MAX_DISTANCE / max_exact)
                         * (N_BUCKETS - max_exact)).astype(jnp.int32)
    large = jnp.minimum(large, N_BUCKETS - 1)
    return jnp.where(n < max_exact, n, large)


def diff_attention(q, k, v, q_pos, k_pos, rel_bias, lam, subln_g, lam_init):
    B, Tq, H, _ = q.shape
    Tk = k.shape[1]
    qf = q.astype(jnp.float32).reshape(B, Tq, H, 2, HEAD_DIM)
    kf = k.astype(jnp.float32).reshape(B, Tk, H, 2, HEAD_DIM)
    s = jnp.einsum('bqhcd,bkhcd->bchqk', qf, kf) * (HEAD_DIM ** -0.5)
    bias = jnp.transpose(rel_bias.astype(jnp.float32)[rel_bucket(k_pos[None, :] - q_pos[:, None])],
                         (2, 0, 1))
    mask = k_pos[None, :] <= q_pos[:, None]
    s = jnp.where(mask, s + bias, -jnp.inf)
    p = jax.nn.softmax(s, axis=-1)
    w = p[:, 0] - lam * p[:, 1]
    o = jnp.einsum('bhqk,bkhe->bqhe', w, v.astype(jnp.float32))
    o = o * lax.rsqrt(jnp.mean(o * o, axis=-1, keepdims=True) + SUBLN_EPS)
    o = o * subln_g.astype(jnp.float32) * (1.0 - lam_init)
    return o.reshape(B, Tq, H * 2 * HEAD_DIM)


def stick_breaking_attention(q, k, v, q_pos, k_pos):
    B, Tq, H, _ = q.shape
    z = jnp.einsum('bqhd,bkhd->bhqk', q.astype(jnp.float32), k.astype(jnp.float32)) * (HEAD_DIM ** -0.5)
    mask = k_pos[None, :] < q_pos[:, None]
    log_beta = jax.nn.log_sigmoid(z)
    log_rest = jnp.where(mask, jax.nn.log_sigmoid(-z), 0.0)
    suffix = lax.cumsum(log_rest, axis=3, reverse=True) - log_rest
    a = jnp.where(mask, jnp.exp(log_beta + suffix), 0.0)
    o = jnp.einsum('bhqk,bkhd->bqhd', a, v.astype(jnp.float32))
    return o.reshape(B, Tq, H * HEAD_DIM)


def mix(qd, qs, q_pos, kd, vd, ks, vs, k_pos, rel_bias, lam, subln_g, lam_init):
    od = diff_attention(qd, kd, vd, q_pos, k_pos, rel_bias, lam, subln_g, lam_init)
    osb = stick_breaking_attention(qs, ks, vs, q_pos, k_pos)
    return jnp.concatenate([od, osb], axis=-1)


def hmoe(x, w_group, b_group, w_erouter, b_erouter, w_gate, w_up, w_down):
    B, T, D = x.shape
    xt = x.reshape(B * T, D)
    xf = xt.astype(jnp.float32)
    gl = xf @ w_group.astype(jnp.float32) + b_group.astype(jnp.float32)
    pg = jax.nn.softmax(gl, axis=-1)
    g = jnp.argmax(gl, axis=-1)
    pg_sel = jnp.take_along_axis(pg, g[:, None], axis=1)[:, 0]
    el = jnp.einsum('nd,dge->nge', xf, w_erouter.astype(jnp.float32)) + b_erouter.astype(jnp.float32)
    el_g = jnp.take_along_axis(el, g[:, None, None], axis=1)[:, 0]
    tv, ti = lax.top_k(el_g, TOP_K_INNER)
    wi = jax.nn.softmax(tv, axis=-1)
    inner = jnp.einsum('nk,nke->ne', wi, jax.nn.one_hot(ti, EXPERTS_PER_GROUP, dtype=jnp.float32))
    gates = pg_sel[:, None, None] * jax.nn.one_hot(g, N_GROUPS, dtype=jnp.float32)[:, :, None] * inner[:, None, :]
    a = jax.nn.silu(jnp.einsum('nd,gedf->ngef', xt, w_gate)) * jnp.einsum('nd,gedf->ngef', xt, w_up)
    y = jnp.einsum('ngef,gefd->nd', a * gates[..., None].astype(a.dtype), w_down)
    return y.reshape(B, T, D).astype(x.dtype)


def setup_inputs(seed: int = 0) -> dict:
    key = jax.random.key(seed)
    ks = jax.random.split(key, 24)
    n_pages = PAST_LEN // PAGE_SIZE
    n_pool = (DEC_BATCH * n_pages * 5) // 4
    nrm = lambda k, s, sc: jax.random.normal(k, s, jnp.float32) * sc
    page_table = jax.random.permutation(ks[0], n_pool)[:DEC_BATCH * n_pages].reshape(DEC_BATCH, n_pages).astype(jnp.int32)
    return {
        'x_prompt': nrm(ks[1], (BATCH, SEQ, D_MODEL), 1.0),
        'x_sample': nrm(ks[2], (DEC_BATCH, DEC_SEQ, D_MODEL), 1.0),
        'cache_kv_diff': nrm(ks[3], (n_pool, PAGE_SIZE, DEPTH, 2, H_DIFF, 2 * HEAD_DIM), 1.0),
        'cache_kv_sb': nrm(ks[4], (n_pool, PAGE_SIZE, DEPTH, 2, H_SB, HEAD_DIM), 1.0),
        'page_table': page_table,
        'rel_bias': nrm(ks[5], (N_BUCKETS, H_DIFF), 0.5),
        'g_mix': 1.0 + nrm(ks[6], (DEPTH, D_MODEL), 0.02),
        'w_in': nrm(ks[7], (DEPTH, D_MODEL, 3 * MIX_W), D_MODEL ** -0.5),
        'lambda_q1': nrm(ks[8], (DEPTH, HEAD_DIM), 0.1),
        'lambda_k1': nrm(ks[9], (DEPTH, HEAD_DIM), 0.1),
        'lambda_q2': nrm(ks[10], (DEPTH, HEAD_DIM), 0.1),
        'lambda_k2': nrm(ks[11], (DEPTH, HEAD_DIM), 0.1),
        'subln_g': 1.0 + nrm(ks[12], (DEPTH, 2 * HEAD_DIM), 0.02),
        'w_out': nrm(ks[13], (DEPTH, MIX_W, D_MODEL), MIX_W ** -0.5),
        'g_ffn': 1.0 + nrm(ks[14], (DEPTH, D_MODEL), 0.02),
        'w_group': nrm(ks[15], (DEPTH, D_MODEL, N_GROUPS), D_MODEL ** -0.5),
        'b_group': nrm(ks[16], (DEPTH, N_GROUPS), 0.01),
        'w_erouter': nrm(ks[17], (DEPTH, D_MODEL, N_GROUPS, EXPERTS_PER_GROUP), D_MODEL ** -0.5),
        'b_erouter': nrm(ks[18], (DEPTH, N_GROUPS, EXPERTS_PER_GROUP), 0.01),
        'w_gate': nrm(ks[19], (DEPTH, N_GROUPS, EXPERTS_PER_GROUP, D_MODEL, D_EXPERT), D_MODEL ** -0.5),
        'w_up': nrm(ks[20], (DEPTH, N_GROUPS, EXPERTS_PER_GROUP, D_MODEL, D_EXPERT), D_MODEL ** -0.5),
        'w_down': nrm(ks[21], (DEPTH, N_GROUPS, EXPERTS_PER_GROUP, D_EXPERT, D_MODEL), D_EXPERT ** -0.5),
        'g_final': 1.0 + nrm(ks[22], (D_MODEL,), 0.02),
    }


def reference(x_prompt, x_sample, cache_kv_diff, cache_kv_sb, page_table, rel_bias, g_mix, w_in,
              lambda_q1, lambda_k1, lambda_q2, lambda_k2, subln_g, w_out, g_ffn, w_group, b_group,
              w_erouter, b_erouter, w_gate, w_up, w_down, g_final):
    xp, xs = x_prompt, x_sample
    B, S, _ = xp.shape
    DB, T, _ = xs.shape
    P = page_table.shape[1] * cache_kv_diff.shape[1]
    nb = S // Q_BLOCK
    p_pos = jnp.arange(S)
    s_qpos = P + jnp.arange(T)
    s_kpos = jnp.arange(P + T)
    kvd_p, kvs_p, kvd_s, kvs_s = [], [], [], []
    to_blocks = lambda a: jnp.moveaxis(a.reshape(B, nb, Q_BLOCK, *a.shape[2:]), 1, 0)
    for l in range(DEPTH):
        li = lambda_init(l)
        f32 = lambda a: a.astype(jnp.float32)
        lam = (jnp.exp(jnp.sum(f32(lambda_q1[l]) * f32(lambda_k1[l])))
               - jnp.exp(jnp.sum(f32(lambda_q2[l]) * f32(lambda_k2[l]))) + li)
        h = rms_norm(xp, g_mix[l])
        qd, kd, vd, qs, ks_, vs = project(h, w_in[l])

        def one_block(args, kd=kd, vd=vd, ks_=ks_, vs=vs, lam=lam, l=l, li=li):
            qdb, qsb, pb = args
            return mix(qdb, qsb, pb, kd, vd, ks_, vs, p_pos, rel_bias, lam, subln_g[l], li)

        o = lax.map(one_block, (to_blocks(qd), to_blocks(qs), p_pos.reshape(nb, Q_BLOCK)))
        o = jnp.moveaxis(o, 0, 1).reshape(B, S, MIX_W)
        xp = xp + o.astype(xp.dtype) @ w_out[l]
        xp = xp + hmoe(rms_norm(xp, g_ffn[l]), w_group[l], b_group[l], w_erouter[l], b_erouter[l],
                       w_gate[l], w_up[l], w_down[l])
        kvd_p.append(jnp.stack([kd, vd], axis=2))
        kvs_p.append(jnp.stack([ks_, vs], axis=2))
        h = rms_norm(xs, g_mix[l])
        qd, kd, vd, qs, ks_, vs = project(h, w_in[l])
        new_d = jnp.stack([kd, vd], axis=2)
        new_s = jnp.stack([ks_, vs], axis=2)
        past_d = cache_kv_diff[:, :, l][page_table].reshape(DB, P, 2, H_DIFF, 2 * HEAD_DIM)
        past_s = cache_kv_sb[:, :, l][page_table].reshape(DB, P, 2, H_SB, HEAD_DIM)
        all_d = jnp.concatenate([past_d, new_d.astype(past_d.dtype)], axis=1)
        all_s = jnp.concatenate([past_s, new_s.astype(past_s.dtype)], axis=1)
        o = mix(qd, qs, s_qpos, all_d[:, :, 0], all_d[:, :, 1], all_s[:, :, 0], all_s[:, :, 1],
                s_kpos, rel_bias, lam, subln_g[l], li)
        xs = xs + o.astype(xs.dtype) @ w_out[l]
        xs = xs + hmoe(rms_norm(xs, g_ffn[l]), w_group[l], b_group[l], w_erouter[l], b_erouter[l],
                       w_gate[l], w_up[l], w_down[l])
        kvd_s.append(new_d)
        kvs_s.append(new_s)
    y_prompt = rms_norm(xp, g_final)
    y_sample = rms_norm(xs, g_final)
    new_kv_diff_prompt = jnp.stack(kvd_p, axis=2)
    new_kv_sb_prompt = jnp.stack(kvs_p, axis=2)
    new_kv_diff_sample = jnp.stack(kvd_s, axis=2)
    new_kv_sb_sample = jnp.stack(kvs_s, axis=2)
    return (y_prompt, y_sample, new_kv_diff_prompt, new_kv_sb_prompt, new_kv_diff_sample, new_kv_sb_sample)
```

```python
import functools
import math

import numpy as np
import jax
import jax.numpy as jnp
from jax import lax
from jax.experimental import pallas as pl
from jax.experimental.pallas import tpu as pltpu

HEAD_DIM = 64
MAX_DISTANCE = 128
NORM_EPS = 1e-6
SUBLN_EPS = 1e-5
TOP_K_INNER = 2

LANES = 128
VMEM_LIMIT_BYTES = 56 * 1024 * 1024

ATTN_TILE = 256
SB_SKIP_LOG = -90.0
PAGES_PER_STEP = 8

F32 = jnp.float32
BF16 = jnp.bfloat16
NT_DIMS = (((1,), (1,)), ((), ()))


def _lambda_init(layer):
    return 0.8 - 0.6 * math.exp(-0.3 * layer)


def _rel_bucket_np(rel, n_buckets):
    n = np.maximum(-rel, 0)
    max_exact = n_buckets // 2
    nf = np.maximum(n, 1).astype(np.float32)
    large = max_exact + (np.log(nf / np.float32(max_exact)) / np.float32(math.log(MAX_DISTANCE / max_exact))
                         * np.float32(n_buckets - max_exact)).astype(np.int32)
    large = np.minimum(large, n_buckets - 1)
    return np.where(n < max_exact, n, large).astype(np.int32)


def _params(sem):
    return pltpu.CompilerParams(dimension_semantics=sem, vmem_limit_bytes=VMEM_LIMIT_BYTES)


def _normed_proj_cols(x_ref, g_ref, w_ref, half_w):
    x = x_ref[...]
    ms = jnp.mean(x * x, axis=-1, keepdims=True)
    h = (x * lax.rsqrt(ms + NORM_EPS) * g_ref[...]).astype(BF16)
    return lambda i: jnp.dot(h, w_ref[:, i * half_w:(i + 1) * half_w], preferred_element_type=F32)


def _proj_prompt_kernel(x_ref, g_ref, w_ref, qd_ref, qs_ref, kvd_ref, kvst_ref, kd16_ref, vdt16_ref, ks16_ref,
                        vst16_ref, *, half_w, scale):
    cols = _normed_proj_cols(x_ref, g_ref, w_ref, half_w)
    tm = x_ref.shape[0]
    qd_ref[...] = (cols(0) * scale).astype(BF16)
    qs_ref[...] = (cols(3) * scale).astype(BF16)
    n_rows = 2 * half_w // LANES
    kd = cols(1)
    vd = cols(2)
    for r in range(n_rows // 2):
        kvd_ref[pl.ds(r, tm, stride=n_rows), :] = kd[:, r * LANES:(r + 1) * LANES]
        kvd_ref[pl.ds(n_rows // 2 + r, tm, stride=n_rows), :] = vd[:, r * LANES:(r + 1) * LANES]
    kd16_ref[...] = kd.astype(BF16)
    vdt16_ref[...] = vd.T.astype(BF16)
    ks = cols(4)
    ks16_ref[...] = ks.astype(BF16)
    kvst_ref[0, :half_w, :] = ks.T
    vst = cols(5).T
    kvst_ref[0, half_w:, :] = vst
    vst16_ref[...] = vst.astype(BF16)


def _project_prompt(x2d, g, w16, *, tm, batch, seq):
    n, d = x2d.shape
    half_w = w16.shape[1] // 6
    n_rows = 2 * half_w // LANES
    nbs = seq // tm
    row = lambda i: (i, 0)
    const = lambda i: (0, 0)
    colblk = lambda i: (0, i)
    out_shape = (jax.ShapeDtypeStruct((n, half_w), BF16), jax.ShapeDtypeStruct((n, half_w), BF16),
                 jax.ShapeDtypeStruct((n * n_rows, LANES), F32),
                 jax.ShapeDtypeStruct((batch, 2 * half_w, seq), F32),
                 jax.ShapeDtypeStruct((n, half_w), BF16), jax.ShapeDtypeStruct((half_w, n), BF16),
                 jax.ShapeDtypeStruct((n, half_w), BF16), jax.ShapeDtypeStruct((half_w, n), BF16))
    out_specs = (pl.BlockSpec((tm, half_w), row), pl.BlockSpec((tm, half_w), row),
                 pl.BlockSpec((tm * n_rows, LANES), row),
                 pl.BlockSpec((1, 2 * half_w, tm), lambda i: (i // nbs, 0, i % nbs)),
                 pl.BlockSpec((tm, half_w), row), pl.BlockSpec((half_w, tm), colblk),
                 pl.BlockSpec((tm, half_w), row), pl.BlockSpec((half_w, tm), colblk))
    return pl.pallas_call(
        functools.partial(_proj_prompt_kernel, half_w=half_w, scale=HEAD_DIM ** -0.5),
        out_shape=out_shape,
        grid=(n // tm,),
        in_specs=[pl.BlockSpec((tm, d), row), pl.BlockSpec((1, d), const),
                  pl.BlockSpec((d, 6 * half_w), const)],
        out_specs=out_specs,
        compiler_params=_params(("parallel",)),
        name="rmsnorm_qkv_proj_prompt",
    )(x2d, g.reshape(1, d), w16)


def _proj_sample_kernel(x_ref, g_ref, w_ref, qd_ref, qs_ref, kvd_ref, kvs_ref, *, half_w, scale):
    cols = _normed_proj_cols(x_ref, g_ref, w_ref, half_w)
    qd_ref[...] = cols(0) * scale
    qs_ref[...] = cols(3) * scale
    kvd_ref[:, :half_w] = cols(1)
    kvd_ref[:, half_w:] = cols(2)
    kvs_ref[:, :half_w] = cols(4)
    kvs_ref[:, half_w:] = cols(5)


def _project_sample(x2d, g, w16, *, tm):
    n, d = x2d.shape
    half_w = w16.shape[1] // 6
    row = lambda i: (i, 0)
    const = lambda i: (0, 0)
    return pl.pallas_call(
        functools.partial(_proj_sample_kernel, half_w=half_w, scale=HEAD_DIM ** -0.5),
        out_shape=(jax.ShapeDtypeStruct((n, half_w), F32), jax.ShapeDtypeStruct((n, half_w), F32),
                   jax.ShapeDtypeStruct((n, 2 * half_w), F32), jax.ShapeDtypeStruct((n, 2 * half_w), F32)),
        grid=(n // tm,),
        in_specs=[pl.BlockSpec((tm, d), row), pl.BlockSpec((1, d), const),
                  pl.BlockSpec((d, 6 * half_w), const)],
        out_specs=(pl.BlockSpec((tm, half_w), row), pl.BlockSpec((tm, half_w), row),
                   pl.BlockSpec((tm, 2 * half_w), row), pl.BlockSpec((tm, 2 * half_w), row)),
        compiler_params=_params(("parallel",)),
        name="rmsnorm_qkv_proj_sample",
    )(x2d, g.reshape(1, d), w16)


def _bias_select(bucket, rb_ref, h, n_buckets):
    out = jnp.zeros(bucket.shape, F32)
    for b in range(n_buckets):
        out = jnp.where(bucket == b, rb_ref[b, h], out)
    return out


def _prompt_bias_kernel(rb_ref, bucket_ref, out_ref, *, n_buckets):
    h = pl.program_id(0)
    for d in range(bucket_ref.shape[0]):
        tile = _bias_select(bucket_ref[d], rb_ref, h, n_buckets)
        out_ref[0, d] = jnp.concatenate([tile, tile], axis=1)


def _prompt_bias_tiles(rel_bias, t):
    n_buckets, n_heads = rel_bias.shape
    key = np.arange(t)[:, None]
    qry = np.arange(t)[None, :]
    bucket = np.stack([_rel_bucket_np(key - qry - d * t, n_buckets) for d in range(2)])
    return pl.pallas_call(
        functools.partial(_prompt_bias_kernel, n_buckets=n_buckets),
        out_shape=jax.ShapeDtypeStruct((n_heads, 2, t, 2 * t), F32),
        grid=(n_heads,),
        in_specs=[pl.BlockSpec(memory_space=pltpu.SMEM),
                  pl.BlockSpec((2, t, t), lambda h: (0, 0, 0))],
        out_specs=pl.BlockSpec((1, 2, t, 2 * t), lambda h: (h, 0, 0, 0)),
        compiler_params=_params(("parallel",)),
        name="prompt_rel_bias_tiles",
    )(rel_bias.astype(F32), jnp.asarray(bucket))


def _sample_bias_kernel(rb_ref, bucket_ref, out_ref, *, n_buckets, n_heads):
    rows = bucket_ref.shape[0] // n_heads
    for h in range(n_heads):
        sl = slice(h * rows, (h + 1) * rows)
        out_ref[sl, :] = _bias_select(bucket_ref[sl, :], rb_ref, h, n_buckets)


def _sample_bias_table(rel_bias, past_len, page, t_new):
    n_buckets, n_heads = rel_bias.shape
    rows = n_heads * 2 * t_new
    q_pos = past_len + (np.arange(rows) % t_new)[:, None]
    far = np.full((rows, page), n_buckets - 1, np.int32)
    last = _rel_bucket_np((past_len - page + np.arange(page))[None, :] - q_pos, n_buckets)
    new = _rel_bucket_np((past_len + np.arange(page))[None, :] - q_pos, n_buckets)
    bucket = np.concatenate([far, last, new], axis=1)
    return pl.pallas_call(
        functools.partial(_sample_bias_kernel, n_buckets=n_buckets, n_heads=n_heads),
        out_shape=jax.ShapeDtypeStruct(bucket.shape, F32),
        in_specs=[pl.BlockSpec(memory_space=pltpu.SMEM), pl.BlockSpec(memory_space=pltpu.VMEM)],
        out_specs=pl.BlockSpec(memory_space=pltpu.VMEM),
        name="sample_rel_bias_table",
    )(rel_bias.astype(F32), jnp.asarray(bucket))


def _split_heads_rows(q):
    lane = lax.broadcasted_iota(jnp.int32, q.shape, 1)
    zero = jnp.zeros_like(q)
    return jnp.concatenate([jnp.where(lane < HEAD_DIM, q, zero), jnp.where(lane >= HEAD_DIM, q, zero)], axis=0)


def _lambda_value(lq1_ref, lk1_ref, lq2_ref, lk2_ref, lam_init):
    a = jnp.sum(lq1_ref[...] * lk1_ref[...], axis=-1, keepdims=True)
    b = jnp.sum(lq2_ref[...] * lk2_ref[...], axis=-1, keepdims=True)
    return jnp.exp(a) - jnp.exp(b) + lam_init


def _diff_prompt_kernel(rb_ref, lq1_ref, lk1_ref, lq2_ref, lk2_ref, g_ref, q_ref, k_ref, vt_ref, bias_ref,
                        o_ref, m_ref, l_ref, acc_ref, *, t, far_bucket, lam_init):
    h = pl.program_id(1)
    qi = pl.program_id(2)
    qq = _split_heads_rows(q_ref[...])

    def scores(ki):
        k = k_ref[pl.ds(pl.multiple_of(ki * t, t), t), :]
        return lax.dot_general(k, qq, NT_DIMS, preferred_element_type=F32)

    def accumulate(ki, s):
        m_old = m_ref[...]
        m_new = jnp.maximum(m_old, jnp.max(s, axis=0, keepdims=True))
        alpha = jnp.exp(m_old - m_new)
        p = jnp.exp(s - m_new)
        l_ref[...] = alpha * l_ref[...] + jnp.sum(p, axis=0, keepdims=True)
        vt = vt_ref[:, pl.ds(pl.multiple_of(ki * t, t), t)]
        acc_ref[...] = alpha * acc_ref[...] + jnp.dot(vt, p.astype(BF16), preferred_element_type=F32)
        m_ref[...] = m_new

    key = lax.broadcasted_iota(jnp.int32, (t, 2 * t), 0)
    col = lax.broadcasted_iota(jnp.int32, (t, 2 * t), 1)
    qry = jnp.where(col >= t, col - t, col)
    s = jnp.where(key <= qry, scores(qi) + bias_ref[0, 0], -jnp.inf)
    m0 = jnp.max(s, axis=0, keepdims=True)
    p = jnp.exp(s - m0)
    m_ref[...] = m0
    l_ref[...] = jnp.sum(p, axis=0, keepdims=True)
    vt = vt_ref[:, pl.ds(pl.multiple_of(qi * t, t), t)]
    acc_ref[...] = jnp.dot(vt, p.astype(BF16), preferred_element_type=F32)

    @pl.when(qi >= 1)
    def _():
        accumulate(qi - 1, scores(qi - 1) + bias_ref[0, 1])

    far_bias = rb_ref[far_bucket, h]

    def far_body(ki, carry):
        accumulate(ki, scores(ki) + far_bias)
        return carry

    lax.fori_loop(0, qi - 1, far_body, 0)

    o = acc_ref[...] / l_ref[...]
    lam = _lambda_value(lq1_ref, lk1_ref, lq2_ref, lk2_ref, lam_init)
    od = o[:, :t] - lam * o[:, t:]
    od = od * lax.rsqrt(jnp.mean(od * od, axis=0, keepdims=True) + SUBLN_EPS)
    od = od * g_ref[...] * (1.0 - lam_init)
    o_ref[...] = od.T.astype(o_ref.dtype)


def _diff_prompt_attention(qd16, kd16, vdt16, bias_tiles, rel_bias, lam_params, subln_g, *, batch, seq, lam_init):
    n, width = qd16.shape
    n_heads = width // LANES
    t = ATTN_TILE
    nq = seq // t
    n_buckets = rel_bias.shape[0]
    assert int(_rel_bucket_np(np.array([-(t + 1)]), n_buckets)[0]) == n_buckets - 1
    small = lambda b, h, i: (0, 0)
    return pl.pallas_call(
        functools.partial(_diff_prompt_kernel, t=t, far_bucket=n_buckets - 1, lam_init=lam_init),
        out_shape=jax.ShapeDtypeStruct((n, width), BF16),
        grid=(batch, n_heads, nq),
        in_specs=[pl.BlockSpec(memory_space=pltpu.SMEM)]
                 + [pl.BlockSpec((1, HEAD_DIM), small)] * 4
                 + [pl.BlockSpec((LANES, 1), small),
                    pl.BlockSpec((t, LANES), lambda b, h, i: (b * nq + i, h)),
                    pl.BlockSpec((seq, LANES), lambda b, h, i: (b, h)),
                    pl.BlockSpec((LANES, seq), lambda b, h, i: (h, b)),
                    pl.BlockSpec((1, 2, t, 2 * t), lambda b, h, i: (h, 0, 0, 0))],
        out_specs=pl.BlockSpec((t, LANES), lambda b, h, i: (b * nq + i, h)),
        scratch_shapes=[pltpu.VMEM((1, 2 * t), F32), pltpu.VMEM((1, 2 * t), F32),
                        pltpu.VMEM((LANES, 2 * t), F32)],
        compiler_params=_params(("parallel", "parallel", "arbitrary")),
        name="diff_attention_prompt",
    )(rel_bias.astype(F32), *lam_params, subln_g.reshape(LANES, 1), qd16, kd16, vdt16, bias_tiles)


def _log_sigmoid_pair(z):
    soft = jnp.log(1.0 + jnp.exp(-jnp.abs(z)))
    return jnp.minimum(z, 0.0) - soft, jnp.minimum(-z, 0.0) - soft


def _split_hi_lo(x):
    hi = x.astype(BF16)
    lo = (x - hi.astype(F32)).astype(BF16)
    return hi, lo


def _sb_prompt_kernel(q_ref, k_ref, vt_ref, o_ref, run_ref, acc_ref, *, t):
    qi = pl.program_id(2)
    qq = _split_heads_rows(q_ref[...])
    row = lax.broadcasted_iota(jnp.int32, (t, t), 0)
    colk = lax.broadcasted_iota(jnp.int32, (t, t), 1)
    later = (colk > row).astype(BF16)
    later2 = jnp.concatenate([later, later], axis=1)

    def block(ki, mask):
        k = k_ref[pl.ds(pl.multiple_of(ki * t, t), t), :]
        z = lax.dot_general(k, qq, NT_DIMS, preferred_element_type=F32)
        log_beta, log_rest = _log_sigmoid_pair(z)
        if mask is not None:
            log_rest = jnp.where(mask, log_rest, 0.0)
        hi, lo = _split_hi_lo(log_rest)
        suffix = jnp.dot(later2, jnp.concatenate([hi, lo], axis=0), preferred_element_type=F32)
        a = jnp.exp(log_beta + suffix + run_ref[...])
        if mask is not None:
            a = jnp.where(mask, a, 0.0)
        vt = vt_ref[:, pl.ds(pl.multiple_of(ki * t, t), t)]
        acc_ref[...] += jnp.dot(vt, a.astype(BF16), preferred_element_type=F32)
        run_ref[...] += jnp.sum(log_rest, axis=0, keepdims=True)

    run_ref[...] = jnp.zeros_like(run_ref)
    acc_ref[...] = jnp.zeros_like(acc_ref)
    key = lax.broadcasted_iota(jnp.int32, (t, 2 * t), 0)
    col = lax.broadcasted_iota(jnp.int32, (t, 2 * t), 1)
    qry = jnp.where(col >= t, col - t, col)
    block(qi, key < qry)

    def cond(c):
        ki, live = c
        return jnp.logical_and(ki >= 0, live > SB_SKIP_LOG)

    def body(c):
        ki, _ = c
        block(ki, None)
        return ki - 1, jnp.max(run_ref[...])

    lax.while_loop(cond, body, (qi - 1, jnp.max(run_ref[...])))

    acc = acc_ref[...]
    o = jnp.concatenate([acc[:HEAD_DIM, :t], acc[HEAD_DIM:, t:]], axis=0)
    o_ref[...] = o.T.astype(o_ref.dtype)


def _sb_prompt_attention(qs16, ks16, vst16, *, batch, seq):
    n, width = qs16.shape
    n_pairs = width // LANES
    t = ATTN_TILE
    nq = seq // t
    return pl.pallas_call(
        functools.partial(_sb_prompt_kernel, t=t),
        out_shape=jax.ShapeDtypeStruct((n, width), BF16),
        grid=(batch, n_pairs, nq),
        in_specs=[pl.BlockSpec((t, LANES), lambda b, h, i: (b * nq + i, h)),
                  pl.BlockSpec((seq, LANES), lambda b, h, i: (b, h)),
                  pl.BlockSpec((LANES, seq), lambda b, h, i: (h, b))],
        out_specs=pl.BlockSpec((t, LANES), lambda b, h, i: (b * nq + i, h)),
        scratch_shapes=[pltpu.VMEM((1, 2 * t), F32), pltpu.VMEM((LANES, 2 * t), F32)],
        compiler_params=_params(("parallel", "parallel", "arbitrary")),
        name="stick_breaking_attention_prompt",
    )(qs16, ks16, vst16)


def _block_rows(q, group_w):
    t, w = q.shape
    n_groups = w // group_w
    qt = jnp.concatenate([q] * n_groups, axis=0)
    row_g = lax.broadcasted_iota(jnp.int32, qt.shape, 0) // t
    col_g = lax.broadcasted_iota(jnp.int32, qt.shape, 1) // group_w
    return jnp.where(row_g == col_g, qt, 0.0).astype(BF16)


def _pad_rows(x, rows):
    return jnp.concatenate([x, jnp.zeros((rows - x.shape[0], x.shape[1]), x.dtype)], axis=0)


def _gather_head_lanes(acc, n_groups, t):
    w = acc.shape[1]
    col_g = lax.broadcasted_iota(jnp.int32, (t, w), 1) // (w // n_groups)
    out = jnp.zeros((t, w), acc.dtype)
    for g in range(n_groups):
        out = jnp.where(col_g == g, acc[g * t:(g + 1) * t, :], out)
    return out


def _diff_sample_kernel(pt_ref, lq1_ref, lk1_ref, lq2_ref, lk2_ref, g_ref, q_ref, kvn_ref, bias_ref, *rest,
                        n_pages_step, page, half_w, t_new, lam_init, row0, slot_rows):
    page_refs = rest[:n_pages_step]
    o_ref, qb_ref, m_ref, l_ref, acc_ref = rest[n_pages_step:]
    j = pl.program_id(1)
    n_steps = pl.num_programs(1)
    n_heads = half_w // LANES

    def accumulate(k, v, bias, mask):
        s = lax.dot_general(qb_ref[...], k, NT_DIMS, preferred_element_type=F32) + bias
        if mask is not None:
            s = jnp.where(mask, s, -jnp.inf)
        m_old = m_ref[...]
        m_new = jnp.maximum(m_old, jnp.max(s, axis=1, keepdims=True))
        alpha = jnp.exp(m_old - m_new)
        p = jnp.exp(s - m_new)
        l_ref[...] = alpha * l_ref[...] + jnp.sum(p, axis=1, keepdims=True)
        acc_ref[...] = alpha * acc_ref[...] + jnp.dot(p.astype(BF16), v, preferred_element_type=F32)
        m_ref[...] = m_new

    @pl.when(j == 0)
    def _():
        qb_ref[...] = _block_rows(q_ref[...], HEAD_DIM)
        rows = qb_ref.shape[0]
        m_ref[...] = jnp.full(m_ref.shape, -jnp.inf, F32)
        l_ref[...] = jnp.zeros_like(l_ref)
        acc_ref[...] = jnp.zeros_like(acc_ref)
        kvn = _pad_rows(kvn_ref[...], page)
        tq = lax.broadcasted_iota(jnp.int32, (rows, page), 0) % t_new
        tk = lax.broadcasted_iota(jnp.int32, (rows, page), 1)
        accumulate(kvn[:, :half_w].astype(BF16), kvn[:, half_w:].astype(BF16),
                   bias_ref[:, 2 * page:3 * page], tk <= tq)

    def heads(ref, first):
        parts = [ref[0, pl.ds(row0 + first + h, page, stride=slot_rows), :] for h in range(n_heads)]
        return jnp.concatenate(parts, axis=1).astype(BF16)

    for g in range(n_pages_step):
        bias = bias_ref[:, 0:page]
        if g == n_pages_step - 1:
            bias = jnp.where(j == n_steps - 1, bias_ref[:, page:2 * page], bias)
        accumulate(heads(page_refs[g], 0), heads(page_refs[g], n_heads), bias, None)

    @pl.when(j == n_steps - 1)
    def _():
        o = acc_ref[...] / l_ref[...]
        o = _gather_head_lanes(o, n_heads, 2 * t_new)
        lam = _lambda_value(lq1_ref, lk1_ref, lq2_ref, lk2_ref, lam_init)
        od = o[:t_new] - lam * o[t_new:]
        outs = []
        for h in range(n_heads):
            oh = od[:, h * LANES:(h + 1) * LANES]
            oh = oh * lax.rsqrt(jnp.mean(oh * oh, axis=-1, keepdims=True) + SUBLN_EPS)
            outs.append(oh * g_ref[...] * (1.0 - lam_init))
        o_ref[...] = jnp.concatenate(outs, axis=1).astype(o_ref.dtype)


def _diff_sample_attention(q32, kvn32, cache_rows, layer, depth, page_table, bias_tab, lam_params, subln_g, *,
                           t_new, lam_init):
    n, half_w = q32.shape
    n_seq, n_pages = page_table.shape
    n_heads = half_w // LANES
    slot_rows = depth * 2 * n_heads
    page = cache_rows.shape[1] // slot_rows
    gp = PAGES_PER_STEP
    rows = n_heads * 2 * t_new
    small = lambda b, j, pt: (0, 0)
    seq_row = lambda b, j, pt: (b, 0)

    def page_spec(g):
        return pl.BlockSpec((1, page * slot_rows, LANES), lambda b, j, pt: (pt[b, j * gp + g], 0, 0))

    return pl.pallas_call(
        functools.partial(_diff_sample_kernel, n_pages_step=gp, page=page, half_w=half_w, t_new=t_new,
                          lam_init=lam_init, row0=layer * 2 * n_heads, slot_rows=slot_rows),
        out_shape=jax.ShapeDtypeStruct((n, half_w), F32),
        grid_spec=pltpu.PrefetchScalarGridSpec(
            num_scalar_prefetch=1,
            grid=(n_seq, n_pages // gp),
            in_specs=[pl.BlockSpec((1, HEAD_DIM), small)] * 4
                     + [pl.BlockSpec((1, LANES), small),
                        pl.BlockSpec((t_new, half_w), seq_row),
                        pl.BlockSpec((t_new, 2 * half_w), seq_row),
                        pl.BlockSpec((rows, 3 * page), small)]
                     + [page_spec(g) for g in range(gp)],
            out_specs=pl.BlockSpec((t_new, half_w), seq_row),
            scratch_shapes=[pltpu.VMEM((rows, half_w), BF16), pltpu.VMEM((rows, 1), F32),
                            pltpu.VMEM((rows, 1), F32), pltpu.VMEM((rows, half_w), F32)]),
        compiler_params=_params(("parallel", "arbitrary")),
        name="diff_attention_sample",
    )(page_table, *lam_params, subln_g.reshape(1, LANES), q32, kvn32, bias_tab, *([cache_rows] * gp))


def _sb_sample_kernel(pt_ref, q_ref, kvn_ref, *rest, n_pages_step, page, half_w, t_new):
    page_refs = rest[:n_pages_step]
    o_ref, qb_ref, run_ref, acc_ref = rest[n_pages_step:]
    j = pl.program_id(1)
    n_steps = pl.num_programs(1)
    row = lax.broadcasted_iota(jnp.int32, (page, page), 0)
    colk = lax.broadcasted_iota(jnp.int32, (page, page), 1)
    later = (row > colk).astype(BF16)
    later2 = jnp.concatenate([later, later], axis=0)

    def accumulate(z, pv, mask):
        log_beta, log_rest = _log_sigmoid_pair(z)
        if mask is not None:
            log_rest = jnp.where(mask, log_rest, 0.0)
        hi, lo = _split_hi_lo(log_rest)
        suffix = jnp.dot(jnp.concatenate([hi, lo], axis=1), later2, preferred_element_type=F32)
        a = jnp.exp(log_beta + suffix + run_ref[...])
        if mask is not None:
            a = jnp.where(mask, a, 0.0)
        acc_ref[...] += pv(a.astype(BF16))
        run_ref[...] += jnp.sum(log_rest, axis=1, keepdims=True)

    @pl.when(j == 0)
    def _():
        qb_ref[...] = _block_rows(q_ref[...], HEAD_DIM)
        rows = qb_ref.shape[0]
        run_ref[...] = jnp.zeros_like(run_ref)
        acc_ref[...] = jnp.zeros_like(acc_ref)
        kvn = _pad_rows(kvn_ref[...], page)
        k = kvn[:, :half_w].astype(BF16)
        v = kvn[:, half_w:].astype(BF16)
        tq = lax.broadcasted_iota(jnp.int32, (rows, page), 0) % t_new
        tk = lax.broadcasted_iota(jnp.int32, (rows, page), 1)
        accumulate(lax.dot_general(qb_ref[...], k, NT_DIMS, preferred_element_type=F32),
                   lambda a: jnp.dot(a, v, preferred_element_type=F32), tk < tq)

    for g in range(n_pages_step):
        @pl.when(jnp.max(run_ref[...]) > SB_SKIP_LOG)
        def _():
            kt = page_refs[g][0, :half_w, :].astype(BF16)
            vt = page_refs[g][0, half_w:, :].astype(BF16)
            accumulate(jnp.dot(qb_ref[...], kt, preferred_element_type=F32),
                       lambda a: lax.dot_general(a, vt, NT_DIMS, preferred_element_type=F32), None)

    @pl.when(j == n_steps - 1)
    def _():
        n_heads = half_w // HEAD_DIM
        o_ref[...] = _gather_head_lanes(acc_ref[...], n_heads, t_new).astype(o_ref.dtype)


def _sb_sample_attention(q32, kvn32, cache_t, layer, page_table, *, t_new):
    n, half_w = q32.shape
    n_seq, n_pages = page_table.shape
    page = cache_t.shape[2]
    gp = PAGES_PER_STEP
    rows = (half_w // HEAD_DIM) * t_new
    seq_row = lambda b, j, pt: (b, 0)

    def page_spec(g):
        return pl.BlockSpec((1, 2 * half_w, page),
                            lambda b, j, pt: (pt[b, n_pages - 1 - (j * gp + g)], layer, 0))

    return pl.pallas_call(
        functools.partial(_sb_sample_kernel, n_pages_step=gp, page=page, half_w=half_w, t_new=t_new),
        out_shape=jax.ShapeDtypeStruct((n, half_w), F32),
        grid_spec=pltpu.PrefetchScalarGridSpec(
            num_scalar_prefetch=1,
            grid=(n_seq, n_pages // gp),
            in_specs=[pl.BlockSpec((t_new, half_w), seq_row),
                      pl.BlockSpec((t_new, 2 * half_w), seq_row)]
                     + [page_spec(g) for g in range(gp)],
            out_specs=pl.BlockSpec((t_new, half_w), seq_row),
            scratch_shapes=[pltpu.VMEM((rows, half_w), BF16), pltpu.VMEM((rows, 1), F32),
                            pltpu.VMEM((rows, half_w), F32)]),
        compiler_params=_params(("parallel", "arbitrary")),
        name="stick_breaking_attention_sample",
    )(page_table, q32, kvn32, *([cache_t] * gp))


def _router_gates(logits, n_groups, n_experts):
    lane = lax.broadcasted_iota(jnp.int32, logits.shape, 1).astype(F32)
    neg = -jnp.inf
    big = float(LANES)
    is_group = lane < n_groups
    gl = jnp.where(is_group, logits, neg)
    gmax = jnp.max(gl, axis=1, keepdims=True)
    pg_sel = 1.0 / jnp.sum(jnp.where(is_group, jnp.exp(gl - gmax), 0.0), axis=1, keepdims=True)
    g = jnp.min(jnp.where(gl == gmax, lane, big), axis=1, keepdims=True)
    lo = n_groups + g * n_experts
    in_group = jnp.logical_and(lane >= lo, lane < lo + n_experts)
    el = jnp.where(in_group, logits, neg)
    v1 = jnp.max(el, axis=1, keepdims=True)
    i1 = jnp.min(jnp.where(el == v1, lane, big), axis=1, keepdims=True)
    el2 = jnp.where(lane == i1, neg, el)
    v2 = jnp.max(el2, axis=1, keepdims=True)
    i2 = jnp.min(jnp.where(el2 == v2, lane, big), axis=1, keepdims=True)
    e2 = jnp.exp(v2 - v1)
    w1 = 1.0 / (1.0 + e2)
    w2 = e2 * w1
    return pg_sel * (jnp.where(lane == i1, w1, 0.0) + jnp.where(lane == i2, w2, 0.0))


def _post_kernel(x_ref, od_ref, os_ref, wo_ref, gf_ref, wrh_ref, wrl_ref, br_ref, wg_ref, wu_ref, wd_ref,
                 gfin_ref, y_ref, x2_ref, h_ref, gates_ref, acc_ref, *, n_groups, n_experts, final_norm):
    e = pl.program_id(1)
    half = od_ref.shape[1]

    @pl.when(e == 0)
    def _():
        attn = (jnp.dot(od_ref[...], wo_ref[:half, :], preferred_element_type=F32)
                + jnp.dot(os_ref[...], wo_ref[half:, :], preferred_element_type=F32))
        x2 = x_ref[...] + attn
        x2_ref[...] = x2
        hf = x2 * lax.rsqrt(jnp.mean(x2 * x2, axis=-1, keepdims=True) + NORM_EPS) * gf_ref[...]
        hi, lo = _split_hi_lo(hf)
        h_ref[...] = hi
        logits = (jnp.dot(hi, wrh_ref[...], preferred_element_type=F32)
                  + jnp.dot(lo, wrh_ref[...], preferred_element_type=F32)
                  + jnp.dot(hi, wrl_ref[...], preferred_element_type=F32)) + br_ref[...]
        gates_ref[...] = _router_gates(logits, n_groups, n_experts)
        acc_ref[...] = jnp.zeros_like(acc_ref)

    h = h_ref[...]
    gates = gates_ref[...]
    lane = lax.broadcasted_iota(jnp.int32, gates.shape, 1)
    gate = jnp.sum(jnp.where(lane == n_groups + e, gates, 0.0), axis=1, keepdims=True)
    hg = jnp.dot(h, wg_ref[0], preferred_element_type=F32)
    hu = jnp.dot(h, wu_ref[0], preferred_element_type=F32)
    a = hg * (1.0 / (1.0 + jnp.exp(-hg))) * hu * gate
    acc_ref[...] += jnp.dot(a.astype(BF16), wd_ref[0], preferred_element_type=F32)

    @pl.when(e == pl.num_programs(1) - 1)
    def _():
        x3 = x2_ref[...] + acc_ref[...]
        if final_norm:
            x3 = x3 * lax.rsqrt(jnp.mean(x3 * x3, axis=-1, keepdims=True) + NORM_EPS) * gfin_ref[...]
        y_ref[...] = x3


def _post_attention(x2d, od16, os16, wo16, g_ffn, wr_hi, wr_lo, b_router, wg16, wu16, wd16, g_final, *,
                    tm, n_groups, n_experts, final_norm):
    n, d = x2d.shape
    half = od16.shape[1]
    n_exp_total, _, d_exp = wg16.shape
    row = lambda i, e: (i, 0)
    const = lambda i, e: (0, 0)
    return pl.pallas_call(
        functools.partial(_post_kernel, n_groups=n_groups, n_experts=n_experts, final_norm=final_norm),
        out_shape=jax.ShapeDtypeStruct((n, d), F32),
        grid=(n // tm, n_exp_total),
        in_specs=[pl.BlockSpec((tm, d), row), pl.BlockSpec((tm, half), row), pl.BlockSpec((tm, half), row),
                  pl.BlockSpec((d, d), const), pl.BlockSpec((1, d), const),
                  pl.BlockSpec((d, LANES), const), pl.BlockSpec((d, LANES), const), pl.BlockSpec((1, LANES), const),
                  pl.BlockSpec((1, d, d_exp), lambda i, e: (e, 0, 0)),
                  pl.BlockSpec((1, d, d_exp), lambda i, e: (e, 0, 0)),
                  pl.BlockSpec((1, d_exp, d), lambda i, e: (e, 0, 0)),
                  pl.BlockSpec((1, d), const)],
        out_specs=pl.BlockSpec((tm, d), row),
        scratch_shapes=[pltpu.VMEM((tm, d), F32), pltpu.VMEM((tm, d), BF16),
                        pltpu.VMEM((tm, LANES), F32), pltpu.VMEM((tm, d), F32)],
        compiler_params=_params(("parallel", "arbitrary")),
        name="out_proj_hmoe",
    )(x2d, od16, os16, wo16, g_ffn.reshape(1, d), wr_hi, wr_lo, b_router, wg16, wu16, wd16,
      g_final.reshape(1, d))


def _row_tile(n, pref):
    tm = min(n, pref)
    while n % tm:
        tm //= 2
    return tm


def kernel(x_prompt, x_sample, cache_kv_diff, cache_kv_sb, page_table, rel_bias, g_mix, w_in, lambda_q1,
           lambda_k1, lambda_q2, lambda_k2, subln_g, w_out, g_ffn, w_group, b_group, w_erouter, b_erouter,
           w_gate, w_up, w_down, g_final):
    batch, seq, d = x_prompt.shape
    n_seq, t_new, _ = x_sample.shape
    n_pool, page, depth, _, h_diff, diff_hw = cache_kv_diff.shape
    _, _, _, _, h_sb, sb_hw = cache_kv_sb.shape
    n_pages = page_table.shape[1]
    past_len = n_pages * page
    n_groups, n_experts = w_erouter.shape[2], w_erouter.shape[3]
    half_w = h_diff * diff_hw
    assert diff_hw == 2 * HEAD_DIM == LANES and sb_hw == HEAD_DIM and h_sb * sb_hw == half_w
    assert seq % ATTN_TILE == 0 and n_pages % PAGES_PER_STEP == 0 and page == LANES
    assert n_groups + n_groups * n_experts <= LANES

    xp = x_prompt.reshape(batch * seq, d)
    xs = x_sample.reshape(n_seq * t_new, d)
    cache_d = cache_kv_diff.reshape(n_pool, page * depth * 2 * h_diff, diff_hw)
    cache_s = jnp.transpose(cache_kv_sb, (0, 2, 3, 4, 5, 1)).reshape(n_pool, depth * 2 * half_w, page)
    bias_tiles = _prompt_bias_tiles(rel_bias, ATTN_TILE)
    bias_tab = _sample_bias_table(rel_bias, past_len, page, t_new)
    tm_p = _row_tile(seq, 512)
    tm_s = _row_tile(n_seq * t_new, 512)
    tm_moe_p = _row_tile(batch * seq, 1024)

    kvd_p, kvs_p, kvd_s, kvs_s = [], [], [], []
    for l in range(depth):
        li = _lambda_init(l)
        lam_params = [a[l].reshape(1, HEAD_DIM).astype(F32) for a in (lambda_q1, lambda_k1, lambda_q2, lambda_k2)]
        w16 = w_in[l].astype(BF16)
        wo16 = w_out[l].astype(BF16)
        wr = jnp.concatenate([w_group[l].astype(F32), w_erouter[l].astype(F32).reshape(d, n_groups * n_experts)],
                             axis=1)
        wr = jnp.pad(wr, ((0, 0), (0, LANES - wr.shape[1])))
        wr_hi = wr.astype(BF16)
        wr_lo = (wr - wr_hi.astype(F32)).astype(BF16)
        br = jnp.concatenate([b_group[l].astype(F32), b_erouter[l].astype(F32).reshape(-1)])
        br = jnp.pad(br, (0, LANES - br.shape[0])).reshape(1, LANES)
        d_exp = w_gate.shape[-1]
        wg16 = w_gate[l].astype(BF16).reshape(n_groups * n_experts, d, d_exp)
        wu16 = w_up[l].astype(BF16).reshape(n_groups * n_experts, d, d_exp)
        wd16 = w_down[l].astype(BF16).reshape(n_groups * n_experts, d_exp, d)
        last = l == depth - 1

        qd16, qs16, kvd, kvst, kd16, vdt16, ks16, vst16 = _project_prompt(xp, g_mix[l], w16, tm=tm_p, batch=batch,
                                                                         seq=seq)
        od16 = _diff_prompt_attention(qd16, kd16, vdt16, bias_tiles, rel_bias, lam_params, subln_g[l],
                                      batch=batch, seq=seq, lam_init=li)
        os16 = _sb_prompt_attention(qs16, ks16, vst16, batch=batch, seq=seq)
        xp = _post_attention(xp, od16, os16, wo16, g_ffn[l], wr_hi, wr_lo, br, wg16, wu16, wd16, g_final,
                             tm=tm_moe_p, n_groups=n_groups, n_experts=n_experts, final_norm=last)
        kvd_p.append(kvd.reshape(batch, seq, 2, h_diff, diff_hw))
        kvs_p.append(jnp.transpose(kvst.reshape(batch, 2, h_sb, sb_hw, seq), (0, 4, 1, 2, 3)))

        qd32, qs32, kvd, kvs = _project_sample(xs, g_mix[l], w16, tm=tm_s)
        od = _diff_sample_attention(qd32, kvd, cache_d, l, depth, page_table, bias_tab, lam_params, subln_g[l],
                                    t_new=t_new, lam_init=li)
        osb = _sb_sample_attention(qs32, kvs, cache_s, l, page_table, t_new=t_new)
        xs = _post_attention(xs, od.astype(BF16), osb.astype(BF16), wo16, g_ffn[l], wr_hi, wr_lo, br, wg16, wu16,
                             wd16, g_final, tm=tm_s, n_groups=n_groups, n_experts=n_experts, final_norm=last)
        kvd_s.append(kvd.reshape(n_seq, t_new, 2, h_diff, diff_hw))
        kvs_s.append(kvs.reshape(n_seq, t_new, 2, h_sb, sb_hw))

    y_prompt = xp.reshape(batch, seq, d)
    y_sample = xs.reshape(n_seq, t_new, d)
    return (y_prompt, y_sample, jnp.stack(kvd_p, axis=2), jnp.stack(kvs_p, axis=2),
            jnp.stack(kvd_s, axis=2), jnp.stack(kvs_s, axis=2))
```

```python
import functools
import math

import numpy as np
import jax
import jax.numpy as jnp
from jax import lax
from jax.experimental import pallas as pl
from jax.experimental.pallas import tpu as pltpu

HEAD_DIM = 64
MAX_DISTANCE = 128
NORM_EPS = 1e-6
SUBLN_EPS = 1e-5
TOP_K_INNER = 2

LANES = 128
VMEM_LIMIT_BYTES = 56 * 1024 * 1024

ATTN_TILE = 256
SUM_ROWS = 16
SB_SKIP_LOG = -90.0
PAGES_PER_STEP = 8

F32 = jnp.float32
BF16 = jnp.bfloat16
NT_DIMS = (((1,), (1,)), ((), ()))
LOG2_E = math.log2(math.e)


def _lambda_init(layer):
    return 0.8 - 0.6 * math.exp(-0.3 * layer)


def _rel_bucket_np(rel, n_buckets):
    n = np.maximum(-rel, 0)
    max_exact = n_buckets // 2
    nf = np.maximum(n, 1).astype(np.float32)
    large = max_exact + (np.log(nf / np.float32(max_exact)) / np.float32(math.log(MAX_DISTANCE / max_exact))
                         * np.float32(n_buckets - max_exact)).astype(np.int32)
    large = np.minimum(large, n_buckets - 1)
    return np.where(n < max_exact, n, large).astype(np.int32)


def _params(sem):
    return pltpu.CompilerParams(dimension_semantics=sem, vmem_limit_bytes=VMEM_LIMIT_BYTES)


def _normed_proj_cols(x_ref, g_ref, w_ref, half_w):
    x = x_ref[...]
    ms = jnp.mean(x * x, axis=-1, keepdims=True)
    h = (x * lax.rsqrt(ms + NORM_EPS) * g_ref[...]).astype(BF16)
    return lambda i: jnp.dot(h, w_ref[:, i * half_w:(i + 1) * half_w], preferred_element_type=F32)


def _proj_prompt_kernel(x_ref, g_ref, w_ref, qd_ref, qs_ref, kvd_ref, kvst_ref, kd16_ref, vdt16_ref, ks16_ref,
                        vst16_ref, *, half_w, scale):
    cols = _normed_proj_cols(x_ref, g_ref, w_ref, half_w)
    tm = x_ref.shape[0]
    qd_ref[...] = (cols(0) * (scale * LOG2_E)).astype(BF16)
    qs_ref[...] = (cols(3) * (scale * LOG2_E)).astype(BF16)
    n_rows = 2 * half_w // LANES
    kd = cols(1)
    vd = cols(2)
    for r in range(n_rows // 2):
        kvd_ref[pl.ds(r, tm, stride=n_rows), :] = kd[:, r * LANES:(r + 1) * LANES]
        kvd_ref[pl.ds(n_rows // 2 + r, tm, stride=n_rows), :] = vd[:, r * LANES:(r + 1) * LANES]
    kd16_ref[...] = kd.astype(BF16)
    vdt16_ref[...] = vd.T.astype(BF16)
    ks = cols(4)
    ks16_ref[...] = ks.astype(BF16)
    kvst_ref[0, :half_w, :] = ks.T
    vst = cols(5).T
    kvst_ref[0, half_w:, :] = vst
    vst16_ref[...] = vst.astype(BF16)


def _project_prompt(x2d, g, w16, *, tm, batch, seq):
    n, d = x2d.shape
    half_w = w16.shape[1] // 6
    n_rows = 2 * half_w // LANES
    nbs = seq // tm
    row = lambda i: (i, 0)
    const = lambda i: (0, 0)
    colblk = lambda i: (0, i)
    out_shape = (jax.ShapeDtypeStruct((n, half_w), BF16), jax.ShapeDtypeStruct((n, half_w), BF16),
                 jax.ShapeDtypeStruct((n * n_rows, LANES), F32),
                 jax.ShapeDtypeStruct((batch, 2 * half_w, seq), F32),
                 jax.ShapeDtypeStruct((n, half_w), BF16), jax.ShapeDtypeStruct((half_w, n), BF16),
                 jax.ShapeDtypeStruct((n, half_w), BF16), jax.ShapeDtypeStruct((half_w, n), BF16))
    out_specs = (pl.BlockSpec((tm, half_w), row), pl.BlockSpec((tm, half_w), row),
                 pl.BlockSpec((tm * n_rows, LANES), row),
                 pl.BlockSpec((1, 2 * half_w, tm), lambda i: (i // nbs, 0, i % nbs)),
                 pl.BlockSpec((tm, half_w), row), pl.BlockSpec((half_w, tm), colblk),
                 pl.BlockSpec((tm, half_w), row), pl.BlockSpec((half_w, tm), colblk))
    return pl.pallas_call(
        functools.partial(_proj_prompt_kernel, half_w=half_w, scale=HEAD_DIM ** -0.5),
        out_shape=out_shape,
        grid=(n // tm,),
        in_specs=[pl.BlockSpec((tm, d), row), pl.BlockSpec((1, d), const),
                  pl.BlockSpec((d, 6 * half_w), const)],
        out_specs=out_specs,
        compiler_params=_params(("parallel",)),
        name="rmsnorm_qkv_proj_prompt",
    )(x2d, g.reshape(1, d), w16)


def _proj_sample_kernel(x_ref, g_ref, w_ref, qd_ref, qs_ref, kvd_ref, kvs_ref, *, half_w, scale):
    cols = _normed_proj_cols(x_ref, g_ref, w_ref, half_w)
    qd_ref[...] = cols(0) * scale
    qs_ref[...] = cols(3) * scale
    kvd_ref[:, :half_w] = cols(1)
    kvd_ref[:, half_w:] = cols(2)
    kvs_ref[:, :half_w] = cols(4)
    kvs_ref[:, half_w:] = cols(5)


def _project_sample(x2d, g, w16, *, tm):
    n, d = x2d.shape
    half_w = w16.shape[1] // 6
    row = lambda i: (i, 0)
    const = lambda i: (0, 0)
    return pl.pallas_call(
        functools.partial(_proj_sample_kernel, half_w=half_w, scale=HEAD_DIM ** -0.5),
        out_shape=(jax.ShapeDtypeStruct((n, half_w), F32), jax.ShapeDtypeStruct((n, half_w), F32),
                   jax.ShapeDtypeStruct((n, 2 * half_w), F32), jax.ShapeDtypeStruct((n, 2 * half_w), F32)),
        grid=(n // tm,),
        in_specs=[pl.BlockSpec((tm, d), row), pl.BlockSpec((1, d), const),
                  pl.BlockSpec((d, 6 * half_w), const)],
        out_specs=(pl.BlockSpec((tm, half_w), row), pl.BlockSpec((tm, half_w), row),
                   pl.BlockSpec((tm, 2 * half_w), row), pl.BlockSpec((tm, 2 * half_w), row)),
        compiler_params=_params(("parallel",)),
        name="rmsnorm_qkv_proj_sample",
    )(x2d, g.reshape(1, d), w16)


def _bias_select(bucket, rb_ref, h, n_buckets):
    out = jnp.zeros(bucket.shape, F32)
    for b in range(n_buckets):
        out = jnp.where(bucket == b, rb_ref[b, h], out)
    return out


def _prompt_bias_kernel(rb_ref, bucket_ref, out_ref, *, n_buckets):
    h = pl.program_id(0)
    for d in range(bucket_ref.shape[0]):
        tile = _bias_select(bucket_ref[d], rb_ref, h, n_buckets) * LOG2_E
        out_ref[0, d] = jnp.concatenate([tile, tile], axis=1)


def _prompt_bias_tiles(rel_bias, t):
    n_buckets, n_heads = rel_bias.shape
    key = np.arange(t)[:, None]
    qry = np.arange(t)[None, :]
    bucket = np.stack([_rel_bucket_np(key - qry - d * t, n_buckets) for d in range(2)])
    return pl.pallas_call(
        functools.partial(_prompt_bias_kernel, n_buckets=n_buckets),
        out_shape=jax.ShapeDtypeStruct((n_heads, 2, t, 2 * t), F32),
        grid=(n_heads,),
        in_specs=[pl.BlockSpec(memory_space=pltpu.SMEM),
                  pl.BlockSpec((2, t, t), lambda h: (0, 0, 0))],
        out_specs=pl.BlockSpec((1, 2, t, 2 * t), lambda h: (h, 0, 0, 0)),
        compiler_params=_params(("parallel",)),
        name="prompt_rel_bias_tiles",
    )(rel_bias.astype(F32), jnp.asarray(bucket))


def _sample_bias_kernel(rb_ref, bucket_ref, out_ref, *, n_buckets, n_heads):
    rows = bucket_ref.shape[0] // n_heads
    for h in range(n_heads):
        sl = slice(h * rows, (h + 1) * rows)
        out_ref[sl, :] = _bias_select(bucket_ref[sl, :], rb_ref, h, n_buckets)


def _sample_bias_table(rel_bias, past_len, page, t_new):
    n_buckets, n_heads = rel_bias.shape
    rows = n_heads * 2 * t_new
    q_pos = past_len + (np.arange(rows) % t_new)[:, None]
    far = np.full((rows, page), n_buckets - 1, np.int32)
    last = _rel_bucket_np((past_len - page + np.arange(page))[None, :] - q_pos, n_buckets)
    new = _rel_bucket_np((past_len + np.arange(page))[None, :] - q_pos, n_buckets)
    bucket = np.concatenate([far, last, new], axis=1)
    return pl.pallas_call(
        functools.partial(_sample_bias_kernel, n_buckets=n_buckets, n_heads=n_heads),
        out_shape=jax.ShapeDtypeStruct(bucket.shape, F32),
        in_specs=[pl.BlockSpec(memory_space=pltpu.SMEM), pl.BlockSpec(memory_space=pltpu.VMEM)],
        out_specs=pl.BlockSpec(memory_space=pltpu.VMEM),
        name="sample_rel_bias_table",
    )(rel_bias.astype(F32), jnp.asarray(bucket))


def _split_heads_rows(q):
    lane = lax.broadcasted_iota(jnp.int32, q.shape, 1)
    zero = jnp.zeros_like(q)
    return jnp.concatenate([jnp.where(lane < HEAD_DIM, q, zero), jnp.where(lane >= HEAD_DIM, q, zero)], axis=0)


def _lambda_value(lq1_ref, lk1_ref, lq2_ref, lk2_ref, lam_init):
    a = jnp.sum(lq1_ref[...] * lk1_ref[...], axis=-1, keepdims=True)
    b = jnp.sum(lq2_ref[...] * lk2_ref[...], axis=-1, keepdims=True)
    return jnp.exp(a) - jnp.exp(b) + lam_init


def _diff_prompt_kernel(rb_ref, lq1_ref, lk1_ref, lq2_ref, lk2_ref, g_ref, q_ref, k_ref, vt_ref, bias_ref,
                        o_ref, m_ref, acc_ref, s_ref, *, t, far_bucket, lam_init):
    qi = pl.program_id(1)
    n_heads = q_ref.shape[1] // LANES
    head = lambda h: slice(h * LANES, (h + 1) * LANES)
    qq = [_split_heads_rows(q_ref[:, head(h)]) for h in range(n_heads)]

    def scores(h, ki):
        k = k_ref[pl.ds(pl.multiple_of(ki * t, t), t), head(h)]
        return lax.dot_general(k, qq[h], NT_DIMS, preferred_element_type=F32)

    ones = jnp.ones((SUM_ROWS, t), BF16)

    def weighted_values(h, ki, p):
        vt = vt_ref[head(h), pl.ds(pl.multiple_of(ki * t, t), t)]
        return jnp.dot(jnp.concatenate([vt, ones], axis=0), p.astype(BF16), preferred_element_type=F32)

    def accumulate(h, ki, s, shift):
        m_old = m_ref[h]
        m_blk = jnp.max(s, axis=0, keepdims=True)
        m_new = jnp.maximum(m_old, m_blk if shift is None else m_blk + shift)
        alpha = jnp.exp2(m_old - m_new)
        p = jnp.exp2(s - (m_new if shift is None else m_new - shift))
        acc_ref[h] = alpha * acc_ref[h] + weighted_values(h, ki, p)
        m_ref[h] = m_new

    key = lax.broadcasted_iota(jnp.int32, (t, 2 * t), 0)
    col = lax.broadcasted_iota(jnp.int32, (t, 2 * t), 1)
    causal = key <= jnp.where(col >= t, col - t, col)
    s_next = scores(0, qi)
    for h in range(n_heads):
        s_cur = s_next
        if h + 1 < n_heads:
            s_next = scores(h + 1, qi)
        s = jnp.where(causal, s_cur + bias_ref[h, 0], -jnp.inf)
        m0 = jnp.max(s, axis=0, keepdims=True)
        p = jnp.exp2(s - m0)
        m_ref[h] = m0
        acc_ref[h] = weighted_values(h, qi, p)

    @pl.when(qi >= 1)
    def _():
        s_next = scores(0, qi - 1)
        for h in range(n_heads):
            s_cur = s_next
            if h + 1 < n_heads:
                s_next = scores(h + 1, qi - 1)
            accumulate(h, qi - 1, s_cur + bias_ref[h, 1], None)

    far_bias = [rb_ref[far_bucket, h] * LOG2_E for h in range(n_heads)]

    def far_blocks(ki0, n_blocks):
        s_next = s_ref[...]
        for ki in [ki0 + j for j in range(n_blocks)]:
            for h in range(n_heads):
                s_cur = s_next
                s_next = scores(h + 1, ki) if h + 1 < n_heads else scores(0, ki + 1)
                accumulate(h, ki, s_cur, far_bias[h])
        s_ref[...] = s_next

    def far_pair(i, carry):
        far_blocks(2 * i, 2)
        return carry

    n_far = jnp.maximum(qi - 1, 0)
    s_ref[...] = scores(0, 0)
    lax.fori_loop(0, n_far // 2, far_pair, 0)

    @pl.when(n_far % 2 == 1)
    def _():
        far_blocks(n_far - 1, 1)

    lam = _lambda_value(lq1_ref, lk1_ref, lq2_ref, lk2_ref, lam_init)
    for h in range(n_heads):
        o = acc_ref[h, :LANES, :] / acc_ref[h, LANES:LANES + 1, :]
        od = o[:, :t] - lam * o[:, t:]
        od = od * lax.rsqrt(jnp.mean(od * od, axis=0, keepdims=True) + SUBLN_EPS)
        od = od * g_ref[...] * (1.0 - lam_init)
        o_ref[:, head(h)] = od.T.astype(o_ref.dtype)


def _diff_prompt_attention(qd16, kd16, vdt16, bias_tiles, rel_bias, lam_params, subln_g, *, batch, seq, lam_init):
    n, width = qd16.shape
    n_heads = width // LANES
    t = ATTN_TILE
    nq = seq // t
    n_buckets = rel_bias.shape[0]
    assert int(_rel_bucket_np(np.array([-(t + 1)]), n_buckets)[0]) == n_buckets - 1
    small = lambda b, i: (0, 0)
    return pl.pallas_call(
        functools.partial(_diff_prompt_kernel, t=t, far_bucket=n_buckets - 1, lam_init=lam_init),
        out_shape=jax.ShapeDtypeStruct((n, width), BF16),
        grid=(batch, nq),
        in_specs=[pl.BlockSpec(memory_space=pltpu.SMEM)]
                 + [pl.BlockSpec((1, HEAD_DIM), small)] * 4
                 + [pl.BlockSpec((LANES, 1), small),
                    pl.BlockSpec((t, width), lambda b, i: (b * nq + i, 0)),
                    pl.BlockSpec((seq, width), lambda b, i: (b, 0)),
                    pl.BlockSpec((width, seq), lambda b, i: (0, b)),
                    pl.BlockSpec((n_heads, 2, t, 2 * t), lambda b, i: (0, 0, 0, 0))],
        out_specs=pl.BlockSpec((t, width), lambda b, i: (b * nq + i, 0)),
        scratch_shapes=[pltpu.VMEM((n_heads, 1, 2 * t), F32),
                        pltpu.VMEM((n_heads, LANES + SUM_ROWS, 2 * t), F32), pltpu.VMEM((t, 2 * t), F32)],
        compiler_params=_params(("parallel", "arbitrary")),
        name="diff_attention_prompt",
    )(rel_bias.astype(F32), *lam_params, subln_g.reshape(LANES, 1), qd16, kd16, vdt16, bias_tiles)


def _log_sigmoid_pair(z):
    soft = jnp.log(1.0 + jnp.exp(-jnp.abs(z)))
    return jnp.minimum(z, 0.0) - soft, jnp.minimum(-z, 0.0) - soft


def _split_hi_lo(x):
    hi = x.astype(BF16)
    lo = (x - hi.astype(F32)).astype(BF16)
    return hi, lo


def _sb_prompt_kernel(q_ref, k_ref, vt_ref, o_ref, used_ref, acc_ref, z_ref, *, t):
    qi = pl.program_id(1)
    n_pairs = q_ref.shape[1] // LANES
    pair = lambda p: slice(p * LANES, (p + 1) * LANES)
    qq = [_split_heads_rows(q_ref[:, pair(p)]) for p in range(n_pairs)]
    row = lax.broadcasted_iota(jnp.int32, (t, t), 0)
    colk = lax.broadcasted_iota(jnp.int32, (t, t), 1)
    later = (colk > row).astype(BF16)
    later2 = jnp.concatenate([later, later], axis=1)

    def logits(p, ki):
        k = k_ref[pl.ds(pl.multiple_of(ki * t, t), t), pair(p)]
        return lax.dot_general(k, qq[p], NT_DIMS, preferred_element_type=F32)

    def stage_a(z, mask):
        soft = jnp.maximum(z, 0.0) + jnp.log2(1.0 + jnp.exp2(-jnp.abs(z)))
        log_beta = z - soft
        if mask is not None:
            soft = jnp.where(mask, soft, 0.0)
        hi, lo = _split_hi_lo(soft)
        after = jnp.dot(later2, jnp.concatenate([hi, lo], axis=0), preferred_element_type=F32)
        return log_beta, after, after[0:1, :] + soft[0:1, :]

    def stage_b(p, ki, mask, log_beta, after, total):
        used = used_ref[p]
        a = jnp.exp2(log_beta - after - used)
        if mask is not None:
            a = jnp.where(mask, a, 0.0)
        vt = vt_ref[pair(p), pl.ds(pl.multiple_of(ki * t, t), t)]
        acc_ref[p] += jnp.dot(vt, a.astype(BF16), preferred_element_type=F32)
        used_ref[p] = used + total

    def run_block(ki, z_first, mask):
        older = jnp.maximum(ki - 1, 0)
        z_next = z_first
        pending = None
        for p in range(n_pairs):
            z_cur = z_next
            z_next = logits(p + 1, ki) if p + 1 < n_pairs else logits(0, older)
            staged = stage_a(z_cur, mask)
            if pending is not None:
                stage_b(*pending)
            pending = (p, ki, mask) + staged
        stage_b(*pending)
        return z_next

    used_ref[...] = jnp.zeros_like(used_ref)
    acc_ref[...] = jnp.zeros_like(acc_ref)
    key = lax.broadcasted_iota(jnp.int32, (t, 2 * t), 0)
    col = lax.broadcasted_iota(jnp.int32, (t, 2 * t), 1)
    z_ref[...] = run_block(qi, logits(0, qi), key < jnp.where(col >= t, col - t, col))

    def cond(c):
        ki, least_used = c
        return jnp.logical_and(ki >= 0, least_used < -SB_SKIP_LOG * LOG2_E)

    def body(c):
        ki, _ = c
        z_ref[...] = run_block(ki, z_ref[...], None)
        return ki - 1, jnp.min(used_ref[...])

    lax.while_loop(cond, body, (qi - 1, jnp.min(used_ref[...])))

    for p in range(n_pairs):
        o = jnp.concatenate([acc_ref[p, :HEAD_DIM, :t], acc_ref[p, HEAD_DIM:, t:]], axis=0)
        o_ref[:, pair(p)] = o.T.astype(o_ref.dtype)


def _sb_prompt_attention(qs16, ks16, vst16, *, batch, seq):
    n, width = qs16.shape
    n_pairs = width // LANES
    t = ATTN_TILE
    nq = seq // t
    return pl.pallas_call(
        functools.partial(_sb_prompt_kernel, t=t),
        out_shape=jax.ShapeDtypeStruct((n, width), BF16),
        grid=(batch, nq),
        in_specs=[pl.BlockSpec((t, width), lambda b, i: (b * nq + i, 0)),
                  pl.BlockSpec((seq, width), lambda b, i: (b, 0)),
                  pl.BlockSpec((width, seq), lambda b, i: (0, b))],
        out_specs=pl.BlockSpec((t, width), lambda b, i: (b * nq + i, 0)),
        scratch_shapes=[pltpu.VMEM((n_pairs, 1, 2 * t), F32), pltpu.VMEM((n_pairs, LANES, 2 * t), F32),
                        pltpu.VMEM((t, 2 * t), F32)],
        compiler_params=_params(("parallel", "arbitrary")),
        name="stick_breaking_attention_prompt",
    )(qs16, ks16, vst16)


def _block_rows(q, group_w):
    t, w = q.shape
    n_groups = w // group_w
    qt = jnp.concatenate([q] * n_groups, axis=0)
    row_g = lax.broadcasted_iota(jnp.int32, qt.shape, 0) // t
    col_g = lax.broadcasted_iota(jnp.int32, qt.shape, 1) // group_w
    return jnp.where(row_g == col_g, qt, 0.0).astype(BF16)


def _pad_rows(x, rows):
    return jnp.concatenate([x, jnp.zeros((rows - x.shape[0], x.shape[1]), x.dtype)], axis=0)


def _gather_head_lanes(acc, n_groups, t):
    w = acc.shape[1]
    col_g = lax.broadcasted_iota(jnp.int32, (t, w), 1) // (w // n_groups)
    out = jnp.zeros((t, w), acc.dtype)
    for g in range(n_groups):
        out = jnp.where(col_g == g, acc[g * t:(g + 1) * t, :], out)
    return out


def _diff_sample_kernel(pt_ref, lq1_ref, lk1_ref, lq2_ref, lk2_ref, g_ref, q_ref, kvn_ref, bias_ref, *rest,
                        n_pages_step, page, half_w, t_new, lam_init, row0, slot_rows):
    page_refs = rest[:n_pages_step]
    o_ref, qb_ref, m_ref, l_ref, acc_ref = rest[n_pages_step:]
    j = pl.program_id(1)
    n_steps = pl.num_programs(1)
    n_heads = half_w // LANES

    def accumulate(k, v, bias, mask):
        s = lax.dot_general(qb_ref[...], k, NT_DIMS, preferred_element_type=F32) + bias
        if mask is not None:
            s = jnp.where(mask, s, -jnp.inf)
        m_old = m_ref[...]
        m_new = jnp.maximum(m_old, jnp.max(s, axis=1, keepdims=True))
        alpha = jnp.exp(m_old - m_new)
        p = jnp.exp(s - m_new)
        l_ref[...] = alpha * l_ref[...] + jnp.sum(p, axis=1, keepdims=True)
        acc_ref[...] = alpha * acc_ref[...] + jnp.dot(p.astype(BF16), v, preferred_element_type=F32)
        m_ref[...] = m_new

    @pl.when(j == 0)
    def _():
        qb_ref[...] = _block_rows(q_ref[...], HEAD_DIM)
        rows = qb_ref.shape[0]
        m_ref[...] = jnp.full(m_ref.shape, -jnp.inf, F32)
        l_ref[...] = jnp.zeros_like(l_ref)
        acc_ref[...] = jnp.zeros_like(acc_ref)
        kvn = _pad_rows(kvn_ref[...], page)
        tq = lax.broadcasted_iota(jnp.int32, (rows, page), 0) % t_new
        tk = lax.broadcasted_iota(jnp.int32, (rows, page), 1)
        accumulate(kvn[:, :half_w].astype(BF16), kvn[:, half_w:].astype(BF16),
                   bias_ref[:, 2 * page:3 * page], tk <= tq)

    def heads(ref, first):
        parts = [ref[0, pl.ds(row0 + first + h, page, stride=slot_rows), :] for h in range(n_heads)]
        return jnp.concatenate(parts, axis=1).astype(BF16)

    for g in range(n_pages_step):
        bias = bias_ref[:, 0:page]
        if g == n_pages_step - 1:
            bias = jnp.where(j == n_steps - 1, bias_ref[:, page:2 * page], bias)
        accumulate(heads(page_refs[g], 0), heads(page_refs[g], n_heads), bias, None)

    @pl.when(j == n_steps - 1)
    def _():
        o = acc_ref[...] / l_ref[...]
        o = _gather_head_lanes(o, n_heads, 2 * t_new)
        lam = _lambda_value(lq1_ref, lk1_ref, lq2_ref, lk2_ref, lam_init)
        od = o[:t_new] - lam * o[t_new:]
        outs = []
        for h in range(n_heads):
            oh = od[:, h * LANES:(h + 1) * LANES]
            oh = oh * lax.rsqrt(jnp.mean(oh * oh, axis=-1, keepdims=True) + SUBLN_EPS)
            outs.append(oh * g_ref[...] * (1.0 - lam_init))
        o_ref[...] = jnp.concatenate(outs, axis=1).astype(o_ref.dtype)


def _diff_sample_attention(q32, kvn32, cache_rows, layer, depth, page_table, bias_tab, lam_params, subln_g, *,
                           t_new, lam_init):
    n, half_w = q32.shape
    n_seq, n_pages = page_table.shape
    n_heads = half_w // LANES
    slot_rows = depth * 2 * n_heads
    page = cache_rows.shape[1] // slot_rows
    gp = PAGES_PER_STEP
    rows = n_heads * 2 * t_new
    small = lambda b, j, pt: (0, 0)
    seq_row = lambda b, j, pt: (b, 0)

    def page_spec(g):
        return pl.BlockSpec((1, page * slot_rows, LANES), lambda b, j, pt: (pt[b, j * gp + g], 0, 0))

    return pl.pallas_call(
        functools.partial(_diff_sample_kernel, n_pages_step=gp, page=page, half_w=half_w, t_new=t_new,
                          lam_init=lam_init, row0=layer * 2 * n_heads, slot_rows=slot_rows),
        out_shape=jax.ShapeDtypeStruct((n, half_w), F32),
        grid_spec=pltpu.PrefetchScalarGridSpec(
            num_scalar_prefetch=1,
            grid=(n_seq, n_pages // gp),
            in_specs=[pl.BlockSpec((1, HEAD_DIM), small)] * 4
                     + [pl.BlockSpec((1, LANES), small),
                        pl.BlockSpec((t_new, half_w), seq_row),
                        pl.BlockSpec((t_new, 2 * half_w), seq_row),
                        pl.BlockSpec((rows, 3 * page), small)]
                     + [page_spec(g) for g in range(gp)],
            out_specs=pl.BlockSpec((t_new, half_w), seq_row),
            scratch_shapes=[pltpu.VMEM((rows, half_w), BF16), pltpu.VMEM((rows, 1), F32),
                            pltpu.VMEM((rows, 1), F32), pltpu.VMEM((rows, half_w), F32)]),
        compiler_params=_params(("parallel", "arbitrary")),
        name="diff_attention_sample",
    )(page_table, *lam_params, subln_g.reshape(1, LANES), q32, kvn32, bias_tab, *([cache_rows] * gp))


def _sb_sample_kernel(pt_ref, q_ref, kvn_ref, cache_ref, o_ref, qb_ref, run_ref, acc_ref, buf_ref, sem_ref, *,
                      n_pages, page, half_w, t_new, row0):
    b = pl.program_id(0)

    def page_copy(j, slot):
        return pltpu.make_async_copy(cache_ref.at[pt_ref[b, j], pl.ds(row0, 2 * half_w), :],
                                     buf_ref.at[slot], sem_ref.at[slot])

    slot_of = lambda j: (n_pages - 1 - j) & 1
    page_copy(n_pages - 1, 0).start()
    row = lax.broadcasted_iota(jnp.int32, (page, page), 0)
    colk = lax.broadcasted_iota(jnp.int32, (page, page), 1)
    later = (row > colk).astype(BF16)
    later2 = jnp.concatenate([later, later], axis=0)

    def accumulate(z, pv, mask):
        log_beta, log_rest = _log_sigmoid_pair(z)
        if mask is not None:
            log_rest = jnp.where(mask, log_rest, 0.0)
        hi, lo = _split_hi_lo(log_rest)
        suffix = jnp.dot(jnp.concatenate([hi, lo], axis=1), later2, preferred_element_type=F32)
        a = jnp.exp(log_beta + suffix + run_ref[...])
        if mask is not None:
            a = jnp.where(mask, a, 0.0)
        acc_ref[...] += pv(a.astype(BF16))
        run_ref[...] += jnp.sum(log_rest, axis=1, keepdims=True)

    qb_ref[...] = _block_rows(q_ref[...], HEAD_DIM)
    rows = qb_ref.shape[0]
    run_ref[...] = jnp.zeros_like(run_ref)
    acc_ref[...] = jnp.zeros_like(acc_ref)
    kvn = _pad_rows(kvn_ref[...], page)
    k_new = kvn[:, :half_w].astype(BF16)
    v_new = kvn[:, half_w:].astype(BF16)
    tq = lax.broadcasted_iota(jnp.int32, (rows, page), 0) % t_new
    tk = lax.broadcasted_iota(jnp.int32, (rows, page), 1)
    accumulate(lax.dot_general(qb_ref[...], k_new, NT_DIMS, preferred_element_type=F32),
               lambda a: jnp.dot(a, v_new, preferred_element_type=F32), tk < tq)

    def cond(c):
        j, live = c
        return jnp.logical_and(j >= 0, live > SB_SKIP_LOG)

    def body(c):
        j, _ = c
        slot = slot_of(j)
        page_copy(j, slot).wait()

        @pl.when(j >= 1)
        def _():
            page_copy(j - 1, 1 - slot).start()

        kt = buf_ref[slot, :half_w, :].astype(BF16)
        vt = buf_ref[slot, half_w:, :].astype(BF16)
        accumulate(jnp.dot(qb_ref[...], kt, preferred_element_type=F32),
                   lambda a: lax.dot_general(a, vt, NT_DIMS, preferred_element_type=F32), None)
        return j - 1, jnp.max(run_ref[...])

    j_end, _ = lax.while_loop(cond, body, (n_pages - 1, jnp.max(run_ref[...])))

    @pl.when(j_end >= 0)
    def _():
        page_copy(j_end, slot_of(j_end)).wait()

    n_heads = half_w // HEAD_DIM
    o_ref[...] = _gather_head_lanes(acc_ref[...], n_heads, t_new).astype(o_ref.dtype)


def _sb_sample_attention(q32, kvn32, cache_t, layer, page_table, *, t_new):
    n, half_w = q32.shape
    n_seq, n_pages = page_table.shape
    page = cache_t.shape[2]
    rows = (half_w // HEAD_DIM) * t_new
    seq_row = lambda b, pt: (b, 0)
    return pl.pallas_call(
        functools.partial(_sb_sample_kernel, n_pages=n_pages, page=page, half_w=half_w, t_new=t_new,
                          row0=layer * 2 * half_w),
        out_shape=jax.ShapeDtypeStruct((n, half_w), F32),
        grid_spec=pltpu.PrefetchScalarGridSpec(
            num_scalar_prefetch=1,
            grid=(n_seq,),
            in_specs=[pl.BlockSpec((t_new, half_w), seq_row),
                      pl.BlockSpec((t_new, 2 * half_w), seq_row),
                      pl.BlockSpec(memory_space=pl.ANY)],
            out_specs=pl.BlockSpec((t_new, half_w), seq_row),
            scratch_shapes=[pltpu.VMEM((rows, half_w), BF16), pltpu.VMEM((rows, 1), F32),
                            pltpu.VMEM((rows, half_w), F32), pltpu.VMEM((2, 2 * half_w, page), F32),
                            pltpu.SemaphoreType.DMA((2,))]),
        compiler_params=_params(("arbitrary",)),
        name="stick_breaking_attention_sample",
    )(page_table, q32, kvn32, cache_t)


def _router_gates(logits, n_groups, n_experts):
    lane = lax.broadcasted_iota(jnp.int32, logits.shape, 1).astype(F32)
    neg = -jnp.inf
    big = float(LANES)
    is_group = lane < n_groups
    gl = jnp.where(is_group, logits, neg)
    gmax = jnp.max(gl, axis=1, keepdims=True)
    pg_sel = 1.0 / jnp.sum(jnp.where(is_group, jnp.exp(gl - gmax), 0.0), axis=1, keepdims=True)
    g = jnp.min(jnp.where(gl == gmax, lane, big), axis=1, keepdims=True)
    lo = n_groups + g * n_experts
    in_group = jnp.logical_and(lane >= lo, lane < lo + n_experts)
    el = jnp.where(in_group, logits, neg)
    v1 = jnp.max(el, axis=1, keepdims=True)
    i1 = jnp.min(jnp.where(el == v1, lane, big), axis=1, keepdims=True)
    el2 = jnp.where(lane == i1, neg, el)
    v2 = jnp.max(el2, axis=1, keepdims=True)
    i2 = jnp.min(jnp.where(el2 == v2, lane, big), axis=1, keepdims=True)
    e2 = jnp.exp(v2 - v1)
    w1 = 1.0 / (1.0 + e2)
    w2 = e2 * w1
    return pg_sel * (jnp.where(lane == i1, w1, 0.0) + jnp.where(lane == i2, w2, 0.0))


def _post_kernel(x_ref, od_ref, os_ref, wo_ref, gf_ref, wrh_ref, wrl_ref, br_ref, wg_ref, wu_ref, wd_ref,
                 gfin_ref, y_ref, x2_ref, h_ref, gates_ref, acc_ref, *, n_groups, n_experts, final_norm):
    e = pl.program_id(1)
    half = od_ref.shape[1]

    @pl.when(e == 0)
    def _():
        attn = (jnp.dot(od_ref[...], wo_ref[:half, :], preferred_element_type=F32)
                + jnp.dot(os_ref[...], wo_ref[half:, :], preferred_element_type=F32))
        x2 = x_ref[...] + attn
        x2_ref[...] = x2
        hf = x2 * lax.rsqrt(jnp.mean(x2 * x2, axis=-1, keepdims=True) + NORM_EPS) * gf_ref[...]
        hi, lo = _split_hi_lo(hf)
        h_ref[...] = hi
        logits = (jnp.dot(hi, wrh_ref[...], preferred_element_type=F32)
                  + jnp.dot(lo, wrh_ref[...], preferred_element_type=F32)
                  + jnp.dot(hi, wrl_ref[...], preferred_element_type=F32)) + br_ref[...]
        gates_ref[...] = _router_gates(logits, n_groups, n_experts)
        acc_ref[...] = jnp.zeros_like(acc_ref)

    h = h_ref[...]
    gates = gates_ref[...]
    lane = lax.broadcasted_iota(jnp.int32, gates.shape, 1)
    gate = jnp.sum(jnp.where(lane == n_groups + e, gates, 0.0), axis=1, keepdims=True)
    hg = jnp.dot(h, wg_ref[0], preferred_element_type=F32)
    hu = jnp.dot(h, wu_ref[0], preferred_element_type=F32)
    a = hg * (1.0 / (1.0 + jnp.exp(-hg))) * hu * gate
    acc_ref[...] += jnp.dot(a.astype(BF16), wd_ref[0], preferred_element_type=F32)

    @pl.when(e == pl.num_programs(1) - 1)
    def _():
        x3 = x2_ref[...] + acc_ref[...]
        if final_norm:
            x3 = x3 * lax.rsqrt(jnp.mean(x3 * x3, axis=-1, keepdims=True) + NORM_EPS) * gfin_ref[...]
        y_ref[...] = x3


def _post_attention(x2d, od16, os16, wo16, g_ffn, wr_hi, wr_lo, b_router, wg16, wu16, wd16, g_final, *,
                    tm, n_groups, n_experts, final_norm):
    n, d = x2d.shape
    half = od16.shape[1]
    n_exp_total, _, d_exp = wg16.shape
    row = lambda i, e: (i, 0)
    const = lambda i, e: (0, 0)
    return pl.pallas_call(
        functools.partial(_post_kernel, n_groups=n_groups, n_experts=n_experts, final_norm=final_norm),
        out_shape=jax.ShapeDtypeStruct((n, d), F32),
        grid=(n // tm, n_exp_total),
        in_specs=[pl.BlockSpec((tm, d), row), pl.BlockSpec((tm, half), row), pl.BlockSpec((tm, half), row),
                  pl.BlockSpec((d, d), const), pl.BlockSpec((1, d), const),
                  pl.BlockSpec((d, LANES), const), pl.BlockSpec((d, LANES), const), pl.BlockSpec((1, LANES), const),
                  pl.BlockSpec((1, d, d_exp), lambda i, e: (e, 0, 0)),
                  pl.BlockSpec((1, d, d_exp), lambda i, e: (e, 0, 0)),
                  pl.BlockSpec((1, d_exp, d), lambda i, e: (e, 0, 0)),
                  pl.BlockSpec((1, d), const)],
        out_specs=pl.BlockSpec((tm, d), row),
        scratch_shapes=[pltpu.VMEM((tm, d), F32), pltpu.VMEM((tm, d), BF16),
                        pltpu.VMEM((tm, LANES), F32), pltpu.VMEM((tm, d), F32)],
        compiler_params=_params(("parallel", "arbitrary")),
        name="out_proj_hmoe",
    )(x2d, od16, os16, wo16, g_ffn.reshape(1, d), wr_hi, wr_lo, b_router, wg16, wu16, wd16,
      g_final.reshape(1, d))


def _row_tile(n, pref):
    tm = min(n, pref)
    while n % tm:
        tm //= 2
    return tm


def kernel(x_prompt, x_sample, cache_kv_diff, cache_kv_sb, page_table, rel_bias, g_mix, w_in, lambda_q1,
           lambda_k1, lambda_q2, lambda_k2, subln_g, w_out, g_ffn, w_group, b_group, w_erouter, b_erouter,
           w_gate, w_up, w_down, g_final):
    batch, seq, d = x_prompt.shape
    n_seq, t_new, _ = x_sample.shape
    n_pool, page, depth, _, h_diff, diff_hw = cache_kv_diff.shape
    _, _, _, _, h_sb, sb_hw = cache_kv_sb.shape
    n_pages = page_table.shape[1]
    past_len = n_pages * page
    n_groups, n_experts = w_erouter.shape[2], w_erouter.shape[3]
    half_w = h_diff * diff_hw
    assert diff_hw == 2 * HEAD_DIM == LANES and sb_hw == HEAD_DIM and h_sb * sb_hw == half_w
    assert seq % ATTN_TILE == 0 and n_pages % PAGES_PER_STEP == 0 and page == LANES
    assert n_groups + n_groups * n_experts <= LANES

    xp = x_prompt.reshape(batch * seq, d)
    xs = x_sample.reshape(n_seq * t_new, d)
    cache_d = cache_kv_diff.reshape(n_pool, page * depth * 2 * h_diff, diff_hw)
    cache_s = jnp.transpose(cache_kv_sb, (0, 2, 3, 4, 5, 1)).reshape(n_pool, depth * 2 * half_w, page)
    bias_tiles = _prompt_bias_tiles(rel_bias, ATTN_TILE)
    bias_tab = _sample_bias_table(rel_bias, past_len, page, t_new)
    tm_p = _row_tile(seq, 512)
    tm_s = _row_tile(n_seq * t_new, 512)
    tm_moe_p = _row_tile(batch * seq, 1024)

    kvd_p, kvs_p, kvd_s, kvs_s = [], [], [], []
    for l in range(depth):
        li = _lambda_init(l)
        lam_params = [a[l].reshape(1, HEAD_DIM).astype(F32) for a in (lambda_q1, lambda_k1, lambda_q2, lambda_k2)]
        w16 = w_in[l].astype(BF16)
        wo16 = w_out[l].astype(BF16)
        wr = jnp.concatenate([w_group[l].astype(F32), w_erouter[l].astype(F32).reshape(d, n_groups * n_experts)],
                             axis=1)
        wr = jnp.pad(wr, ((0, 0), (0, LANES - wr.shape[1])))
        wr_hi = wr.astype(BF16)
        wr_lo = (wr - wr_hi.astype(F32)).astype(BF16)
        br = jnp.concatenate([b_group[l].astype(F32), b_erouter[l].astype(F32).reshape(-1)])
        br = jnp.pad(br, (0, LANES - br.shape[0])).reshape(1, LANES)
        d_exp = w_gate.shape[-1]
        wg16 = w_gate[l].astype(BF16).reshape(n_groups * n_experts, d, d_exp)
        wu16 = w_up[l].astype(BF16).reshape(n_groups * n_experts, d, d_exp)
        wd16 = w_down[l].astype(BF16).reshape(n_groups * n_experts, d_exp, d)
        last = l == depth - 1

        qd16, qs16, kvd, kvst, kd16, vdt16, ks16, vst16 = _project_prompt(xp, g_mix[l], w16, tm=tm_p, batch=batch,
                                                                         seq=seq)
        od16 = _diff_prompt_attention(qd16, kd16, vdt16, bias_tiles, rel_bias, lam_params, subln_g[l],
                                      batch=batch, seq=seq, lam_init=li)
        os16 = _sb_prompt_attention(qs16, ks16, vst16, batch=batch, seq=seq)
        xp = _post_attention(xp, od16, os16, wo16, g_ffn[l], wr_hi, wr_lo, br, wg16, wu16, wd16, g_final,
                             tm=tm_moe_p, n_groups=n_groups, n_experts=n_experts, final_norm=last)
        kvd_p.append(kvd.reshape(batch, seq, 2, h_diff, diff_hw))
        kvs_p.append(jnp.transpose(kvst.reshape(batch, 2, h_sb, sb_hw, seq), (0, 4, 1, 2, 3)))

        qd32, qs32, kvd, kvs = _project_sample(xs, g_mix[l], w16, tm=tm_s)
        od = _diff_sample_attention(qd32, kvd, cache_d, l, depth, page_table, bias_tab, lam_params, subln_g[l],
                                    t_new=t_new, lam_init=li)
        osb = _sb_sample_attention(qs32, kvs, cache_s, l, page_table, t_new=t_new)
        xs = _post_attention(xs, od.astype(BF16), osb.astype(BF16), wo16, g_ffn[l], wr_hi, wr_lo, br, wg16, wu16,
                             wd16, g_final, tm=tm_s, n_groups=n_groups, n_experts=n_experts, final_norm=last)
        kvd_s.append(kvd.reshape(n_seq, t_new, 2, h_diff, diff_hw))
        kvs_s.append(kvs.reshape(n_seq, t_new, 2, h_sb, sb_hw))

    y_prompt = xp.reshape(batch, seq, d)
    y_sample = xs.reshape(n_seq, t_new, d)
    return (y_prompt, y_sample, jnp.stack(kvd_p, axis=2), jnp.stack(kvs_p, axis=2),
            jnp.stack(kvd_s, axis=2), jnp.stack(kvs_s, axis=2))
```

```python
import functools
import math

import numpy as np
import jax
import jax.numpy as jnp
from jax import lax
from jax.experimental import pallas as pl
from jax.experimental.pallas import tpu as pltpu

HEAD_DIM = 64
MAX_DISTANCE = 128
NORM_EPS = 1e-6
SUBLN_EPS = 1e-5
TOP_K_INNER = 2

LANES = 128
VMEM_LIMIT_BYTES = 56 * 1024 * 1024

ATTN_TILE = 256
SUM_ROWS = 16
SB_SKIP_LOG = -90.0
PAGES_PER_STEP = 8

F32 = jnp.float32
BF16 = jnp.bfloat16
NT_DIMS = (((1,), (1,)), ((), ()))
LOG2_E = math.log2(math.e)


def _lambda_init(layer):
    return 0.8 - 0.6 * math.exp(-0.3 * layer)


def _rel_bucket_np(rel, n_buckets):
    n = np.maximum(-rel, 0)
    max_exact = n_buckets // 2
    nf = np.maximum(n, 1).astype(np.float32)
    large = max_exact + (np.log(nf / np.float32(max_exact)) / np.float32(math.log(MAX_DISTANCE / max_exact))
                         * np.float32(n_buckets - max_exact)).astype(np.int32)
    large = np.minimum(large, n_buckets - 1)
    return np.where(n < max_exact, n, large).astype(np.int32)


def _params(sem):
    return pltpu.CompilerParams(dimension_semantics=sem, vmem_limit_bytes=VMEM_LIMIT_BYTES)


def _normed_proj_cols(x_ref, g_ref, w_ref, half_w):
    x = x_ref[...]
    ms = jnp.mean(x * x, axis=-1, keepdims=True)
    h = (x * lax.rsqrt(ms + NORM_EPS) * g_ref[...]).astype(BF16)
    return lambda i: jnp.dot(h, w_ref[:, i * half_w:(i + 1) * half_w], preferred_element_type=F32)


def _proj_prompt_kernel(x_ref, g_ref, w_ref, qd_ref, qs_ref, kvd_ref, kvst_ref, kd16_ref, vdt16_ref, ks16_ref,
                        vst16_ref, *, half_w, scale):
    cols = _normed_proj_cols(x_ref, g_ref, w_ref, half_w)
    tm = x_ref.shape[0]
    qd_ref[...] = (cols(0) * (scale * LOG2_E)).astype(BF16)
    qs_ref[...] = (cols(3) * (scale * LOG2_E)).astype(BF16)
    n_rows = 2 * half_w // LANES
    kd = cols(1)
    vd = cols(2)
    for r in range(n_rows // 2):
        kvd_ref[pl.ds(r, tm, stride=n_rows), :] = kd[:, r * LANES:(r + 1) * LANES]
        kvd_ref[pl.ds(n_rows // 2 + r, tm, stride=n_rows), :] = vd[:, r * LANES:(r + 1) * LANES]
    kd16_ref[...] = kd.astype(BF16)
    vdt16_ref[...] = vd.T.astype(BF16)
    ks = cols(4)
    ks16_ref[...] = ks.astype(BF16)
    kvst_ref[0, :half_w, :] = ks.T
    vst = cols(5).T
    kvst_ref[0, half_w:, :] = vst
    vst16_ref[...] = vst.astype(BF16)


def _project_prompt(x2d, g, w16, *, tm, batch, seq):
    n, d = x2d.shape
    half_w = w16.shape[1] // 6
    n_rows = 2 * half_w // LANES
    nbs = seq // tm
    row = lambda i: (i, 0)
    const = lambda i: (0, 0)
    colblk = lambda i: (0, i)
    out_shape = (jax.ShapeDtypeStruct((n, half_w), BF16), jax.ShapeDtypeStruct((n, half_w), BF16),
                 jax.ShapeDtypeStruct((n * n_rows, LANES), F32),
                 jax.ShapeDtypeStruct((batch, 2 * half_w, seq), F32),
                 jax.ShapeDtypeStruct((n, half_w), BF16), jax.ShapeDtypeStruct((half_w, n), BF16),
                 jax.ShapeDtypeStruct((n, half_w), BF16), jax.ShapeDtypeStruct((half_w, n), BF16))
    out_specs = (pl.BlockSpec((tm, half_w), row), pl.BlockSpec((tm, half_w), row),
                 pl.BlockSpec((tm * n_rows, LANES), row),
                 pl.BlockSpec((1, 2 * half_w, tm), lambda i: (i // nbs, 0, i % nbs)),
                 pl.BlockSpec((tm, half_w), row), pl.BlockSpec((half_w, tm), colblk),
                 pl.BlockSpec((tm, half_w), row), pl.BlockSpec((half_w, tm), colblk))
    return pl.pallas_call(
        functools.partial(_proj_prompt_kernel, half_w=half_w, scale=HEAD_DIM ** -0.5),
        out_shape=out_shape,
        grid=(n // tm,),
        in_specs=[pl.BlockSpec((tm, d), row), pl.BlockSpec((1, d), const),
                  pl.BlockSpec((d, 6 * half_w), const)],
        out_specs=out_specs,
        compiler_params=_params(("parallel",)),
        name="rmsnorm_qkv_proj_prompt",
    )(x2d, g.reshape(1, d), w16)


def _proj_sample_kernel(x_ref, g_ref, w_ref, qd_ref, qs_ref, kvd_ref, kvs_ref, *, half_w, scale):
    cols = _normed_proj_cols(x_ref, g_ref, w_ref, half_w)
    qd_ref[...] = cols(0) * scale
    qs_ref[...] = cols(3) * scale
    kvd_ref[:, :half_w] = cols(1)
    kvd_ref[:, half_w:] = cols(2)
    kvs_ref[:, :half_w] = cols(4)
    kvs_ref[:, half_w:] = cols(5)


def _project_sample(x2d, g, w16, *, tm):
    n, d = x2d.shape
    half_w = w16.shape[1] // 6
    row = lambda i: (i, 0)
    const = lambda i: (0, 0)
    return pl.pallas_call(
        functools.partial(_proj_sample_kernel, half_w=half_w, scale=HEAD_DIM ** -0.5),
        out_shape=(jax.ShapeDtypeStruct((n, half_w), F32), jax.ShapeDtypeStruct((n, half_w), F32),
                   jax.ShapeDtypeStruct((n, 2 * half_w), F32), jax.ShapeDtypeStruct((n, 2 * half_w), F32)),
        grid=(n // tm,),
        in_specs=[pl.BlockSpec((tm, d), row), pl.BlockSpec((1, d), const),
                  pl.BlockSpec((d, 6 * half_w), const)],
        out_specs=(pl.BlockSpec((tm, half_w), row), pl.BlockSpec((tm, half_w), row),
                   pl.BlockSpec((tm, 2 * half_w), row), pl.BlockSpec((tm, 2 * half_w), row)),
        compiler_params=_params(("parallel",)),
        name="rmsnorm_qkv_proj_sample",
    )(x2d, g.reshape(1, d), w16)


def _bias_select(bucket, rb_ref, h, n_buckets):
    out = jnp.zeros(bucket.shape, F32)
    for b in range(n_buckets):
        out = jnp.where(bucket == b, rb_ref[b, h], out)
    return out


def _prompt_bias_kernel(rb_ref, bucket_ref, out_ref, *, n_buckets):
    h = pl.program_id(0)
    for d in range(bucket_ref.shape[0]):
        tile = _bias_select(bucket_ref[d], rb_ref, h, n_buckets) * LOG2_E
        if d == 0:
            key = lax.broadcasted_iota(jnp.int32, tile.shape, 0)
            qry = lax.broadcasted_iota(jnp.int32, tile.shape, 1)
            tile = jnp.where(key <= qry, tile, -jnp.inf)
        out_ref[0, d] = jnp.concatenate([tile, tile], axis=1)


def _prompt_bias_tiles(rel_bias, t):
    n_buckets, n_heads = rel_bias.shape
    key = np.arange(t)[:, None]
    qry = np.arange(t)[None, :]
    bucket = np.stack([_rel_bucket_np(key - qry - d * t, n_buckets) for d in range(2)])
    return pl.pallas_call(
        functools.partial(_prompt_bias_kernel, n_buckets=n_buckets),
        out_shape=jax.ShapeDtypeStruct((n_heads, 2, t, 2 * t), F32),
        grid=(n_heads,),
        in_specs=[pl.BlockSpec(memory_space=pltpu.SMEM),
                  pl.BlockSpec((2, t, t), lambda h: (0, 0, 0))],
        out_specs=pl.BlockSpec((1, 2, t, 2 * t), lambda h: (h, 0, 0, 0)),
        compiler_params=_params(("parallel",)),
        name="prompt_rel_bias_tiles",
    )(rel_bias.astype(F32), jnp.asarray(bucket))


def _sample_bias_kernel(rb_ref, bucket_ref, out_ref, *, n_buckets, n_heads):
    rows = bucket_ref.shape[0] // n_heads
    for h in range(n_heads):
        sl = slice(h * rows, (h + 1) * rows)
        out_ref[sl, :] = _bias_select(bucket_ref[sl, :], rb_ref, h, n_buckets)


def _sample_bias_table(rel_bias, past_len, page, t_new):
    n_buckets, n_heads = rel_bias.shape
    rows = n_heads * 2 * t_new
    q_pos = past_len + (np.arange(rows) % t_new)[:, None]
    far = np.full((rows, page), n_buckets - 1, np.int32)
    last = _rel_bucket_np((past_len - page + np.arange(page))[None, :] - q_pos, n_buckets)
    new = _rel_bucket_np((past_len + np.arange(page))[None, :] - q_pos, n_buckets)
    bucket = np.concatenate([far, last, new], axis=1)
    return pl.pallas_call(
        functools.partial(_sample_bias_kernel, n_buckets=n_buckets, n_heads=n_heads),
        out_shape=jax.ShapeDtypeStruct(bucket.shape, F32),
        in_specs=[pl.BlockSpec(memory_space=pltpu.SMEM), pl.BlockSpec(memory_space=pltpu.VMEM)],
        out_specs=pl.BlockSpec(memory_space=pltpu.VMEM),
        name="sample_rel_bias_table",
    )(rel_bias.astype(F32), jnp.asarray(bucket))


def _split_heads_rows(q):
    lane = lax.broadcasted_iota(jnp.int32, q.shape, 1)
    zero = jnp.zeros_like(q)
    return jnp.concatenate([jnp.where(lane < HEAD_DIM, q, zero), jnp.where(lane >= HEAD_DIM, q, zero)], axis=0)


def _lambda_value(lq1_ref, lk1_ref, lq2_ref, lk2_ref, lam_init):
    a = jnp.sum(lq1_ref[...] * lk1_ref[...], axis=-1, keepdims=True)
    b = jnp.sum(lq2_ref[...] * lk2_ref[...], axis=-1, keepdims=True)
    return jnp.exp(a) - jnp.exp(b) + lam_init


def _diff_prompt_kernel(rb_ref, lq1_ref, lk1_ref, lq2_ref, lk2_ref, g_ref, q_ref, k_ref, vt_ref, bias_ref,
                        o_ref, m_ref, acc_ref, s_ref, *, t, far_bucket, lam_init):
    qi = pl.program_id(1)
    n_heads = q_ref.shape[1] // LANES
    head = lambda h: slice(h * LANES, (h + 1) * LANES)
    qq = [_split_heads_rows(q_ref[:, head(h)]) for h in range(n_heads)]

    def scores(h, ki):
        k = k_ref[pl.ds(pl.multiple_of(ki * t, t), t), head(h)]
        return lax.dot_general(k, qq[h], NT_DIMS, preferred_element_type=F32)

    ones = jnp.ones((SUM_ROWS, t), BF16)

    def weighted_values(h, ki, p):
        vt = vt_ref[head(h), pl.ds(pl.multiple_of(ki * t, t), t)]
        return jnp.dot(jnp.concatenate([vt, ones], axis=0), p.astype(BF16), preferred_element_type=F32)

    bias_tile = lambda h, back: bias_ref[h, back]

    def accumulate(h, ki, s, bias, first=False):
        if callable(bias):
            m_blk = jnp.max(s + bias(), axis=0, keepdims=True)
        else:
            m_blk = jnp.max(s, axis=0, keepdims=True) + bias
        if first:
            m_new = m_blk
        else:
            m_old = m_ref[h]
            m_new = jnp.maximum(m_old, m_blk)
        pv = weighted_values(h, ki, jnp.exp2(s - (m_new - (bias() if callable(bias) else bias))))
        acc_ref[h] = pv if first else jnp.exp2(m_old - m_new) * acc_ref[h] + pv
        m_ref[h] = m_new

    @pl.when(qi == 0)
    def _():
        s_next = scores(0, 0)
        for h in range(n_heads):
            s_cur = s_next
            if h + 1 < n_heads:
                s_next = scores(h + 1, 0)
            accumulate(h, 0, s_cur, functools.partial(bias_tile, h, 0), first=True)

    @pl.when(qi >= 1)
    def _():
        steps = [(h, 0) for h in range(n_heads)] + [(h, 1) for h in range(n_heads)]
        s_next = scores(0, qi)
        for i, (h, back) in enumerate(steps):
            s_cur = s_next
            s_next = scores(steps[i + 1][0], qi - steps[i + 1][1]) if i + 1 < len(steps) else scores(0, 0)
            accumulate(h, qi - back, s_cur, functools.partial(bias_tile, h, back), first=back == 0)
        s_ref[...] = s_next

    far_bias = [rb_ref[far_bucket, h] * LOG2_E for h in range(n_heads)]

    def far_blocks(ki0, n_blocks):
        s_next = s_ref[...]
        for ki in [ki0 + j for j in range(n_blocks)]:
            for h in range(n_heads):
                s_cur = s_next
                s_next = scores(h + 1, ki) if h + 1 < n_heads else scores(0, ki + 1)
                accumulate(h, ki, s_cur, far_bias[h])
        s_ref[...] = s_next

    def far_pair(i, carry):
        far_blocks(2 * i, 2)
        return carry

    n_far = jnp.maximum(qi - 1, 0)
    lax.fori_loop(0, n_far // 2, far_pair, 0)

    @pl.when(n_far % 2 == 1)
    def _():
        far_blocks(n_far - 1, 1)

    lam = _lambda_value(lq1_ref, lk1_ref, lq2_ref, lk2_ref, lam_init)
    for h in range(n_heads):
        o = acc_ref[h, :LANES, :] / acc_ref[h, LANES:LANES + 1, :]
        od = o[:, :t] - lam * o[:, t:]
        od = od * lax.rsqrt(jnp.mean(od * od, axis=0, keepdims=True) + SUBLN_EPS)
        od = od * g_ref[...] * (1.0 - lam_init)
        o_ref[:, head(h)] = od.T.astype(o_ref.dtype)


def _diff_prompt_attention(qd16, kd16, vdt16, bias_tiles, rel_bias, lam_params, subln_g, *, batch, seq, lam_init):
    n, width = qd16.shape
    n_heads = width // LANES
    t = ATTN_TILE
    nq = seq // t
    n_buckets = rel_bias.shape[0]
    assert int(_rel_bucket_np(np.array([-(t + 1)]), n_buckets)[0]) == n_buckets - 1
    small = lambda b, i: (0, 0)
    return pl.pallas_call(
        functools.partial(_diff_prompt_kernel, t=t, far_bucket=n_buckets - 1, lam_init=lam_init),
        out_shape=jax.ShapeDtypeStruct((n, width), BF16),
        grid=(batch, nq),
        in_specs=[pl.BlockSpec(memory_space=pltpu.SMEM)]
                 + [pl.BlockSpec((1, HEAD_DIM), small)] * 4
                 + [pl.BlockSpec((LANES, 1), small),
                    pl.BlockSpec((t, width), lambda b, i: (b * nq + i, 0)),
                    pl.BlockSpec((seq, width), lambda b, i: (b, 0)),
                    pl.BlockSpec((width, seq), lambda b, i: (0, b)),
                    pl.BlockSpec((n_heads, 2, t, 2 * t), lambda b, i: (0, 0, 0, 0))],
        out_specs=pl.BlockSpec((t, width), lambda b, i: (b * nq + i, 0)),
        scratch_shapes=[pltpu.VMEM((n_heads, 1, 2 * t), F32),
                        pltpu.VMEM((n_heads, LANES + SUM_ROWS, 2 * t), F32), pltpu.VMEM((t, 2 * t), F32)],
        compiler_params=_params(("parallel", "arbitrary")),
        name="diff_attention_prompt",
    )(rel_bias.astype(F32), *lam_params, subln_g.reshape(LANES, 1), qd16, kd16, vdt16, bias_tiles)


def _log_sigmoid_pair(z):
    soft = jnp.log(1.0 + jnp.exp(-jnp.abs(z)))
    return jnp.minimum(z, 0.0) - soft, jnp.minimum(-z, 0.0) - soft


def _split_hi_lo(x):
    hi = x.astype(BF16)
    lo = (x - hi.astype(F32)).astype(BF16)
    return hi, lo


def _sb_prompt_kernel(q_ref, k_ref, vt_ref, o_ref, used_ref, acc_ref, z_ref, *, t):
    qi = pl.program_id(1)
    n_pairs = q_ref.shape[1] // LANES
    pair = lambda p: slice(p * LANES, (p + 1) * LANES)
    qq = [_split_heads_rows(q_ref[:, pair(p)]) for p in range(n_pairs)]
    row = lax.broadcasted_iota(jnp.int32, (t, t), 0)
    colk = lax.broadcasted_iota(jnp.int32, (t, t), 1)
    later = (colk > row).astype(BF16)
    later2 = jnp.concatenate([later, later], axis=1)

    def logits(p, ki):
        k = k_ref[pl.ds(pl.multiple_of(ki * t, t), t), pair(p)]
        return lax.dot_general(k, qq[p], NT_DIMS, preferred_element_type=F32)

    def stage_a(z, mask):
        soft = jnp.maximum(z, 0.0) + jnp.log2(1.0 + jnp.exp2(-jnp.abs(z)))
        log_beta = z - soft
        if mask is not None:
            soft = jnp.where(mask, soft, 0.0)
        hi, lo = _split_hi_lo(soft)
        after = jnp.dot(later2, jnp.concatenate([hi, lo], axis=0), preferred_element_type=F32)
        return log_beta, after, after[0:1, :] + soft[0:1, :]

    def stage_b(p, ki, mask, log_beta, after, total):
        used = used_ref[p]
        a = jnp.exp2(log_beta - after - used)
        if mask is not None:
            a = jnp.where(mask, a, 0.0)
        vt = vt_ref[pair(p), pl.ds(pl.multiple_of(ki * t, t), t)]
        acc_ref[p] += jnp.dot(vt, a.astype(BF16), preferred_element_type=F32)
        used_ref[p] = used + total

    def run_block(ki, z_first, mask):
        older = jnp.maximum(ki - 1, 0)
        z_next = z_first
        pending = None
        for p in range(n_pairs):
            z_cur = z_next
            z_next = logits(p + 1, ki) if p + 1 < n_pairs else logits(0, older)
            staged = stage_a(z_cur, mask)
            if pending is not None:
                stage_b(*pending)
            pending = (p, ki, mask) + staged
        stage_b(*pending)
        return z_next

    used_ref[...] = jnp.zeros_like(used_ref)
    acc_ref[...] = jnp.zeros_like(acc_ref)
    key = lax.broadcasted_iota(jnp.int32, (t, 2 * t), 0)
    col = lax.broadcasted_iota(jnp.int32, (t, 2 * t), 1)
    z_ref[...] = run_block(qi, logits(0, qi), key < jnp.where(col >= t, col - t, col))

    def cond(c):
        ki, least_used = c
        return jnp.logical_and(ki >= 0, least_used < -SB_SKIP_LOG * LOG2_E)

    def body(c):
        ki, _ = c
        z_ref[...] = run_block(ki, z_ref[...], None)
        return ki - 1, jnp.min(used_ref[...])

    lax.while_loop(cond, body, (qi - 1, jnp.min(used_ref[...])))

    for p in range(n_pairs):
        o = jnp.concatenate([acc_ref[p, :HEAD_DIM, :t], acc_ref[p, HEAD_DIM:, t:]], axis=0)
        o_ref[:, pair(p)] = o.T.astype(o_ref.dtype)


def _sb_prompt_attention(qs16, ks16, vst16, *, batch, seq):
    n, width = qs16.shape
    n_pairs = width // LANES
    t = ATTN_TILE
    nq = seq // t
    return pl.pallas_call(
        functools.partial(_sb_prompt_kernel, t=t),
        out_shape=jax.ShapeDtypeStruct((n, width), BF16),
        grid=(batch, nq),
        in_specs=[pl.BlockSpec((t, width), lambda b, i: (b * nq + i, 0)),
                  pl.BlockSpec((seq, width), lambda b, i: (b, 0)),
                  pl.BlockSpec((width, seq), lambda b, i: (0, b))],
        out_specs=pl.BlockSpec((t, width), lambda b, i: (b * nq + i, 0)),
        scratch_shapes=[pltpu.VMEM((n_pairs, 1, 2 * t), F32), pltpu.VMEM((n_pairs, LANES, 2 * t), F32),
                        pltpu.VMEM((t, 2 * t), F32)],
        compiler_params=_params(("parallel", "arbitrary")),
        name="stick_breaking_attention_prompt",
    )(qs16, ks16, vst16)


def _block_rows(q, group_w):
    t, w = q.shape
    n_groups = w // group_w
    qt = jnp.concatenate([q] * n_groups, axis=0)
    row_g = lax.broadcasted_iota(jnp.int32, qt.shape, 0) // t
    col_g = lax.broadcasted_iota(jnp.int32, qt.shape, 1) // group_w
    return jnp.where(row_g == col_g, qt, 0.0).astype(BF16)


def _pad_rows(x, rows):
    return jnp.concatenate([x, jnp.zeros((rows - x.shape[0], x.shape[1]), x.dtype)], axis=0)


def _gather_head_lanes(acc, n_groups, t):
    w = acc.shape[1]
    col_g = lax.broadcasted_iota(jnp.int32, (t, w), 1) // (w // n_groups)
    out = jnp.zeros((t, w), acc.dtype)
    for g in range(n_groups):
        out = jnp.where(col_g == g, acc[g * t:(g + 1) * t, :], out)
    return out


def _diff_sample_kernel(pt_ref, lq1_ref, lk1_ref, lq2_ref, lk2_ref, g_ref, q_ref, kvn_ref, bias_ref, *rest,
                        n_pages_step, page, half_w, t_new, lam_init, row0, slot_rows):
    page_refs = rest[:n_pages_step]
    o_ref, qb_ref, m_ref, l_ref, acc_ref = rest[n_pages_step:]
    j = pl.program_id(1)
    n_steps = pl.num_programs(1)
    n_heads = half_w // LANES

    def accumulate(k, v, bias, mask):
        s = lax.dot_general(qb_ref[...], k, NT_DIMS, preferred_element_type=F32) + bias
        if mask is not None:
            s = jnp.where(mask, s, -jnp.inf)
        m_old = m_ref[...]
        m_new = jnp.maximum(m_old, jnp.max(s, axis=1, keepdims=True))
        alpha = jnp.exp(m_old - m_new)
        p = jnp.exp(s - m_new)
        l_ref[...] = alpha * l_ref[...] + jnp.sum(p, axis=1, keepdims=True)
        acc_ref[...] = alpha * acc_ref[...] + jnp.dot(p.astype(BF16), v, preferred_element_type=F32)
        m_ref[...] = m_new

    @pl.when(j == 0)
    def _():
        qb_ref[...] = _block_rows(q_ref[...], HEAD_DIM)
        rows = qb_ref.shape[0]
        m_ref[...] = jnp.full(m_ref.shape, -jnp.inf, F32)
        l_ref[...] = jnp.zeros_like(l_ref)
        acc_ref[...] = jnp.zeros_like(acc_ref)
        kvn = _pad_rows(kvn_ref[...], page)
        tq = lax.broadcasted_iota(jnp.int32, (rows, page), 0) % t_new
        tk = lax.broadcasted_iota(jnp.int32, (rows, page), 1)
        accumulate(kvn[:, :half_w].astype(BF16), kvn[:, half_w:].astype(BF16),
                   bias_ref[:, 2 * page:3 * page], tk <= tq)

    def heads(ref, first):
        parts = [ref[0, pl.ds(row0 + first + h, page, stride=slot_rows), :] for h in range(n_heads)]
        return jnp.concatenate(parts, axis=1).astype(BF16)

    far = bias_ref[:, 0:page]
    last = jnp.where(j == n_steps - 1, bias_ref[:, page:2 * page], far)
    accumulate(jnp.concatenate([heads(r, 0) for r in page_refs], axis=0),
               jnp.concatenate([heads(r, n_heads) for r in page_refs], axis=0),
               jnp.concatenate([far] * (n_pages_step - 1) + [last], axis=1), None)

    @pl.when(j == n_steps - 1)
    def _():
        o = acc_ref[...] / l_ref[...]
        o = _gather_head_lanes(o, n_heads, 2 * t_new)
        lam = _lambda_value(lq1_ref, lk1_ref, lq2_ref, lk2_ref, lam_init)
        od = o[:t_new] - lam * o[t_new:]
        outs = []
        for h in range(n_heads):
            oh = od[:, h * LANES:(h + 1) * LANES]
            oh = oh * lax.rsqrt(jnp.mean(oh * oh, axis=-1, keepdims=True) + SUBLN_EPS)
            outs.append(oh * g_ref[...] * (1.0 - lam_init))
        o_ref[...] = jnp.concatenate(outs, axis=1).astype(o_ref.dtype)


def _diff_sample_attention(q32, kvn32, cache_rows, layer, depth, page_table, bias_tab, lam_params, subln_g, *,
                           t_new, lam_init):
    n, half_w = q32.shape
    n_seq, n_pages = page_table.shape
    n_heads = half_w // LANES
    slot_rows = depth * 2 * n_heads
    page = cache_rows.shape[1] // slot_rows
    gp = PAGES_PER_STEP
    rows = n_heads * 2 * t_new
    small = lambda b, j, pt: (0, 0)
    seq_row = lambda b, j, pt: (b, 0)

    def page_spec(g):
        return pl.BlockSpec((1, page * slot_rows, LANES), lambda b, j, pt: (pt[b, j * gp + g], 0, 0))

    return pl.pallas_call(
        functools.partial(_diff_sample_kernel, n_pages_step=gp, page=page, half_w=half_w, t_new=t_new,
                          lam_init=lam_init, row0=layer * 2 * n_heads, slot_rows=slot_rows),
        out_shape=jax.ShapeDtypeStruct((n, half_w), F32),
        grid_spec=pltpu.PrefetchScalarGridSpec(
            num_scalar_prefetch=1,
            grid=(n_seq, n_pages // gp),
            in_specs=[pl.BlockSpec((1, HEAD_DIM), small)] * 4
                     + [pl.BlockSpec((1, LANES), small),
                        pl.BlockSpec((t_new, half_w), seq_row),
                        pl.BlockSpec((t_new, 2 * half_w), seq_row),
                        pl.BlockSpec((rows, 3 * page), small)]
                     + [page_spec(g) for g in range(gp)],
            out_specs=pl.BlockSpec((t_new, half_w), seq_row),
            scratch_shapes=[pltpu.VMEM((rows, half_w), BF16), pltpu.VMEM((rows, 1), F32),
                            pltpu.VMEM((rows, 1), F32), pltpu.VMEM((rows, half_w), F32)]),
        compiler_params=_params(("parallel", "arbitrary")),
        name="diff_attention_sample",
    )(page_table, *lam_params, subln_g.reshape(1, LANES), q32, kvn32, bias_tab, *([cache_rows] * gp))


def _sb_sample_kernel(pt_ref, q_ref, kvn_ref, cache_ref, o_ref, qb_ref, run_ref, acc_ref, buf_ref, sem_ref, *,
                      n_pages, page, half_w, t_new, row0):
    b = pl.program_id(0)

    def page_copy(j, slot):
        return pltpu.make_async_copy(cache_ref.at[pt_ref[b, j], pl.ds(row0, 2 * half_w), :],
                                     buf_ref.at[slot], sem_ref.at[slot])

    slot_of = lambda j: (n_pages - 1 - j) & 1
    page_copy(n_pages - 1, 0).start()
    row = lax.broadcasted_iota(jnp.int32, (page, page), 0)
    colk = lax.broadcasted_iota(jnp.int32, (page, page), 1)
    later = (row > colk).astype(BF16)
    later2 = jnp.concatenate([later, later], axis=0)

    def accumulate(z, pv, mask):
        log_beta, log_rest = _log_sigmoid_pair(z)
        if mask is not None:
            log_rest = jnp.where(mask, log_rest, 0.0)
        hi, lo = _split_hi_lo(log_rest)
        suffix = jnp.dot(jnp.concatenate([hi, lo], axis=1), later2, preferred_element_type=F32)
        a = jnp.exp(log_beta + suffix + run_ref[...])
        if mask is not None:
            a = jnp.where(mask, a, 0.0)
        acc_ref[...] += pv(a.astype(BF16))
        run_ref[...] += jnp.sum(log_rest, axis=1, keepdims=True)

    qb_ref[...] = _block_rows(q_ref[...], HEAD_DIM)
    rows = qb_ref.shape[0]
    run_ref[...] = jnp.zeros_like(run_ref)
    acc_ref[...] = jnp.zeros_like(acc_ref)
    kvn = _pad_rows(kvn_ref[...], page)
    k_new = kvn[:, :half_w].astype(BF16)
    v_new = kvn[:, half_w:].astype(BF16)
    tq = lax.broadcasted_iota(jnp.int32, (rows, page), 0) % t_new
    tk = lax.broadcasted_iota(jnp.int32, (rows, page), 1)
    accumulate(lax.dot_general(qb_ref[...], k_new, NT_DIMS, preferred_element_type=F32),
               lambda a: jnp.dot(a, v_new, preferred_element_type=F32), tk < tq)

    def cond(c):
        j, live = c
        return jnp.logical_and(j >= 0, live > SB_SKIP_LOG)

    def body(c):
        j, _ = c
        slot = slot_of(j)
        page_copy(j, slot).wait()

        @pl.when(j >= 1)
        def _():
            page_copy(j - 1, 1 - slot).start()

        kt = buf_ref[slot, :half_w, :].astype(BF16)
        vt = buf_ref[slot, half_w:, :].astype(BF16)
        accumulate(jnp.dot(qb_ref[...], kt, preferred_element_type=F32),
                   lambda a: lax.dot_general(a, vt, NT_DIMS, preferred_element_type=F32), None)
        return j - 1, jnp.max(run_ref[...])

    j_end, _ = lax.while_loop(cond, body, (n_pages - 1, jnp.max(run_ref[...])))

    @pl.when(j_end >= 0)
    def _():
        page_copy(j_end, slot_of(j_end)).wait()

    n_heads = half_w // HEAD_DIM
    o_ref[...] = _gather_head_lanes(acc_ref[...], n_heads, t_new).astype(o_ref.dtype)


def _sb_sample_attention(q32, kvn32, cache_t, layer, page_table, *, t_new):
    n, half_w = q32.shape
    n_seq, n_pages = page_table.shape
    page = cache_t.shape[2]
    rows = (half_w // HEAD_DIM) * t_new
    seq_row = lambda b, pt: (b, 0)
    return pl.pallas_call(
        functools.partial(_sb_sample_kernel, n_pages=n_pages, page=page, half_w=half_w, t_new=t_new,
                          row0=layer * 2 * half_w),
        out_shape=jax.ShapeDtypeStruct((n, half_w), F32),
        grid_spec=pltpu.PrefetchScalarGridSpec(
            num_scalar_prefetch=1,
            grid=(n_seq,),
            in_specs=[pl.BlockSpec((t_new, half_w), seq_row),
                      pl.BlockSpec((t_new, 2 * half_w), seq_row),
                      pl.BlockSpec(memory_space=pl.ANY)],
            out_specs=pl.BlockSpec((t_new, half_w), seq_row),
            scratch_shapes=[pltpu.VMEM((rows, half_w), BF16), pltpu.VMEM((rows, 1), F32),
                            pltpu.VMEM((rows, half_w), F32), pltpu.VMEM((2, 2 * half_w, page), F32),
                            pltpu.SemaphoreType.DMA((2,))]),
        compiler_params=_params(("arbitrary",)),
        name="stick_breaking_attention_sample",
    )(page_table, q32, kvn32, cache_t)


def _router_gates(logits, n_groups, n_experts):
    lane = lax.broadcasted_iota(jnp.int32, logits.shape, 1).astype(F32)
    neg = -jnp.inf
    big = float(LANES)
    is_group = lane < n_groups
    gl = jnp.where(is_group, logits, neg)
    gmax = jnp.max(gl, axis=1, keepdims=True)
    pg_sel = 1.0 / jnp.sum(jnp.where(is_group, jnp.exp(gl - gmax), 0.0), axis=1, keepdims=True)
    g = jnp.min(jnp.where(gl == gmax, lane, big), axis=1, keepdims=True)
    lo = n_groups + g * n_experts
    in_group = jnp.logical_and(lane >= lo, lane < lo + n_experts)
    el = jnp.where(in_group, logits, neg)
    v1 = jnp.max(el, axis=1, keepdims=True)
    i1 = jnp.min(jnp.where(el == v1, lane, big), axis=1, keepdims=True)
    el2 = jnp.where(lane == i1, neg, el)
    v2 = jnp.max(el2, axis=1, keepdims=True)
    i2 = jnp.min(jnp.where(el2 == v2, lane, big), axis=1, keepdims=True)
    e2 = jnp.exp(v2 - v1)
    w1 = 1.0 / (1.0 + e2)
    w2 = e2 * w1
    return pg_sel * (jnp.where(lane == i1, w1, 0.0) + jnp.where(lane == i2, w2, 0.0))


def _post_kernel(x_ref, od_ref, os_ref, wo_ref, gf_ref, wr_ref, br_ref, wg_ref, wu_ref, wd_ref, gfin_ref,
                 y_ref, act_ref, *, n_groups, n_experts, final_norm):
    half = od_ref.shape[1]
    attn = (jnp.dot(od_ref[...], wo_ref[:half, :], preferred_element_type=F32)
            + jnp.dot(os_ref[...], wo_ref[half:, :], preferred_element_type=F32))
    x2 = x_ref[...] + attn
    hf = x2 * lax.rsqrt(jnp.mean(x2 * x2, axis=-1, keepdims=True) + NORM_EPS) * gf_ref[...]
    h, h_lo = _split_hi_lo(hf)
    both = jnp.dot(h, wr_ref[...], preferred_element_type=F32)
    logits = (both[:, :LANES] + both[:, LANES:]
              + jnp.dot(h_lo, wr_ref[:, :LANES], preferred_element_type=F32) + br_ref[...])
    gates = _router_gates(logits, n_groups, n_experts)

    n_total, _, d_exp = wg_ref.shape
    hidden = lambda e: (jnp.dot(h, wg_ref[e], preferred_element_type=F32),
                        jnp.dot(h, wu_ref[e], preferred_element_type=F32))
    nxt = hidden(0)
    for e in range(n_total):
        hg, hu = nxt
        if e + 1 < n_total:
            nxt = hidden(e + 1)
        gate = gates[:, n_groups + e:n_groups + e + 1]
        act_ref[:, e * d_exp:(e + 1) * d_exp] = (hg * (1.0 / (1.0 + jnp.exp(-hg))) * hu * gate).astype(BF16)
    x3 = x2 + jnp.dot(act_ref[...], wd_ref[...], preferred_element_type=F32)
    if final_norm:
        x3 = x3 * lax.rsqrt(jnp.mean(x3 * x3, axis=-1, keepdims=True) + NORM_EPS) * gfin_ref[...]
    y_ref[...] = x3


def _post_attention(x2d, od16, os16, wo16, g_ffn, wr16, b_router, wg16, wu16, wd16, g_final, *,
                    tm, n_groups, n_experts, final_norm):
    n, d = x2d.shape
    half = od16.shape[1]
    n_total, _, d_exp = wg16.shape
    row = lambda i: (i, 0)
    once = dict(pipeline_mode=pl.Buffered(1))
    const2 = lambda i: (0, 0)
    const3 = lambda i: (0, 0, 0)
    return pl.pallas_call(
        functools.partial(_post_kernel, n_groups=n_groups, n_experts=n_experts, final_norm=final_norm),
        out_shape=jax.ShapeDtypeStruct((n, d), F32),
        grid=(n // tm,),
        in_specs=[pl.BlockSpec((tm, d), row), pl.BlockSpec((tm, half), row), pl.BlockSpec((tm, half), row),
                  pl.BlockSpec((d, d), const2, **once), pl.BlockSpec((1, d), const2),
                  pl.BlockSpec((d, 2 * LANES), const2, **once), pl.BlockSpec((1, LANES), const2),
                  pl.BlockSpec((n_total, d, d_exp), const3, **once),
                  pl.BlockSpec((n_total, d, d_exp), const3, **once),
                  pl.BlockSpec((n_total * d_exp, d), const2, **once),
                  pl.BlockSpec((1, d), const2)],
        out_specs=pl.BlockSpec((tm, d), row),
        scratch_shapes=[pltpu.VMEM((tm, n_total * d_exp), BF16)],
        compiler_params=_params(("parallel",)),
        name="out_proj_hmoe",
    )(x2d, od16, os16, wo16, g_ffn.reshape(1, d), wr16, b_router, wg16, wu16, wd16, g_final.reshape(1, d))


def _row_tile(n, pref):
    tm = min(n, pref)
    while n % tm:
        tm //= 2
    return tm


def kernel(x_prompt, x_sample, cache_kv_diff, cache_kv_sb, page_table, rel_bias, g_mix, w_in, lambda_q1,
           lambda_k1, lambda_q2, lambda_k2, subln_g, w_out, g_ffn, w_group, b_group, w_erouter, b_erouter,
           w_gate, w_up, w_down, g_final):
    batch, seq, d = x_prompt.shape
    n_seq, t_new, _ = x_sample.shape
    n_pool, page, depth, _, h_diff, diff_hw = cache_kv_diff.shape
    _, _, _, _, h_sb, sb_hw = cache_kv_sb.shape
    n_pages = page_table.shape[1]
    past_len = n_pages * page
    n_groups, n_experts = w_erouter.shape[2], w_erouter.shape[3]
    half_w = h_diff * diff_hw
    assert diff_hw == 2 * HEAD_DIM == LANES and sb_hw == HEAD_DIM and h_sb * sb_hw == half_w
    assert seq % ATTN_TILE == 0 and n_pages % PAGES_PER_STEP == 0 and page == LANES
    assert n_groups + n_groups * n_experts <= LANES

    xp = x_prompt.reshape(batch * seq, d)
    xs = x_sample.reshape(n_seq * t_new, d)
    cache_d = cache_kv_diff.reshape(n_pool, page * depth * 2 * h_diff, diff_hw)
    cache_s = jnp.transpose(cache_kv_sb, (0, 2, 3, 4, 5, 1)).reshape(n_pool, depth * 2 * half_w, page)
    bias_tiles = _prompt_bias_tiles(rel_bias, ATTN_TILE)
    bias_tab = _sample_bias_table(rel_bias, past_len, page, t_new)
    tm_p = _row_tile(seq, 512)
    tm_s = _row_tile(n_seq * t_new, 512)
    tm_moe_p = _row_tile(batch * seq, 512)

    kvd_p, kvs_p, kvd_s, kvs_s = [], [], [], []
    for l in range(depth):
        li = _lambda_init(l)
        lam_params = [a[l].reshape(1, HEAD_DIM).astype(F32) for a in (lambda_q1, lambda_k1, lambda_q2, lambda_k2)]
        w16 = w_in[l].astype(BF16)
        wo16 = w_out[l].astype(BF16)
        wr = jnp.concatenate([w_group[l].astype(F32), w_erouter[l].astype(F32).reshape(d, n_groups * n_experts)],
                             axis=1)
        wr = jnp.pad(wr, ((0, 0), (0, LANES - wr.shape[1])))
        wr_hi = wr.astype(BF16)
        wr16 = jnp.concatenate([wr_hi, (wr - wr_hi.astype(F32)).astype(BF16)], axis=1)
        br = jnp.concatenate([b_group[l].astype(F32), b_erouter[l].astype(F32).reshape(-1)])
        br = jnp.pad(br, (0, LANES - br.shape[0])).reshape(1, LANES)
        d_exp = w_gate.shape[-1]
        wg16 = w_gate[l].astype(BF16).reshape(n_groups * n_experts, d, d_exp)
        wu16 = w_up[l].astype(BF16).reshape(n_groups * n_experts, d, d_exp)
        wd16 = w_down[l].astype(BF16).reshape(n_groups * n_experts * d_exp, d)
        last = l == depth - 1

        qd16, qs16, kvd, kvst, kd16, vdt16, ks16, vst16 = _project_prompt(xp, g_mix[l], w16, tm=tm_p, batch=batch,
                                                                         seq=seq)
        od16 = _diff_prompt_attention(qd16, kd16, vdt16, bias_tiles, rel_bias, lam_params, subln_g[l],
                                      batch=batch, seq=seq, lam_init=li)
        os16 = _sb_prompt_attention(qs16, ks16, vst16, batch=batch, seq=seq)
        xp = _post_attention(xp, od16, os16, wo16, g_ffn[l], wr16, br, wg16, wu16, wd16, g_final,
                             tm=tm_moe_p, n_groups=n_groups, n_experts=n_experts, final_norm=last)
        kvd_p.append(kvd.reshape(batch, seq, 2, h_diff, diff_hw))
        kvs_p.append(jnp.transpose(kvst.reshape(batch, 2, h_sb, sb_hw, seq), (0, 4, 1, 2, 3)))

        qd32, qs32, kvd, kvs = _project_sample(xs, g_mix[l], w16, tm=tm_s)
        od = _diff_sample_attention(qd32, kvd, cache_d, l, depth, page_table, bias_tab, lam_params, subln_g[l],
                                    t_new=t_new, lam_init=li)
        osb = _sb_sample_attention(qs32, kvs, cache_s, l, page_table, t_new=t_new)
        xs = _post_attention(xs, od.astype(BF16), osb.astype(BF16), wo16, g_ffn[l], wr16, br, wg16, wu16,
                             wd16, g_final, tm=tm_s, n_groups=n_groups, n_experts=n_experts, final_norm=last)
        kvd_s.append(kvd.reshape(n_seq, t_new, 2, h_diff, diff_hw))
        kvs_s.append(kvs.reshape(n_seq, t_new, 2, h_sb, sb_hw))

    y_prompt = xp.reshape(batch, seq, d)
    y_sample = xs.reshape(n_seq, t_new, d)
    return (y_prompt, y_sample, jnp.stack(kvd_p, axis=2), jnp.stack(kvs_p, axis=2),
            jnp.stack(kvd_s, axis=2), jnp.stack(kvs_s, axis=2))
```

```python
import functools
import math

import numpy as np
import jax
import jax.numpy as jnp
from jax import lax
from jax.experimental import pallas as pl
from jax.experimental.pallas import tpu as pltpu

HEAD_DIM = 64
MAX_DISTANCE = 128
NORM_EPS = 1e-6
SUBLN_EPS = 1e-5
TOP_K_INNER = 2

LANES = 128
VMEM_LIMIT_BYTES = 56 * 1024 * 1024

ATTN_TILE = 256
SUM_ROWS = 16
SB_SKIP_LOG = -90.0
PAGES_PER_STEP = 16

F32 = jnp.float32
BF16 = jnp.bfloat16
NT_DIMS = (((1,), (1,)), ((), ()))
LOG2_E = math.log2(math.e)


def _lambda_init(layer):
    return 0.8 - 0.6 * math.exp(-0.3 * layer)


def _rel_bucket_np(rel, n_buckets):
    n = np.maximum(-rel, 0)
    max_exact = n_buckets // 2
    nf = np.maximum(n, 1).astype(np.float32)
    large = max_exact + (np.log(nf / np.float32(max_exact)) / np.float32(math.log(MAX_DISTANCE / max_exact))
                         * np.float32(n_buckets - max_exact)).astype(np.int32)
    large = np.minimum(large, n_buckets - 1)
    return np.where(n < max_exact, n, large).astype(np.int32)


def _params(sem):
    return pltpu.CompilerParams(dimension_semantics=sem, vmem_limit_bytes=VMEM_LIMIT_BYTES)


def _normed_proj_cols(x_ref, g_ref, w_ref, half_w):
    x = x_ref[...]
    ms = jnp.mean(x * x, axis=-1, keepdims=True)
    h = (x * lax.rsqrt(ms + NORM_EPS) * g_ref[...]).astype(BF16)
    return lambda i: jnp.dot(h, w_ref[:, i * half_w:(i + 1) * half_w], preferred_element_type=F32)


def _proj_prompt_kernel(x_ref, g_ref, w_ref, qd_ref, qs_ref, kvd_ref, kvst_ref, kd16_ref, vdt16_ref, ks16_ref,
                        vst16_ref, *, half_w, scale):
    cols = _normed_proj_cols(x_ref, g_ref, w_ref, half_w)
    tm = x_ref.shape[0]
    qd_ref[...] = (cols(0) * (scale * LOG2_E)).astype(BF16)
    qs_ref[...] = (cols(3) * (scale * LOG2_E)).astype(BF16)
    n_rows = 2 * half_w // LANES
    kd = cols(1)
    vd = cols(2)
    for r in range(n_rows // 2):
        kvd_ref[pl.ds(r, tm, stride=n_rows), :] = kd[:, r * LANES:(r + 1) * LANES]
        kvd_ref[pl.ds(n_rows // 2 + r, tm, stride=n_rows), :] = vd[:, r * LANES:(r + 1) * LANES]
    kd16_ref[...] = kd.astype(BF16)
    vdt16_ref[...] = vd.T.astype(BF16)
    ks = cols(4)
    ks16_ref[...] = ks.astype(BF16)
    kvst_ref[0, :half_w, :] = ks.T
    vst = cols(5).T
    kvst_ref[0, half_w:, :] = vst
    vst16_ref[...] = vst.astype(BF16)


def _project_prompt(x2d, g, w16, *, tm, batch, seq):
    n, d = x2d.shape
    half_w = w16.shape[1] // 6
    n_rows = 2 * half_w // LANES
    nbs = seq // tm
    row = lambda i: (i, 0)
    const = lambda i: (0, 0)
    colblk = lambda i: (0, i)
    out_shape = (jax.ShapeDtypeStruct((n, half_w), BF16), jax.ShapeDtypeStruct((n, half_w), BF16),
                 jax.ShapeDtypeStruct((n * n_rows, LANES), F32),
                 jax.ShapeDtypeStruct((batch, 2 * half_w, seq), F32),
                 jax.ShapeDtypeStruct((n, half_w), BF16), jax.ShapeDtypeStruct((half_w, n), BF16),
                 jax.ShapeDtypeStruct((n, half_w), BF16), jax.ShapeDtypeStruct((half_w, n), BF16))
    out_specs = (pl.BlockSpec((tm, half_w), row), pl.BlockSpec((tm, half_w), row),
                 pl.BlockSpec((tm * n_rows, LANES), row),
                 pl.BlockSpec((1, 2 * half_w, tm), lambda i: (i // nbs, 0, i % nbs)),
                 pl.BlockSpec((tm, half_w), row), pl.BlockSpec((half_w, tm), colblk),
                 pl.BlockSpec((tm, half_w), row), pl.BlockSpec((half_w, tm), colblk))
    return pl.pallas_call(
        functools.partial(_proj_prompt_kernel, half_w=half_w, scale=HEAD_DIM ** -0.5),
        out_shape=out_shape,
        grid=(n // tm,),
        in_specs=[pl.BlockSpec((tm, d), row), pl.BlockSpec((1, d), const),
                  pl.BlockSpec((d, 6 * half_w), const)],
        out_specs=out_specs,
        compiler_params=_params(("parallel",)),
        name="rmsnorm_qkv_proj_prompt",
    )(x2d, g.reshape(1, d), w16)


def _proj_sample_kernel(x_ref, g_ref, w_ref, qd_ref, qs_ref, kvd_ref, kvs_ref, *, half_w, scale):
    cols = _normed_proj_cols(x_ref, g_ref, w_ref, half_w)
    qd_ref[...] = cols(0) * scale
    qs_ref[...] = cols(3) * scale
    kvd_ref[:, :half_w] = cols(1)
    kvd_ref[:, half_w:] = cols(2)
    kvs_ref[:, :half_w] = cols(4)
    kvs_ref[:, half_w:] = cols(5)


def _project_sample(x2d, g, w16, *, tm):
    n, d = x2d.shape
    half_w = w16.shape[1] // 6
    row = lambda i: (i, 0)
    const = lambda i: (0, 0)
    return pl.pallas_call(
        functools.partial(_proj_sample_kernel, half_w=half_w, scale=HEAD_DIM ** -0.5),
        out_shape=(jax.ShapeDtypeStruct((n, half_w), F32), jax.ShapeDtypeStruct((n, half_w), F32),
                   jax.ShapeDtypeStruct((n, 2 * half_w), F32), jax.ShapeDtypeStruct((n, 2 * half_w), F32)),
        grid=(n // tm,),
        in_specs=[pl.BlockSpec((tm, d), row), pl.BlockSpec((1, d), const),
                  pl.BlockSpec((d, 6 * half_w), const)],
        out_specs=(pl.BlockSpec((tm, half_w), row), pl.BlockSpec((tm, half_w), row),
                   pl.BlockSpec((tm, 2 * half_w), row), pl.BlockSpec((tm, 2 * half_w), row)),
        compiler_params=_params(("parallel",)),
        name="rmsnorm_qkv_proj_sample",
    )(x2d, g.reshape(1, d), w16)


def _bias_select(bucket, rb_ref, h, n_buckets):
    out = jnp.zeros(bucket.shape, F32)
    for b in range(n_buckets):
        out = jnp.where(bucket == b, rb_ref[b, h], out)
    return out


def _prompt_bias_kernel(rb_ref, bucket_ref, out_ref, *, n_buckets):
    h = pl.program_id(0)
    t = bucket_ref.shape[1]
    for d in range(bucket_ref.shape[0]):
        tile = _bias_select(bucket_ref[d], rb_ref, h, n_buckets) * LOG2_E
        if d == 0:
            key = lax.broadcasted_iota(jnp.int32, tile.shape, 0)
            qry = lax.broadcasted_iota(jnp.int32, tile.shape, 1)
            tile = jnp.where(key <= qry, tile, -jnp.inf)
        out_ref[0, (1 - d) * t:(2 - d) * t, :] = jnp.concatenate([tile, tile], axis=1)


def _prompt_bias_tiles(rel_bias, t):
    n_buckets, n_heads = rel_bias.shape
    key = np.arange(t)[:, None]
    qry = np.arange(t)[None, :]
    bucket = np.stack([_rel_bucket_np(key - qry - d * t, n_buckets) for d in range(2)])
    return pl.pallas_call(
        functools.partial(_prompt_bias_kernel, n_buckets=n_buckets),
        out_shape=jax.ShapeDtypeStruct((n_heads, 2 * t, 2 * t), F32),
        grid=(n_heads,),
        in_specs=[pl.BlockSpec(memory_space=pltpu.SMEM),
                  pl.BlockSpec((2, t, t), lambda h: (0, 0, 0))],
        out_specs=pl.BlockSpec((1, 2 * t, 2 * t), lambda h: (h, 0, 0)),
        compiler_params=_params(("parallel",)),
        name="prompt_rel_bias_tiles",
    )(rel_bias.astype(F32), jnp.asarray(bucket))


def _sample_bias_kernel(rb_ref, bucket_ref, out_ref, *, n_buckets, n_heads):
    rows = bucket_ref.shape[0] // n_heads
    for h in range(n_heads):
        sl = slice(h * rows, (h + 1) * rows)
        out_ref[sl, :] = _bias_select(bucket_ref[sl, :], rb_ref, h, n_buckets)


def _sample_bias_table(rel_bias, past_len, page, t_new):
    n_buckets, n_heads = rel_bias.shape
    rows = n_heads * 2 * t_new
    q_pos = past_len + (np.arange(rows) % t_new)[:, None]
    far = np.full((rows, page), n_buckets - 1, np.int32)
    last = _rel_bucket_np((past_len - page + np.arange(page))[None, :] - q_pos, n_buckets)
    new = _rel_bucket_np((past_len + np.arange(page))[None, :] - q_pos, n_buckets)
    bucket = np.concatenate([far, last, new], axis=1)
    return pl.pallas_call(
        functools.partial(_sample_bias_kernel, n_buckets=n_buckets, n_heads=n_heads),
        out_shape=jax.ShapeDtypeStruct(bucket.shape, F32),
        in_specs=[pl.BlockSpec(memory_space=pltpu.SMEM), pl.BlockSpec(memory_space=pltpu.VMEM)],
        out_specs=pl.BlockSpec(memory_space=pltpu.VMEM),
        name="sample_rel_bias_table",
    )(rel_bias.astype(F32), jnp.asarray(bucket))


def _split_heads_rows(q):
    lane = lax.broadcasted_iota(jnp.int32, q.shape, 1)
    zero = jnp.zeros_like(q)
    return jnp.concatenate([jnp.where(lane < HEAD_DIM, q, zero), jnp.where(lane >= HEAD_DIM, q, zero)], axis=0)


def _lambda_value(lq1_ref, lk1_ref, lq2_ref, lk2_ref, lam_init):
    a = jnp.sum(lq1_ref[...] * lk1_ref[...], axis=-1, keepdims=True)
    b = jnp.sum(lq2_ref[...] * lk2_ref[...], axis=-1, keepdims=True)
    return jnp.exp(a) - jnp.exp(b) + lam_init


def _diff_prompt_kernel(rb_ref, lq1_ref, lk1_ref, lq2_ref, lk2_ref, g_ref, q_ref, k_ref, vt_ref, bias_ref,
                        o_ref, m_ref, acc_ref, s_ref, *, t, far_bucket, lam_init):
    qi = pl.program_id(1)
    n_heads = q_ref.shape[1] // LANES
    head = lambda h: slice(h * LANES, (h + 1) * LANES)
    qq = [_split_heads_rows(q_ref[:, head(h)]) for h in range(n_heads)]
    keys = lambda ki, nb: pl.ds(pl.multiple_of(ki * t, t), nb * t)

    def scores(h, ki, nb):
        return lax.dot_general(k_ref[keys(ki, nb), head(h)], qq[h], NT_DIMS,
                               preferred_element_type=F32)

    def accumulate(h, ki, nb, s, bias, first=False):
        if callable(bias):
            m_blk = jnp.max(s + bias(), axis=0, keepdims=True)
        else:
            m_blk = jnp.max(s, axis=0, keepdims=True) + bias
        if first:
            m_new = m_blk
        else:
            m_old = m_ref[h]
            m_new = jnp.maximum(m_old, m_blk)
        p = jnp.exp2(s - (m_new - (bias() if callable(bias) else bias)))
        vt1 = jnp.concatenate([vt_ref[head(h), keys(ki, nb)], jnp.ones((SUM_ROWS, nb * t), BF16)], axis=0)
        pv = jnp.dot(vt1, p.astype(BF16), preferred_element_type=F32)
        acc_ref[h] = pv if first else jnp.exp2(m_old - m_new) * acc_ref[h] + pv
        m_ref[h] = m_new

    def run_heads(ki, nb, s_first, bias, first, prefetch):
        s_next = s_first
        for h in range(n_heads):
            s_cur = s_next
            s_next = scores(h + 1, ki, nb) if h + 1 < n_heads else (prefetch() if prefetch else None)
            accumulate(h, ki, nb, s_cur, bias(h), first)
        return s_next

    @pl.when(qi == 0)
    def _():
        run_heads(0, 1, scores(0, 0, 1), lambda h: lambda: bias_ref[h, t:, :], True, None)

    @pl.when(qi >= 1)
    def _():
        s_ref[...] = run_heads(qi - 1, 2, scores(0, qi - 1, 2), lambda h: lambda: bias_ref[h], True,
                               lambda: scores(0, 0, 2))

    far_bias = [rb_ref[far_bucket, h] * LOG2_E for h in range(n_heads)]
    n_far = jnp.maximum(qi - 1, 0)

    def far_pair(i, carry):
        s_ref[...] = run_heads(2 * i, 2, s_ref[...], lambda h: far_bias[h], False,
                               lambda: scores(0, 2 * i + 2, 2))
        return carry

    lax.fori_loop(0, n_far // 2, far_pair, 0)

    @pl.when(n_far % 2 == 1)
    def _():
        run_heads(n_far - 1, 1, s_ref[:t, :], lambda h: far_bias[h], False, None)

    lam = _lambda_value(lq1_ref, lk1_ref, lq2_ref, lk2_ref, lam_init)
    for h in range(n_heads):
        o = acc_ref[h, :LANES, :] / acc_ref[h, LANES:LANES + 1, :]
        od = o[:, :t] - lam * o[:, t:]
        od = od * lax.rsqrt(jnp.mean(od * od, axis=0, keepdims=True) + SUBLN_EPS)
        od = od * g_ref[...] * (1.0 - lam_init)
        o_ref[:, head(h)] = od.T.astype(o_ref.dtype)


def _diff_prompt_attention(qd16, kd16, vdt16, bias_tiles, rel_bias, lam_params, subln_g, *, batch, seq, lam_init):
    n, width = qd16.shape
    n_heads = width // LANES
    t = ATTN_TILE
    nq = seq // t
    n_buckets = rel_bias.shape[0]
    assert int(_rel_bucket_np(np.array([-(t + 1)]), n_buckets)[0]) == n_buckets - 1
    small = lambda b, i: (0, 0)
    return pl.pallas_call(
        functools.partial(_diff_prompt_kernel, t=t, far_bucket=n_buckets - 1, lam_init=lam_init),
        out_shape=jax.ShapeDtypeStruct((n, width), BF16),
        grid=(batch, nq),
        in_specs=[pl.BlockSpec(memory_space=pltpu.SMEM)]
                 + [pl.BlockSpec((1, HEAD_DIM), small)] * 4
                 + [pl.BlockSpec((LANES, 1), small),
                    pl.BlockSpec((t, width), lambda b, i: (b * nq + i, 0)),
                    pl.BlockSpec((seq, width), lambda b, i: (b, 0)),
                    pl.BlockSpec((width, seq), lambda b, i: (0, b)),
                    pl.BlockSpec((n_heads, 2 * t, 2 * t), lambda b, i: (0, 0, 0))],
        out_specs=pl.BlockSpec((t, width), lambda b, i: (b * nq + i, 0)),
        scratch_shapes=[pltpu.VMEM((n_heads, 1, 2 * t), F32),
                        pltpu.VMEM((n_heads, LANES + SUM_ROWS, 2 * t), F32), pltpu.VMEM((2 * t, 2 * t), F32)],
        compiler_params=_params(("parallel", "arbitrary")),
        name="diff_attention_prompt",
    )(rel_bias.astype(F32), *lam_params, subln_g.reshape(LANES, 1), qd16, kd16, vdt16, bias_tiles)


def _log_sigmoid_pair(z):
    soft = jnp.log(1.0 + jnp.exp(-jnp.abs(z)))
    return jnp.minimum(z, 0.0) - soft, jnp.minimum(-z, 0.0) - soft


def _split_hi_lo(x):
    hi = x.astype(BF16)
    lo = (x - hi.astype(F32)).astype(BF16)
    return hi, lo


def _split_trunc(x):
    hi = lax.bitcast_convert_type(lax.bitcast_convert_type(x, jnp.uint32) & jnp.uint32(0xFFFF0000), F32)
    return hi.astype(BF16), (x - hi).astype(BF16)


def _neg_abs(x):
    return lax.bitcast_convert_type(lax.bitcast_convert_type(x, jnp.uint32) | jnp.uint32(0x80000000), F32)


def _sb_prompt_kernel(q_ref, k_ref, vt_ref, o_ref, used_ref, acc_ref, z_ref, *, t):
    qi = pl.program_id(1)
    n_pairs = q_ref.shape[1] // LANES
    pair = lambda p: slice(p * LANES, (p + 1) * LANES)
    qq = [_split_heads_rows(q_ref[:, pair(p)]) for p in range(n_pairs)]
    row = lax.broadcasted_iota(jnp.int32, (t, t), 0)
    colk = lax.broadcasted_iota(jnp.int32, (t, t), 1)
    later = (colk > row).astype(BF16)
    later2 = jnp.concatenate([later, later], axis=1)

    def logits(p, ki):
        k = k_ref[pl.ds(pl.multiple_of(ki * t, t), t), pair(p)]
        return lax.dot_general(k, qq[p], NT_DIMS, preferred_element_type=F32)

    def stage_a(z, mask):
        soft = jnp.maximum(z, 0.0) + jnp.log2(1.0 + jnp.exp2(_neg_abs(z)))
        log_beta = z - soft
        if mask is not None:
            soft = jnp.where(mask, soft, 0.0)
        hi, lo = _split_trunc(soft)
        after = jnp.dot(later2, jnp.concatenate([hi, lo], axis=0), preferred_element_type=F32)
        return log_beta, after, after[0:1, :] + soft[0:1, :]

    def stage_b(p, ki, mask, log_beta, after, total):
        used = used_ref[p]
        a = jnp.exp2(log_beta - after)
        if mask is not None:
            a = jnp.where(mask, a, 0.0)
        vt = vt_ref[pair(p), pl.ds(pl.multiple_of(ki * t, t), t)]
        acc_ref[p] += jnp.dot(vt, a.astype(BF16), preferred_element_type=F32) * jnp.exp2(-used)
        used_ref[p] = used + total

    def run_steps(steps, z_first, prefetch_ki):
        z_next = z_first
        pending = None
        for i, (p, ki, mask) in enumerate(steps):
            z_cur = z_next
            z_next = logits(steps[i + 1][0], steps[i + 1][1]) if i + 1 < len(steps) else logits(0, prefetch_ki)
            staged = stage_a(z_cur, mask)
            if pending is not None:
                stage_b(*pending)
            pending = (p, ki, mask) + staged
        stage_b(*pending)
        return z_next

    used_ref[...] = jnp.zeros_like(used_ref)
    acc_ref[...] = jnp.zeros_like(acc_ref)
    key = lax.broadcasted_iota(jnp.int32, (t, 2 * t), 0)
    col = lax.broadcasted_iota(jnp.int32, (t, 2 * t), 1)
    causal = key < jnp.where(col >= t, col - t, col)
    diagonal = [(p, qi, causal) for p in range(n_pairs)]

    @pl.when(qi == 0)
    def _():
        z_ref[...] = run_steps(diagonal, logits(0, qi), 0)

    @pl.when(qi >= 1)
    def _():
        z_ref[...] = run_steps(diagonal + [(p, qi - 1, None) for p in range(n_pairs)], logits(0, qi),
                               jnp.maximum(qi - 2, 0))

    def cond(c):
        ki, least_used = c
        return jnp.logical_and(ki >= 0, least_used < -SB_SKIP_LOG * LOG2_E)

    def body(c):
        ki, _ = c
        z_ref[...] = run_steps([(p, ki, None) for p in range(n_pairs)], z_ref[...], jnp.maximum(ki - 1, 0))
        return ki - 1, jnp.min(used_ref[...])

    lax.while_loop(cond, body, (qi - 2, jnp.min(used_ref[...])))

    for p in range(n_pairs):
        o = jnp.concatenate([acc_ref[p, :HEAD_DIM, :t], acc_ref[p, HEAD_DIM:, t:]], axis=0)
        o_ref[:, pair(p)] = o.T.astype(o_ref.dtype)


def _sb_prompt_attention(qs16, ks16, vst16, *, batch, seq):
    n, width = qs16.shape
    n_pairs = width // LANES
    t = ATTN_TILE
    nq = seq // t
    return pl.pallas_call(
        functools.partial(_sb_prompt_kernel, t=t),
        out_shape=jax.ShapeDtypeStruct((n, width), BF16),
        grid=(batch, nq),
        in_specs=[pl.BlockSpec((t, width), lambda b, i: (b * nq + i, 0)),
                  pl.BlockSpec((seq, width), lambda b, i: (b, 0)),
                  pl.BlockSpec((width, seq), lambda b, i: (0, b))],
        out_specs=pl.BlockSpec((t, width), lambda b, i: (b * nq + i, 0)),
        scratch_shapes=[pltpu.VMEM((n_pairs, 1, 2 * t), F32), pltpu.VMEM((n_pairs, LANES, 2 * t), F32),
                        pltpu.VMEM((t, 2 * t), F32)],
        compiler_params=_params(("parallel", "arbitrary")),
        name="stick_breaking_attention_prompt",
    )(qs16, ks16, vst16)


def _block_rows(q, group_w):
    t, w = q.shape
    n_groups = w // group_w
    qt = jnp.concatenate([q] * n_groups, axis=0)
    row_g = lax.broadcasted_iota(jnp.int32, qt.shape, 0) // t
    col_g = lax.broadcasted_iota(jnp.int32, qt.shape, 1) // group_w
    return jnp.where(row_g == col_g, qt, 0.0).astype(BF16)


def _pad_rows(x, rows):
    return jnp.concatenate([x, jnp.zeros((rows - x.shape[0], x.shape[1]), x.dtype)], axis=0)


def _gather_head_lanes(acc, n_groups, t):
    w = acc.shape[1]
    col_g = lax.broadcasted_iota(jnp.int32, (t, w), 1) // (w // n_groups)
    out = jnp.zeros((t, w), acc.dtype)
    for g in range(n_groups):
        out = jnp.where(col_g == g, acc[g * t:(g + 1) * t, :], out)
    return out


def _diff_sample_kernel(pt_ref, lq1_ref, lk1_ref, lq2_ref, lk2_ref, g_ref, q_ref, kvn_ref, bias_ref, *rest,
                        n_pages_step, page, half_w, t_new, lam_init, row0, slot_rows):
    page_refs = rest[:n_pages_step]
    o_ref, qb_ref, m_ref, l_ref, acc_ref = rest[n_pages_step:]
    j = pl.program_id(1)
    n_steps = pl.num_programs(1)
    n_heads = half_w // LANES

    def accumulate(k, v, bias, mask):
        s = lax.dot_general(qb_ref[...], k, NT_DIMS, preferred_element_type=F32) + bias
        if mask is not None:
            s = jnp.where(mask, s, -jnp.inf)
        m_old = m_ref[...]
        m_new = jnp.maximum(m_old, jnp.max(s, axis=1, keepdims=True))
        alpha = jnp.exp(m_old - m_new)
        p = jnp.exp(s - m_new)
        l_ref[...] = alpha * l_ref[...] + jnp.sum(p, axis=1, keepdims=True)
        acc_ref[...] = alpha * acc_ref[...] + jnp.dot(p.astype(BF16), v, preferred_element_type=F32)
        m_ref[...] = m_new

    @pl.when(j == 0)
    def _():
        qb_ref[...] = _block_rows(q_ref[...], HEAD_DIM)
        rows = qb_ref.shape[0]
        m_ref[...] = jnp.full(m_ref.shape, -jnp.inf, F32)
        l_ref[...] = jnp.zeros_like(l_ref)
        acc_ref[...] = jnp.zeros_like(acc_ref)
        kvn = _pad_rows(kvn_ref[...], page)
        tq = lax.broadcasted_iota(jnp.int32, (rows, page), 0) % t_new
        tk = lax.broadcasted_iota(jnp.int32, (rows, page), 1)
        accumulate(kvn[:, :half_w].astype(BF16), kvn[:, half_w:].astype(BF16),
                   bias_ref[:, 2 * page:3 * page], tk <= tq)

    def heads(ref, first):
        parts = [ref[0, pl.ds(row0 + first + h, page, stride=slot_rows), :] for h in range(n_heads)]
        return jnp.concatenate(parts, axis=1).astype(BF16)

    far = bias_ref[:, 0:page]
    last = jnp.where(j == n_steps - 1, bias_ref[:, page:2 * page], far)
    accumulate(jnp.concatenate([heads(r, 0) for r in page_refs], axis=0),
               jnp.concatenate([heads(r, n_heads) for r in page_refs], axis=0),
               jnp.concatenate([far] * (n_pages_step - 1) + [last], axis=1), None)

    @pl.when(j == n_steps - 1)
    def _():
        o = acc_ref[...] / l_ref[...]
        o = _gather_head_lanes(o, n_heads, 2 * t_new)
        lam = _lambda_value(lq1_ref, lk1_ref, lq2_ref, lk2_ref, lam_init)
        od = o[:t_new] - lam * o[t_new:]
        outs = []
        for h in range(n_heads):
            oh = od[:, h * LANES:(h + 1) * LANES]
            oh = oh * lax.rsqrt(jnp.mean(oh * oh, axis=-1, keepdims=True) + SUBLN_EPS)
            outs.append(oh * g_ref[...] * (1.0 - lam_init))
        o_ref[...] = jnp.concatenate(outs, axis=1).astype(o_ref.dtype)


def _diff_sample_attention(q32, kvn32, cache_rows, layer, depth, page_table, bias_tab, lam_params, subln_g, *,
                           t_new, lam_init):
    n, half_w = q32.shape
    n_seq, n_pages = page_table.shape
    n_heads = half_w // LANES
    slot_rows = depth * 2 * n_heads
    page = cache_rows.shape[1] // slot_rows
    gp = PAGES_PER_STEP
    rows = n_heads * 2 * t_new
    small = lambda b, j, pt: (0, 0)
    seq_row = lambda b, j, pt: (b, 0)

    def page_spec(g):
        return pl.BlockSpec((1, page * slot_rows, LANES), lambda b, j, pt: (pt[b, j * gp + g], 0, 0))

    return pl.pallas_call(
        functools.partial(_diff_sample_kernel, n_pages_step=gp, page=page, half_w=half_w, t_new=t_new,
                          lam_init=lam_init, row0=layer * 2 * n_heads, slot_rows=slot_rows),
        out_shape=jax.ShapeDtypeStruct((n, half_w), F32),
        grid_spec=pltpu.PrefetchScalarGridSpec(
            num_scalar_prefetch=1,
            grid=(n_seq, n_pages // gp),
            in_specs=[pl.BlockSpec((1, HEAD_DIM), small)] * 4
                     + [pl.BlockSpec((1, LANES), small),
                        pl.BlockSpec((t_new, half_w), seq_row),
                        pl.BlockSpec((t_new, 2 * half_w), seq_row),
                        pl.BlockSpec((rows, 3 * page), small)]
                     + [page_spec(g) for g in range(gp)],
            out_specs=pl.BlockSpec((t_new, half_w), seq_row),
            scratch_shapes=[pltpu.VMEM((rows, half_w), BF16), pltpu.VMEM((rows, 1), F32),
                            pltpu.VMEM((rows, 1), F32), pltpu.VMEM((rows, half_w), F32)]),
        compiler_params=_params(("parallel", "arbitrary")),
        name="diff_attention_sample",
    )(page_table, *lam_params, subln_g.reshape(1, LANES), q32, kvn32, bias_tab, *([cache_rows] * gp))


def _sb_sample_kernel(pt_ref, q_ref, kvn_ref, cache_ref, o_ref, qb_ref, run_ref, acc_ref, buf_ref, sem_ref, *,
                      n_pages, page, half_w, t_new, row0):
    b = pl.program_id(0)

    def page_copy(j, slot):
        return pltpu.make_async_copy(cache_ref.at[pt_ref[b, j], pl.ds(row0, 2 * half_w), :],
                                     buf_ref.at[slot], sem_ref.at[slot])

    slot_of = lambda j: (n_pages - 1 - j) & 1
    page_copy(n_pages - 1, 0).start()
    row = lax.broadcasted_iota(jnp.int32, (page, page), 0)
    colk = lax.broadcasted_iota(jnp.int32, (page, page), 1)
    later = (row > colk).astype(BF16)
    later2 = jnp.concatenate([later, later], axis=0)

    def accumulate(z, pv, mask):
        log_beta, log_rest = _log_sigmoid_pair(z)
        if mask is not None:
            log_rest = jnp.where(mask, log_rest, 0.0)
        hi, lo = _split_hi_lo(log_rest)
        suffix = jnp.dot(jnp.concatenate([hi, lo], axis=1), later2, preferred_element_type=F32)
        a = jnp.exp(log_beta + suffix + run_ref[...])
        if mask is not None:
            a = jnp.where(mask, a, 0.0)
        acc_ref[...] += pv(a.astype(BF16))
        run_ref[...] += jnp.sum(log_rest, axis=1, keepdims=True)

    qb_ref[...] = _block_rows(q_ref[...], HEAD_DIM)
    rows = qb_ref.shape[0]
    run_ref[...] = jnp.zeros_like(run_ref)
    acc_ref[...] = jnp.zeros_like(acc_ref)
    kvn = _pad_rows(kvn_ref[...], page)
    k_new = kvn[:, :half_w].astype(BF16)
    v_new = kvn[:, half_w:].astype(BF16)
    tq = lax.broadcasted_iota(jnp.int32, (rows, page), 0) % t_new
    tk = lax.broadcasted_iota(jnp.int32, (rows, page), 1)
    accumulate(lax.dot_general(qb_ref[...], k_new, NT_DIMS, preferred_element_type=F32),
               lambda a: jnp.dot(a, v_new, preferred_element_type=F32), tk < tq)

    def cond(c):
        j, live = c
        return jnp.logical_and(j >= 0, live > SB_SKIP_LOG)

    def body(c):
        j, _ = c
        slot = slot_of(j)
        page_copy(j, slot).wait()

        @pl.when(j >= 1)
        def _():
            page_copy(j - 1, 1 - slot).start()

        kt = buf_ref[slot, :half_w, :].astype(BF16)
        vt = buf_ref[slot, half_w:, :].astype(BF16)
        accumulate(jnp.dot(qb_ref[...], kt, preferred_element_type=F32),
                   lambda a: lax.dot_general(a, vt, NT_DIMS, preferred_element_type=F32), None)
        return j - 1, jnp.max(run_ref[...])

    j_end, _ = lax.while_loop(cond, body, (n_pages - 1, jnp.max(run_ref[...])))

    @pl.when(j_end >= 0)
    def _():
        page_copy(j_end, slot_of(j_end)).wait()

    n_heads = half_w // HEAD_DIM
    o_ref[...] = _gather_head_lanes(acc_ref[...], n_heads, t_new).astype(o_ref.dtype)


def _sb_sample_attention(q32, kvn32, cache_t, layer, page_table, *, t_new):
    n, half_w = q32.shape
    n_seq, n_pages = page_table.shape
    page = cache_t.shape[2]
    rows = (half_w // HEAD_DIM) * t_new
    seq_row = lambda b, pt: (b, 0)
    return pl.pallas_call(
        functools.partial(_sb_sample_kernel, n_pages=n_pages, page=page, half_w=half_w, t_new=t_new,
                          row0=layer * 2 * half_w),
        out_shape=jax.ShapeDtypeStruct((n, half_w), F32),
        grid_spec=pltpu.PrefetchScalarGridSpec(
            num_scalar_prefetch=1,
            grid=(n_seq,),
            in_specs=[pl.BlockSpec((t_new, half_w), seq_row),
                      pl.BlockSpec((t_new, 2 * half_w), seq_row),
                      pl.BlockSpec(memory_space=pl.ANY)],
            out_specs=pl.BlockSpec((t_new, half_w), seq_row),
            scratch_shapes=[pltpu.VMEM((rows, half_w), BF16), pltpu.VMEM((rows, 1), F32),
                            pltpu.VMEM((rows, half_w), F32), pltpu.VMEM((2, 2 * half_w, page), F32),
                            pltpu.SemaphoreType.DMA((2,))]),
        compiler_params=_params(("arbitrary",)),
        name="stick_breaking_attention_sample",
    )(page_table, q32, kvn32, cache_t)


def _router_gates(logits, n_groups, n_experts):
    lane = lax.broadcasted_iota(jnp.int32, logits.shape, 1).astype(F32)
    neg = -jnp.inf
    big = float(LANES)
    is_group = lane < n_groups
    gl = jnp.where(is_group, logits, neg)
    gmax = jnp.max(gl, axis=1, keepdims=True)
    pg_sel = 1.0 / jnp.sum(jnp.where(is_group, jnp.exp(gl - gmax), 0.0), axis=1, keepdims=True)
    g = jnp.min(jnp.where(gl == gmax, lane, big), axis=1, keepdims=True)
    lo = n_groups + g * n_experts
    in_group = jnp.logical_and(lane >= lo, lane < lo + n_experts)
    el = jnp.where(in_group, logits, neg)
    v1 = jnp.max(el, axis=1, keepdims=True)
    i1 = jnp.min(jnp.where(el == v1, lane, big), axis=1, keepdims=True)
    el2 = jnp.where(lane == i1, neg, el)
    v2 = jnp.max(el2, axis=1, keepdims=True)
    i2 = jnp.min(jnp.where(el2 == v2, lane, big), axis=1, keepdims=True)
    e2 = jnp.exp(v2 - v1)
    w1 = 1.0 / (1.0 + e2)
    w2 = e2 * w1
    return pg_sel * (jnp.where(lane == i1, w1, 0.0) + jnp.where(lane == i2, w2, 0.0))


def _post_kernel(x_ref, od_ref, os_ref, wo_ref, gf_ref, wr_ref, br_ref, wg_ref, wu_ref, wd_ref, gfin_ref,
                 y_ref, act_ref, *, n_groups, n_experts, final_norm):
    half = od_ref.shape[1]
    attn = (jnp.dot(od_ref[...], wo_ref[:half, :], preferred_element_type=F32)
            + jnp.dot(os_ref[...], wo_ref[half:, :], preferred_element_type=F32))
    x2 = x_ref[...] + attn
    hf = x2 * lax.rsqrt(jnp.mean(x2 * x2, axis=-1, keepdims=True) + NORM_EPS) * gf_ref[...]
    h, h_lo = _split_hi_lo(hf)
    both = jnp.dot(h, wr_ref[...], preferred_element_type=F32)
    logits = (both[:, :LANES] + both[:, LANES:]
              + jnp.dot(h_lo, wr_ref[:, :LANES], preferred_element_type=F32) + br_ref[...])
    gates = _router_gates(logits, n_groups, n_experts)

    n_total, _, d_exp = wg_ref.shape
    hidden = lambda e: (jnp.dot(h, wg_ref[e], preferred_element_type=F32),
                        jnp.dot(h, wu_ref[e], preferred_element_type=F32))
    nxt = hidden(0)
    for e in range(n_total):
        hg, hu = nxt
        if e + 1 < n_total:
            nxt = hidden(e + 1)
        gate = gates[:, n_groups + e:n_groups + e + 1]
        act_ref[:, e * d_exp:(e + 1) * d_exp] = (hg * (1.0 / (1.0 + jnp.exp(-hg))) * hu * gate).astype(BF16)
    x3 = x2 + jnp.dot(act_ref[...], wd_ref[...], preferred_element_type=F32)
    if final_norm:
        x3 = x3 * lax.rsqrt(jnp.mean(x3 * x3, axis=-1, keepdims=True) + NORM_EPS) * gfin_ref[...]
    y_ref[...] = x3


def _post_attention(x2d, od16, os16, wo16, g_ffn, wr16, b_router, wg16, wu16, wd16, g_final, *,
                    tm, n_groups, n_experts, final_norm):
    n, d = x2d.shape
    half = od16.shape[1]
    n_total, _, d_exp = wg16.shape
    row = lambda i: (i, 0)
    once = dict(pipeline_mode=pl.Buffered(1))
    const2 = lambda i: (0, 0)
    const3 = lambda i: (0, 0, 0)
    return pl.pallas_call(
        functools.partial(_post_kernel, n_groups=n_groups, n_experts=n_experts, final_norm=final_norm),
        out_shape=jax.ShapeDtypeStruct((n, d), F32),
        grid=(n // tm,),
        in_specs=[pl.BlockSpec((tm, d), row), pl.BlockSpec((tm, half), row), pl.BlockSpec((tm, half), row),
                  pl.BlockSpec((d, d), const2, **once), pl.BlockSpec((1, d), const2),
                  pl.BlockSpec((d, 2 * LANES), const2, **once), pl.BlockSpec((1, LANES), const2),
                  pl.BlockSpec((n_total, d, d_exp), const3, **once),
                  pl.BlockSpec((n_total, d, d_exp), const3, **once),
                  pl.BlockSpec((n_total * d_exp, d), const2, **once),
                  pl.BlockSpec((1, d), const2)],
        out_specs=pl.BlockSpec((tm, d), row),
        scratch_shapes=[pltpu.VMEM((tm, n_total * d_exp), BF16)],
        compiler_params=_params(("parallel",)),
        name="out_proj_hmoe",
    )(x2d, od16, os16, wo16, g_ffn.reshape(1, d), wr16, b_router, wg16, wu16, wd16, g_final.reshape(1, d))


def _row_tile(n, pref):
    tm = min(n, pref)
    while n % tm:
        tm //= 2
    return tm


def kernel(x_prompt, x_sample, cache_kv_diff, cache_kv_sb, page_table, rel_bias, g_mix, w_in, lambda_q1,
           lambda_k1, lambda_q2, lambda_k2, subln_g, w_out, g_ffn, w_group, b_group, w_erouter, b_erouter,
           w_gate, w_up, w_down, g_final):
    batch, seq, d = x_prompt.shape
    n_seq, t_new, _ = x_sample.shape
    n_pool, page, depth, _, h_diff, diff_hw = cache_kv_diff.shape
    _, _, _, _, h_sb, sb_hw = cache_kv_sb.shape
    n_pages = page_table.shape[1]
    past_len = n_pages * page
    n_groups, n_experts = w_erouter.shape[2], w_erouter.shape[3]
    half_w = h_diff * diff_hw
    assert diff_hw == 2 * HEAD_DIM == LANES and sb_hw == HEAD_DIM and h_sb * sb_hw == half_w
    assert seq % ATTN_TILE == 0 and n_pages % PAGES_PER_STEP == 0 and page == LANES
    assert n_groups + n_groups * n_experts <= LANES

    xp = x_prompt.reshape(batch * seq, d)
    xs = x_sample.reshape(n_seq * t_new, d)
    cache_d = cache_kv_diff.reshape(n_pool, page * depth * 2 * h_diff, diff_hw)
    cache_s = jnp.transpose(cache_kv_sb, (0, 2, 3, 4, 5, 1)).reshape(n_pool, depth * 2 * half_w, page)
    bias_tiles = _prompt_bias_tiles(rel_bias, ATTN_TILE)
    bias_tab = _sample_bias_table(rel_bias, past_len, page, t_new)
    tm_p = _row_tile(seq, 512)
    tm_s = _row_tile(n_seq * t_new, 512)
    tm_moe_p = _row_tile(batch * seq, 512)

    kvd_p, kvs_p, kvd_s, kvs_s = [], [], [], []
    for l in range(depth):
        li = _lambda_init(l)
        lam_params = [a[l].reshape(1, HEAD_DIM).astype(F32) for a in (lambda_q1, lambda_k1, lambda_q2, lambda_k2)]
        w16 = w_in[l].astype(BF16)
        wo16 = w_out[l].astype(BF16)
        wr = jnp.concatenate([w_group[l].astype(F32), w_erouter[l].astype(F32).reshape(d, n_groups * n_experts)],
                             axis=1)
        wr = jnp.pad(wr, ((0, 0), (0, LANES - wr.shape[1])))
        wr_hi = wr.astype(BF16)
        wr16 = jnp.concatenate([wr_hi, (wr - wr_hi.astype(F32)).astype(BF16)], axis=1)
        br = jnp.concatenate([b_group[l].astype(F32), b_erouter[l].astype(F32).reshape(-1)])
        br = jnp.pad(br, (0, LANES - br.shape[0])).reshape(1, LANES)
        d_exp = w_gate.shape[-1]
        wg16 = w_gate[l].astype(BF16).reshape(n_groups * n_experts, d, d_exp)
        wu16 = w_up[l].astype(BF16).reshape(n_groups * n_experts, d, d_exp)
        wd16 = w_down[l].astype(BF16).reshape(n_groups * n_experts * d_exp, d)
        last = l == depth - 1

        qd16, qs16, kvd, kvst, kd16, vdt16, ks16, vst16 = _project_prompt(xp, g_mix[l], w16, tm=tm_p, batch=batch,
                                                                         seq=seq)
        od16 = _diff_prompt_attention(qd16, kd16, vdt16, bias_tiles, rel_bias, lam_params, subln_g[l],
                                      batch=batch, seq=seq, lam_init=li)
        os16 = _sb_prompt_attention(qs16, ks16, vst16, batch=batch, seq=seq)
        xp = _post_attention(xp, od16, os16, wo16, g_ffn[l], wr16, br, wg16, wu16, wd16, g_final,
                             tm=tm_moe_p, n_groups=n_groups, n_experts=n_experts, final_norm=last)
        kvd_p.append(kvd.reshape(batch, seq, 2, h_diff, diff_hw))
        kvs_p.append(jnp.transpose(kvst.reshape(batch, 2, h_sb, sb_hw, seq), (0, 4, 1, 2, 3)))

        qd32, qs32, kvd, kvs = _project_sample(xs, g_mix[l], w16, tm=tm_s)
        od = _diff_sample_attention(qd32, kvd, cache_d, l, depth, page_table, bias_tab, lam_params, subln_g[l],
                                    t_new=t_new, lam_init=li)
        osb = _sb_sample_attention(qs32, kvs, cache_s, l, page_table, t_new=t_new)
        xs = _post_attention(xs, od.astype(BF16), osb.astype(BF16), wo16, g_ffn[l], wr16, br, wg16, wu16,
                             wd16, g_final, tm=tm_s, n_groups=n_groups, n_experts=n_experts, final_norm=last)
        kvd_s.append(kvd.reshape(n_seq, t_new, 2, h_diff, diff_hw))
        kvs_s.append(kvs.reshape(n_seq, t_new, 2, h_sb, sb_hw))

    y_prompt = xp.reshape(batch, seq, d)
    y_sample = xs.reshape(n_seq, t_new, d)
    return (y_prompt, y_sample, jnp.stack(kvd_p, axis=2), jnp.stack(kvs_p, axis=2),
            jnp.stack(kvd_s, axis=2), jnp.stack(kvs_s, axis=2))
```

```python
import functools
import math

import numpy as np
import jax
import jax.numpy as jnp
from jax import lax
from jax.experimental import pallas as pl
from jax.experimental.pallas import tpu as pltpu

HEAD_DIM = 64
MAX_DISTANCE = 128
NORM_EPS = 1e-6
SUBLN_EPS = 1e-5
TOP_K_INNER = 2

LANES = 128
VMEM_LIMIT_BYTES = 56 * 1024 * 1024

DIFF_TILE = 256
SB_TILE = 256
SUM_ROWS = 16
SB_SKIP_LOG = -90.0
PAGES_PER_STEP = 32

F32 = jnp.float32
BF16 = jnp.bfloat16
NT_DIMS = (((1,), (1,)), ((), ()))
LOG2_E = math.log2(math.e)


def _lambda_init(layer):
    return 0.8 - 0.6 * math.exp(-0.3 * layer)


def _rel_bucket_np(rel, n_buckets):
    n = np.maximum(-rel, 0)
    max_exact = n_buckets // 2
    nf = np.maximum(n, 1).astype(np.float32)
    large = max_exact + (np.log(nf / np.float32(max_exact)) / np.float32(math.log(MAX_DISTANCE / max_exact))
                         * np.float32(n_buckets - max_exact)).astype(np.int32)
    large = np.minimum(large, n_buckets - 1)
    return np.where(n < max_exact, n, large).astype(np.int32)


def _params(sem):
    return pltpu.CompilerParams(dimension_semantics=sem, vmem_limit_bytes=VMEM_LIMIT_BYTES)


def _normed_proj_cols(x_ref, g_ref, w_ref, half_w):
    x = x_ref[...]
    ms = jnp.mean(x * x, axis=-1, keepdims=True)
    h = (x * lax.rsqrt(ms + NORM_EPS) * g_ref[...]).astype(BF16)
    return lambda i: jnp.dot(h, w_ref[:, i * half_w:(i + 1) * half_w], preferred_element_type=F32)


def _proj_prompt_kernel(x_ref, g_ref, w_ref, qd_ref, qs_ref, kvd_ref, kvst_ref, kd16_ref, vdt16_ref, ks16_ref,
                        vst16_ref, *, half_w, scale):
    cols = _normed_proj_cols(x_ref, g_ref, w_ref, half_w)
    tm = x_ref.shape[0]
    n_rows = 2 * half_w // LANES

    def store_diff_rows(first, val):
        for r in range(n_rows // 2):
            kvd_ref[pl.ds(first + r, tm, stride=n_rows), :] = val[:, r * LANES:(r + 1) * LANES]

    qd = cols(0)
    kd = cols(1)
    qd_ref[...] = (qd * (scale * LOG2_E)).astype(BF16)
    vd = cols(2)
    store_diff_rows(0, kd)
    kd16_ref[...] = kd.astype(BF16)
    qs = cols(3)
    store_diff_rows(n_rows // 2, vd)
    vdt16_ref[...] = vd.T.astype(BF16)
    ks = cols(4)
    qs_ref[...] = (qs * (scale * LOG2_E)).astype(BF16)
    vs = cols(5)
    ks16_ref[...] = ks.astype(BF16)
    kvst_ref[0, :half_w, :] = ks.T
    vst = vs.T
    kvst_ref[0, half_w:, :] = vst
    vst16_ref[...] = vst.astype(BF16)


def _project_prompt(x2d, g, w16, *, tm, batch, seq):
    n, d = x2d.shape
    half_w = w16.shape[1] // 6
    n_rows = 2 * half_w // LANES
    nbs = seq // tm
    row = lambda i: (i, 0)
    const = lambda i: (0, 0)
    colblk = lambda i: (0, i)
    out_shape = (jax.ShapeDtypeStruct((n, half_w), BF16), jax.ShapeDtypeStruct((n, half_w), BF16),
                 jax.ShapeDtypeStruct((n * n_rows, LANES), F32),
                 jax.ShapeDtypeStruct((batch, 2 * half_w, seq), F32),
                 jax.ShapeDtypeStruct((n, half_w), BF16), jax.ShapeDtypeStruct((half_w, n), BF16),
                 jax.ShapeDtypeStruct((n, half_w), BF16), jax.ShapeDtypeStruct((half_w, n), BF16))
    out_specs = (pl.BlockSpec((tm, half_w), row), pl.BlockSpec((tm, half_w), row),
                 pl.BlockSpec((tm * n_rows, LANES), row),
                 pl.BlockSpec((1, 2 * half_w, tm), lambda i: (i // nbs, 0, i % nbs)),
                 pl.BlockSpec((tm, half_w), row), pl.BlockSpec((half_w, tm), colblk),
                 pl.BlockSpec((tm, half_w), row), pl.BlockSpec((half_w, tm), colblk))
    return pl.pallas_call(
        functools.partial(_proj_prompt_kernel, half_w=half_w, scale=HEAD_DIM ** -0.5),
        out_shape=out_shape,
        grid=(n // tm,),
        in_specs=[pl.BlockSpec((tm, d), row), pl.BlockSpec((1, d), const),
                  pl.BlockSpec((d, 6 * half_w), const)],
        out_specs=out_specs,
        compiler_params=_params(("parallel",)),
        name="rmsnorm_qkv_proj_prompt",
    )(x2d, g.reshape(1, d), w16)


def _proj_sample_kernel(x_ref, g_ref, w_ref, qd_ref, qs_ref, kvd_ref, kvs_ref, *, half_w, scale):
    cols = _normed_proj_cols(x_ref, g_ref, w_ref, half_w)
    qd_ref[...] = cols(0) * scale
    qs_ref[...] = cols(3) * scale
    kvd_ref[:, :half_w] = cols(1)
    kvd_ref[:, half_w:] = cols(2)
    kvs_ref[:, :half_w] = cols(4)
    kvs_ref[:, half_w:] = cols(5)


def _project_sample(x2d, g, w16, *, tm):
    n, d = x2d.shape
    half_w = w16.shape[1] // 6
    row = lambda i: (i, 0)
    const = lambda i: (0, 0)
    return pl.pallas_call(
        functools.partial(_proj_sample_kernel, half_w=half_w, scale=HEAD_DIM ** -0.5),
        out_shape=(jax.ShapeDtypeStruct((n, half_w), F32), jax.ShapeDtypeStruct((n, half_w), F32),
                   jax.ShapeDtypeStruct((n, 2 * half_w), F32), jax.ShapeDtypeStruct((n, 2 * half_w), F32)),
        grid=(n // tm,),
        in_specs=[pl.BlockSpec((tm, d), row), pl.BlockSpec((1, d), const),
                  pl.BlockSpec((d, 6 * half_w), const)],
        out_specs=(pl.BlockSpec((tm, half_w), row), pl.BlockSpec((tm, half_w), row),
                   pl.BlockSpec((tm, 2 * half_w), row), pl.BlockSpec((tm, 2 * half_w), row)),
        compiler_params=_params(("parallel",)),
        name="rmsnorm_qkv_proj_sample",
    )(x2d, g.reshape(1, d), w16)


def _bias_select(bucket, rb_ref, h, n_buckets):
    out = jnp.zeros(bucket.shape, F32)
    for b in range(n_buckets):
        out = jnp.where(bucket == b, rb_ref[b, h], out)
    return out


def _prompt_bias_kernel(rb_ref, bucket_ref, out_ref, *, n_buckets):
    h = pl.program_id(0)
    t = bucket_ref.shape[1]
    for d in range(bucket_ref.shape[0]):
        tile = _bias_select(bucket_ref[d], rb_ref, h, n_buckets) * LOG2_E
        if d == 0:
            key = lax.broadcasted_iota(jnp.int32, tile.shape, 0)
            qry = lax.broadcasted_iota(jnp.int32, tile.shape, 1)
            tile = jnp.where(key <= qry, tile, -jnp.inf)
        out_ref[0, (1 - d) * t:(2 - d) * t, :] = jnp.concatenate([tile, tile], axis=1)


def _prompt_bias_tiles(rel_bias, t):
    n_buckets, n_heads = rel_bias.shape
    key = np.arange(t)[:, None]
    qry = np.arange(t)[None, :]
    bucket = np.stack([_rel_bucket_np(key - qry - d * t, n_buckets) for d in range(2)])
    return pl.pallas_call(
        functools.partial(_prompt_bias_kernel, n_buckets=n_buckets),
        out_shape=jax.ShapeDtypeStruct((n_heads, 2 * t, 2 * t), F32),
        grid=(n_heads,),
        in_specs=[pl.BlockSpec(memory_space=pltpu.SMEM),
                  pl.BlockSpec((2, t, t), lambda h: (0, 0, 0))],
        out_specs=pl.BlockSpec((1, 2 * t, 2 * t), lambda h: (h, 0, 0)),
        compiler_params=_params(("parallel",)),
        name="prompt_rel_bias_tiles",
    )(rel_bias.astype(F32), jnp.asarray(bucket))


def _sample_bias_kernel(rb_ref, bucket_ref, out_ref, *, n_buckets, n_heads):
    rows = bucket_ref.shape[0] // n_heads
    for h in range(n_heads):
        sl = slice(h * rows, (h + 1) * rows)
        out_ref[sl, :] = _bias_select(bucket_ref[sl, :], rb_ref, h, n_buckets)


def _sample_bias_table(rel_bias, past_len, page, t_new):
    n_buckets, n_heads = rel_bias.shape
    rows = n_heads * 2 * t_new
    q_pos = past_len + (np.arange(rows) % t_new)[:, None]
    far = np.full((rows, page), n_buckets - 1, np.int32)
    last = _rel_bucket_np((past_len - page + np.arange(page))[None, :] - q_pos, n_buckets)
    new = _rel_bucket_np((past_len + np.arange(page))[None, :] - q_pos, n_buckets)
    bucket = np.concatenate([far, last, new], axis=1)
    return pl.pallas_call(
        functools.partial(_sample_bias_kernel, n_buckets=n_buckets, n_heads=n_heads),
        out_shape=jax.ShapeDtypeStruct(bucket.shape, F32),
        in_specs=[pl.BlockSpec(memory_space=pltpu.SMEM), pl.BlockSpec(memory_space=pltpu.VMEM)],
        out_specs=pl.BlockSpec(memory_space=pltpu.VMEM),
        name="sample_rel_bias_table",
    )(rel_bias.astype(F32), jnp.asarray(bucket))


def _split_heads_rows(q):
    lane = lax.broadcasted_iota(jnp.int32, q.shape, 1)
    zero = jnp.zeros_like(q)
    return jnp.concatenate([jnp.where(lane < HEAD_DIM, q, zero), jnp.where(lane >= HEAD_DIM, q, zero)], axis=0)


def _lambda_value(lq1_ref, lk1_ref, lq2_ref, lk2_ref, lam_init):
    a = jnp.sum(lq1_ref[...] * lk1_ref[...], axis=-1, keepdims=True)
    b = jnp.sum(lq2_ref[...] * lk2_ref[...], axis=-1, keepdims=True)
    return jnp.exp(a) - jnp.exp(b) + lam_init


def _diff_prompt_kernel(rb_ref, lq1_ref, lk1_ref, lq2_ref, lk2_ref, g_ref, q_ref, k_ref, vt_ref, bias_ref,
                        o_ref, m_ref, acc_ref, s_ref, *, t, far_bucket, lam_init):
    qi = pl.program_id(1)
    n_heads = q_ref.shape[1] // LANES
    head = lambda h: slice(h * LANES, (h + 1) * LANES)
    qq = [_split_heads_rows(q_ref[:, head(h)]) for h in range(n_heads)]
    keys = lambda ki, nb: pl.ds(pl.multiple_of(ki * t, t), nb * t)

    def scores(h, ki, nb):
        return lax.dot_general(k_ref[keys(ki, nb), head(h)], qq[h], NT_DIMS,
                               preferred_element_type=F32)

    def accumulate(h, ki, nb, s, bias, first=False):
        if callable(bias):
            m_blk = jnp.max(s + bias(), axis=0, keepdims=True)
        else:
            m_blk = jnp.max(s, axis=0, keepdims=True) + bias
        if first:
            m_new = m_blk
        else:
            m_old = m_ref[h]
            m_new = jnp.maximum(m_old, m_blk)
        p = jnp.exp2(s - (m_new - (bias() if callable(bias) else bias)))
        vt1 = jnp.concatenate([vt_ref[head(h), keys(ki, nb)], jnp.ones((SUM_ROWS, nb * t), BF16)], axis=0)
        pv = jnp.dot(vt1, p.astype(BF16), preferred_element_type=F32)
        acc_ref[h] = pv if first else jnp.exp2(m_old - m_new) * acc_ref[h] + pv
        m_ref[h] = m_new

    def run_heads(ki, nb, s_first, bias, first, prefetch):
        s_next = s_first
        for h in range(n_heads):
            s_cur = s_next
            s_next = scores(h + 1, ki, nb) if h + 1 < n_heads else (prefetch() if prefetch else None)
            accumulate(h, ki, nb, s_cur, bias(h), first)
        return s_next

    @pl.when(qi == 0)
    def _():
        run_heads(0, 1, scores(0, 0, 1), lambda h: lambda: bias_ref[h, t:, :], True, None)

    @pl.when(qi >= 1)
    def _():
        s_ref[...] = run_heads(qi - 1, 2, scores(0, qi - 1, 2), lambda h: lambda: bias_ref[h], True,
                               lambda: scores(0, 0, 2))

    far_bias = [rb_ref[far_bucket, h] * LOG2_E for h in range(n_heads)]
    n_far = jnp.maximum(qi - 1, 0)

    def far_pair(i, carry):
        s_ref[...] = run_heads(2 * i, 2, s_ref[...], lambda h: far_bias[h], False,
                               lambda: scores(0, 2 * i + 2, 2))
        return carry

    lax.fori_loop(0, n_far // 2, far_pair, 0)

    @pl.when(n_far % 2 == 1)
    def _():
        run_heads(n_far - 1, 1, s_ref[:t, :], lambda h: far_bias[h], False, None)

    lam = _lambda_value(lq1_ref, lk1_ref, lq2_ref, lk2_ref, lam_init)
    for h in range(n_heads):
        o = acc_ref[h, :LANES, :] / acc_ref[h, LANES:LANES + 1, :]
        od = o[:, :t] - lam * o[:, t:]
        od = od * lax.rsqrt(jnp.mean(od * od, axis=0, keepdims=True) + SUBLN_EPS)
        od = od * g_ref[...] * (1.0 - lam_init)
        o_ref[:, head(h)] = od.T.astype(o_ref.dtype)


def _diff_prompt_attention(qd16, kd16, vdt16, bias_tiles, rel_bias, lam_params, subln_g, *, batch, seq, lam_init):
    n, width = qd16.shape
    n_heads = width // LANES
    t = DIFF_TILE
    nq = seq // t
    n_buckets = rel_bias.shape[0]
    assert int(_rel_bucket_np(np.array([-(t + 1)]), n_buckets)[0]) == n_buckets - 1
    small = lambda b, i: (0, 0)
    return pl.pallas_call(
        functools.partial(_diff_prompt_kernel, t=t, far_bucket=n_buckets - 1, lam_init=lam_init),
        out_shape=jax.ShapeDtypeStruct((n, width), BF16),
        grid=(batch, nq),
        in_specs=[pl.BlockSpec(memory_space=pltpu.SMEM)]
                 + [pl.BlockSpec((1, HEAD_DIM), small)] * 4
                 + [pl.BlockSpec((LANES, 1), small),
                    pl.BlockSpec((t, width), lambda b, i: (b * nq + i, 0)),
                    pl.BlockSpec((seq, width), lambda b, i: (b, 0)),
                    pl.BlockSpec((width, seq), lambda b, i: (0, b)),
                    pl.BlockSpec((n_heads, 2 * t, 2 * t), lambda b, i: (0, 0, 0), pipeline_mode=pl.Buffered(1))],
        out_specs=pl.BlockSpec((t, width), lambda b, i: (b * nq + i, 0)),
        scratch_shapes=[pltpu.VMEM((n_heads, 1, 2 * t), F32),
                        pltpu.VMEM((n_heads, LANES + SUM_ROWS, 2 * t), F32), pltpu.VMEM((2 * t, 2 * t), F32)],
        compiler_params=_params(("parallel", "arbitrary")),
        name="diff_attention_prompt",
    )(rel_bias.astype(F32), *lam_params, subln_g.reshape(LANES, 1), qd16, kd16, vdt16, bias_tiles)


def _log_sigmoid_pair(z):
    soft = jnp.log(1.0 + jnp.exp(-jnp.abs(z)))
    return jnp.minimum(z, 0.0) - soft, jnp.minimum(-z, 0.0) - soft


def _split_hi_lo(x):
    hi = x.astype(BF16)
    lo = (x - hi.astype(F32)).astype(BF16)
    return hi, lo


def _sb_prompt_kernel(q_ref, k_ref, vt_ref, o_ref, used_ref, acc_ref, z_ref, *, t):
    qi = pl.program_id(1)
    n_pairs = q_ref.shape[1] // LANES
    pair = lambda p: slice(p * LANES, (p + 1) * LANES)
    qq = [_split_heads_rows(q_ref[:, pair(p)]) for p in range(n_pairs)]
    row = lax.broadcasted_iota(jnp.int32, (t, t), 0)
    colk = lax.broadcasted_iota(jnp.int32, (t, t), 1)
    later = (colk > row).astype(BF16)
    later2 = jnp.concatenate([later, later], axis=1)

    def logits(p, ki):
        k = k_ref[pl.ds(pl.multiple_of(ki * t, t), t), pair(p)]
        return lax.dot_general(k, qq[p], NT_DIMS, preferred_element_type=F32)

    def stage_a(z, mask):
        soft = jnp.maximum(z, 0.0) + jnp.log2(1.0 + jnp.exp2(-jnp.abs(z)))
        log_beta = z - soft
        if mask is not None:
            soft = jnp.where(mask, soft, 0.0)
        hi, lo = _split_hi_lo(soft)
        after = jnp.dot(later2, jnp.concatenate([hi, lo], axis=0), preferred_element_type=F32)
        return log_beta, after, after[0:1, :] + soft[0:1, :]

    def stage_b(p, ki, mask, log_beta, after, total):
        used = used_ref[p]
        a = jnp.exp2(log_beta - after)
        if mask is not None:
            a = jnp.where(mask, a, 0.0)
        vt = vt_ref[pair(p), pl.ds(pl.multiple_of(ki * t, t), t)]
        acc_ref[p] += jnp.dot(vt, a.astype(BF16), preferred_element_type=F32) * jnp.exp2(-used)
        used_ref[p] = used + total

    def run_steps(steps, z_first, prefetch_ki):
        z_next = z_first
        pending = None
        for i, (p, ki, mask) in enumerate(steps):
            z_cur = z_next
            z_next = logits(steps[i + 1][0], steps[i + 1][1]) if i + 1 < len(steps) else logits(0, prefetch_ki)
            staged = stage_a(z_cur, mask)
            if pending is not None:
                stage_b(*pending)
            pending = (p, ki, mask) + staged
        stage_b(*pending)
        return z_next

    used_ref[...] = jnp.zeros_like(used_ref)
    acc_ref[...] = jnp.zeros_like(acc_ref)
    key = lax.broadcasted_iota(jnp.int32, (t, 2 * t), 0)
    col = lax.broadcasted_iota(jnp.int32, (t, 2 * t), 1)
    causal = key < jnp.where(col >= t, col - t, col)
    diagonal = [(p, qi, causal) for p in range(n_pairs)]

    @pl.when(qi == 0)
    def _():
        z_ref[...] = run_steps(diagonal, logits(0, qi), 0)

    @pl.when(qi >= 1)
    def _():
        z_ref[...] = run_steps(diagonal + [(p, qi - 1, None) for p in range(n_pairs)], logits(0, qi),
                               jnp.maximum(qi - 2, 0))

    def cond(c):
        ki, least_used = c
        return jnp.logical_and(ki >= 0, least_used < -SB_SKIP_LOG * LOG2_E)

    def body(c):
        ki, _ = c
        z_ref[...] = run_steps([(p, ki, None) for p in range(n_pairs)], z_ref[...], jnp.maximum(ki - 1, 0))
        return ki - 1, jnp.min(used_ref[...])

    lax.while_loop(cond, body, (qi - 2, jnp.min(used_ref[...])))

    for p in range(n_pairs):
        o = jnp.concatenate([acc_ref[p, :HEAD_DIM, :t], acc_ref[p, HEAD_DIM:, t:]], axis=0)
        o_ref[:, pair(p)] = o.T.astype(o_ref.dtype)


def _sb_prompt_attention(qs16, ks16, vst16, *, batch, seq):
    n, width = qs16.shape
    n_pairs = width // LANES
    t = SB_TILE
    nq = seq // t
    return pl.pallas_call(
        functools.partial(_sb_prompt_kernel, t=t),
        out_shape=jax.ShapeDtypeStruct((n, width), BF16),
        grid=(batch, nq),
        in_specs=[pl.BlockSpec((t, width), lambda b, i: (b * nq + i, 0)),
                  pl.BlockSpec((seq, width), lambda b, i: (b, 0)),
                  pl.BlockSpec((width, seq), lambda b, i: (0, b))],
        out_specs=pl.BlockSpec((t, width), lambda b, i: (b * nq + i, 0)),
        scratch_shapes=[pltpu.VMEM((n_pairs, 1, 2 * t), F32), pltpu.VMEM((n_pairs, LANES, 2 * t), F32),
                        pltpu.VMEM((t, 2 * t), F32)],
        compiler_params=_params(("parallel", "arbitrary")),
        name="stick_breaking_attention_prompt",
    )(qs16, ks16, vst16)


def _block_rows(q, group_w):
    t, w = q.shape
    n_groups = w // group_w
    qt = jnp.concatenate([q] * n_groups, axis=0)
    row_g = lax.broadcasted_iota(jnp.int32, qt.shape, 0) // t
    col_g = lax.broadcasted_iota(jnp.int32, qt.shape, 1) // group_w
    return jnp.where(row_g == col_g, qt, 0.0).astype(BF16)


def _pad_rows(x, rows):
    return jnp.concatenate([x, jnp.zeros((rows - x.shape[0], x.shape[1]), x.dtype)], axis=0)


def _gather_head_lanes(acc, n_groups, t):
    w = acc.shape[1]
    col_g = lax.broadcasted_iota(jnp.int32, (t, w), 1) // (w // n_groups)
    out = jnp.zeros((t, w), acc.dtype)
    for g in range(n_groups):
        out = jnp.where(col_g == g, acc[g * t:(g + 1) * t, :], out)
    return out


def _diff_sample_kernel(pt_ref, lq1_ref, lk1_ref, lq2_ref, lk2_ref, g_ref, q_ref, kvn_ref, bias_ref, *rest,
                        n_pages_step, page, half_w, t_new, lam_init, row0, slot_rows):
    page_refs = rest[:n_pages_step]
    o_ref, qb_ref, m_ref, l_ref, acc_ref = rest[n_pages_step:]
    j = pl.program_id(1)
    n_steps = pl.num_programs(1)
    n_heads = half_w // LANES

    def accumulate(k, v, bias, mask):
        s = lax.dot_general(qb_ref[...], k, NT_DIMS, preferred_element_type=F32) + bias
        if mask is not None:
            s = jnp.where(mask, s, -jnp.inf)
        m_old = m_ref[...]
        m_new = jnp.maximum(m_old, jnp.max(s, axis=1, keepdims=True))
        alpha = jnp.exp(m_old - m_new)
        p = jnp.exp(s - m_new)
        l_ref[...] = alpha * l_ref[...] + jnp.sum(p, axis=1, keepdims=True)
        acc_ref[...] = alpha * acc_ref[...] + jnp.dot(p.astype(BF16), v, preferred_element_type=F32)
        m_ref[...] = m_new

    @pl.when(j == 0)
    def _():
        qb_ref[...] = _block_rows(q_ref[...], HEAD_DIM)
        rows = qb_ref.shape[0]
        m_ref[...] = jnp.full(m_ref.shape, -jnp.inf, F32)
        l_ref[...] = jnp.zeros_like(l_ref)
        acc_ref[...] = jnp.zeros_like(acc_ref)
        kvn = _pad_rows(kvn_ref[...], page)
        tq = lax.broadcasted_iota(jnp.int32, (rows, page), 0) % t_new
        tk = lax.broadcasted_iota(jnp.int32, (rows, page), 1)
        accumulate(kvn[:, :half_w].astype(BF16), kvn[:, half_w:].astype(BF16),
                   bias_ref[:, 2 * page:3 * page], tk <= tq)

    def heads(ref, first):
        parts = [ref[0, pl.ds(row0 + first + h, page, stride=slot_rows), :] for h in range(n_heads)]
        return jnp.concatenate(parts, axis=1).astype(BF16)

    far = bias_ref[:, 0:page]
    last = jnp.where(j == n_steps - 1, bias_ref[:, page:2 * page], far)
    accumulate(jnp.concatenate([heads(r, 0) for r in page_refs], axis=0),
               jnp.concatenate([heads(r, n_heads) for r in page_refs], axis=0),
               jnp.concatenate([far] * (n_pages_step - 1) + [last], axis=1), None)

    @pl.when(j == n_steps - 1)
    def _():
        o = acc_ref[...] / l_ref[...]
        o = _gather_head_lanes(o, n_heads, 2 * t_new)
        lam = _lambda_value(lq1_ref, lk1_ref, lq2_ref, lk2_ref, lam_init)
        od = o[:t_new] - lam * o[t_new:]
        outs = []
        for h in range(n_heads):
            oh = od[:, h * LANES:(h + 1) * LANES]
            oh = oh * lax.rsqrt(jnp.mean(oh * oh, axis=-1, keepdims=True) + SUBLN_EPS)
            outs.append(oh * g_ref[...] * (1.0 - lam_init))
        o_ref[...] = jnp.concatenate(outs, axis=1).astype(o_ref.dtype)


def _diff_sample_attention(q32, kvn32, cache_rows, layer, depth, page_table, bias_tab, lam_params, subln_g, *,
                           t_new, lam_init):
    n, half_w = q32.shape
    n_seq, n_pages = page_table.shape
    n_heads = half_w // LANES
    slot_rows = depth * 2 * n_heads
    page = cache_rows.shape[1] // slot_rows
    gp = PAGES_PER_STEP
    rows = n_heads * 2 * t_new
    small = lambda b, j, pt: (0, 0)
    seq_row = lambda b, j, pt: (b, 0)

    def page_spec(g):
        return pl.BlockSpec((1, page * slot_rows, LANES), lambda b, j, pt: (pt[b, j * gp + g], 0, 0))

    return pl.pallas_call(
        functools.partial(_diff_sample_kernel, n_pages_step=gp, page=page, half_w=half_w, t_new=t_new,
                          lam_init=lam_init, row0=layer * 2 * n_heads, slot_rows=slot_rows),
        out_shape=jax.ShapeDtypeStruct((n, half_w), F32),
        grid_spec=pltpu.PrefetchScalarGridSpec(
            num_scalar_prefetch=1,
            grid=(n_seq, n_pages // gp),
            in_specs=[pl.BlockSpec((1, HEAD_DIM), small)] * 4
                     + [pl.BlockSpec((1, LANES), small),
                        pl.BlockSpec((t_new, half_w), seq_row),
                        pl.BlockSpec((t_new, 2 * half_w), seq_row),
                        pl.BlockSpec((rows, 3 * page), small)]
                     + [page_spec(g) for g in range(gp)],
            out_specs=pl.BlockSpec((t_new, half_w), seq_row),
            scratch_shapes=[pltpu.VMEM((rows, half_w), BF16), pltpu.VMEM((rows, 1), F32),
                            pltpu.VMEM((rows, 1), F32), pltpu.VMEM((rows, half_w), F32)]),
        compiler_params=_params(("parallel", "arbitrary")),
        name="diff_attention_sample",
    )(page_table, *lam_params, subln_g.reshape(1, LANES), q32, kvn32, bias_tab, *([cache_rows] * gp))


def _sb_sample_kernel(pt_ref, q_ref, kvn_ref, cache_ref, o_ref, qb_ref, run_ref, acc_ref, buf_ref, sem_ref, *,
                      n_pages, page, half_w, t_new, row0):
    b = pl.program_id(0)

    def page_copy(seq, j):
        slot = (n_pages - 1 - j) & 1
        return pltpu.make_async_copy(cache_ref.at[pt_ref[seq, j], pl.ds(row0, 2 * half_w), :],
                                     buf_ref.at[slot], sem_ref.at[slot])

    @pl.when(b == 0)
    def _():
        page_copy(0, n_pages - 1).start()

    if n_pages >= 2:
        page_copy(b, n_pages - 2).start()

    def later_matrix(n):
        row = lax.broadcasted_iota(jnp.int32, (n, n), 0)
        col = lax.broadcasted_iota(jnp.int32, (n, n), 1)
        later = (row > col).astype(BF16)
        return jnp.concatenate([later, later], axis=0)

    def accumulate(z, pv, mask):
        log_beta, log_rest = _log_sigmoid_pair(z)
        if mask is not None:
            log_rest = jnp.where(mask, log_rest, 0.0)
        hi, lo = _split_hi_lo(log_rest)
        suffix = jnp.dot(jnp.concatenate([hi, lo], axis=1), later_matrix(z.shape[1]),
                         preferred_element_type=F32)
        a = jnp.exp(log_beta + suffix + run_ref[...])
        if mask is not None:
            a = jnp.where(mask, a, 0.0)
        acc_ref[...] += pv(a.astype(BF16))
        run_ref[...] += jnp.sum(log_rest, axis=1, keepdims=True)

    def page_kv(j):
        slot = (n_pages - 1 - j) & 1
        return buf_ref[slot, :half_w, :].astype(BF16), buf_ref[slot, half_w:, :].astype(BF16)

    qb_ref[...] = _block_rows(q_ref[...], HEAD_DIM)
    rows = qb_ref.shape[0]
    run_ref[...] = jnp.zeros_like(run_ref)
    acc_ref[...] = jnp.zeros_like(acc_ref)

    kvn = _pad_rows(kvn_ref[...], page)
    k_new = kvn[:, :half_w].astype(BF16)
    v_new = kvn[:, half_w:].astype(BF16)
    page_copy(b, n_pages - 1).wait()
    kt, vt = page_kv(n_pages - 1)
    tq = lax.broadcasted_iota(jnp.int32, (rows, 2 * page), 0) % t_new
    col = lax.broadcasted_iota(jnp.int32, (rows, 2 * page), 1)
    z = jnp.concatenate([jnp.dot(qb_ref[...], kt, preferred_element_type=F32),
                         lax.dot_general(qb_ref[...], k_new, NT_DIMS, preferred_element_type=F32)], axis=1)
    accumulate(z, lambda a: (lax.dot_general(a[:, :page], vt, NT_DIMS, preferred_element_type=F32)
                             + jnp.dot(a[:, page:], v_new, preferred_element_type=F32)),
               jnp.logical_or(col < page, col - page < tq))
    if n_pages >= 3:
        page_copy(b, n_pages - 3).start()

    def cond(c):
        j, live = c
        return jnp.logical_and(j >= 0, live > SB_SKIP_LOG)

    def body(c):
        j, _ = c
        page_copy(b, j).wait()
        kt, vt = page_kv(j)
        accumulate(jnp.dot(qb_ref[...], kt, preferred_element_type=F32),
                   lambda a: lax.dot_general(a, vt, NT_DIMS, preferred_element_type=F32), None)

        @pl.when(j >= 2)
        def _():
            page_copy(b, j - 2).start()

        return j - 1, jnp.max(run_ref[...])

    j_end, _ = lax.while_loop(cond, body, (n_pages - 2, jnp.max(run_ref[...])))

    @pl.when(j_end >= 0)
    def _():
        page_copy(b, j_end).wait()

    @pl.when(j_end >= 1)
    def _():
        page_copy(b, j_end - 1).wait()

    @pl.when(b + 1 < pl.num_programs(0))
    def _():
        page_copy(b + 1, n_pages - 1).start()

    n_heads = half_w // HEAD_DIM
    o_ref[...] = _gather_head_lanes(acc_ref[...], n_heads, t_new).astype(o_ref.dtype)


def _sb_sample_attention(q32, kvn32, cache_t, layer, page_table, *, t_new):
    n, half_w = q32.shape
    n_seq, n_pages = page_table.shape
    page = cache_t.shape[2]
    rows = (half_w // HEAD_DIM) * t_new
    seq_row = lambda b, pt: (b, 0)
    return pl.pallas_call(
        functools.partial(_sb_sample_kernel, n_pages=n_pages, page=page, half_w=half_w, t_new=t_new,
                          row0=layer * 2 * half_w),
        out_shape=jax.ShapeDtypeStruct((n, half_w), F32),
        grid_spec=pltpu.PrefetchScalarGridSpec(
            num_scalar_prefetch=1,
            grid=(n_seq,),
            in_specs=[pl.BlockSpec((t_new, half_w), seq_row),
                      pl.BlockSpec((t_new, 2 * half_w), seq_row),
                      pl.BlockSpec(memory_space=pl.ANY)],
            out_specs=pl.BlockSpec((t_new, half_w), seq_row),
            scratch_shapes=[pltpu.VMEM((rows, half_w), BF16), pltpu.VMEM((rows, 1), F32),
                            pltpu.VMEM((rows, half_w), F32), pltpu.VMEM((2, 2 * half_w, page), F32),
                            pltpu.SemaphoreType.DMA((2,))]),
        compiler_params=_params(("arbitrary",)),
        name="stick_breaking_attention_sample",
    )(page_table, q32, kvn32, cache_t)


def _router_gates(logits, n_groups, n_experts):
    lane = lax.broadcasted_iota(jnp.int32, logits.shape, 1).astype(F32)
    neg = -jnp.inf
    big = float(LANES)
    is_group = lane < n_groups
    gl = jnp.where(is_group, logits, neg)
    gmax = jnp.max(gl, axis=1, keepdims=True)
    pg_sel = 1.0 / jnp.sum(jnp.where(is_group, jnp.exp(gl - gmax), 0.0), axis=1, keepdims=True)
    g = jnp.min(jnp.where(gl == gmax, lane, big), axis=1, keepdims=True)
    lo = n_groups + g * n_experts
    in_group = jnp.logical_and(lane >= lo, lane < lo + n_experts)
    el = jnp.where(in_group, logits, neg)
    v1 = jnp.max(el, axis=1, keepdims=True)
    i1 = jnp.min(jnp.where(el == v1, lane, big), axis=1, keepdims=True)
    el2 = jnp.where(lane == i1, neg, el)
    v2 = jnp.max(el2, axis=1, keepdims=True)
    i2 = jnp.min(jnp.where(el2 == v2, lane, big), axis=1, keepdims=True)
    e2 = jnp.exp(v2 - v1)
    w1 = 1.0 / (1.0 + e2)
    w2 = e2 * w1
    return pg_sel * (jnp.where(lane == i1, w1, 0.0) + jnp.where(lane == i2, w2, 0.0))


def _post_kernel(x_ref, od_ref, os_ref, wo_ref, gf_ref, wr_ref, br_ref, wg_ref, wu_ref, wd_ref, gfin_ref,
                 y_ref, act_ref, *, n_groups, n_experts, final_norm):
    half = od_ref.shape[1]
    attn = (jnp.dot(od_ref[...], wo_ref[:half, :], preferred_element_type=F32)
            + jnp.dot(os_ref[...], wo_ref[half:, :], preferred_element_type=F32))
    x2 = x_ref[...] + attn
    hf = x2 * lax.rsqrt(jnp.mean(x2 * x2, axis=-1, keepdims=True) + NORM_EPS) * gf_ref[...]
    h, h_lo = _split_hi_lo(hf)
    both = jnp.dot(h, wr_ref[...], preferred_element_type=F32)
    logits = (both[:, :LANES] + both[:, LANES:]
              + jnp.dot(h_lo, wr_ref[:, :LANES], preferred_element_type=F32) + br_ref[...])
    gates = _router_gates(logits, n_groups, n_experts)

    n_total, _, d_exp = wg_ref.shape
    hidden = lambda e: (jnp.dot(h, wg_ref[e], preferred_element_type=F32),
                        jnp.dot(h, wu_ref[e], preferred_element_type=F32))
    nxt = hidden(0)
    for e in range(n_total):
        hg, hu = nxt
        if e + 1 < n_total:
            nxt = hidden(e + 1)
        gate = gates[:, n_groups + e:n_groups + e + 1]
        act_ref[:, e * d_exp:(e + 1) * d_exp] = (hg * (1.0 / (1.0 + jnp.exp(-hg))) * hu * gate).astype(BF16)
    x3 = x2 + jnp.dot(act_ref[...], wd_ref[...], preferred_element_type=F32)
    if final_norm:
        x3 = x3 * lax.rsqrt(jnp.mean(x3 * x3, axis=-1, keepdims=True) + NORM_EPS) * gfin_ref[...]
    y_ref[...] = x3


def _post_attention(x2d, od16, os16, wo16, g_ffn, wr16, b_router, wg16, wu16, wd16, g_final, *,
                    tm, n_groups, n_experts, final_norm):
    n, d = x2d.shape
    half = od16.shape[1]
    n_total, _, d_exp = wg16.shape
    row = lambda i: (i, 0)
    once = dict(pipeline_mode=pl.Buffered(1))
    const2 = lambda i: (0, 0)
    const3 = lambda i: (0, 0, 0)
    return pl.pallas_call(
        functools.partial(_post_kernel, n_groups=n_groups, n_experts=n_experts, final_norm=final_norm),
        out_shape=jax.ShapeDtypeStruct((n, d), F32),
        grid=(n // tm,),
        in_specs=[pl.BlockSpec((tm, d), row), pl.BlockSpec((tm, half), row), pl.BlockSpec((tm, half), row),
                  pl.BlockSpec((d, d), const2, **once), pl.BlockSpec((1, d), const2),
                  pl.BlockSpec((d, 2 * LANES), const2, **once), pl.BlockSpec((1, LANES), const2),
                  pl.BlockSpec((n_total, d, d_exp), const3, **once),
                  pl.BlockSpec((n_total, d, d_exp), const3, **once),
                  pl.BlockSpec((n_total * d_exp, d), const2, **once),
                  pl.BlockSpec((1, d), const2)],
        out_specs=pl.BlockSpec((tm, d), row),
        scratch_shapes=[pltpu.VMEM((tm, n_total * d_exp), BF16)],
        compiler_params=_params(("parallel",)),
        name="out_proj_hmoe",
    )(x2d, od16, os16, wo16, g_ffn.reshape(1, d), wr16, b_router, wg16, wu16, wd16, g_final.reshape(1, d))


def _row_tile(n, pref):
    tm = min(n, pref)
    while n % tm:
        tm //= 2
    return tm


def kernel(x_prompt, x_sample, cache_kv_diff, cache_kv_sb, page_table, rel_bias, g_mix, w_in, lambda_q1,
           lambda_k1, lambda_q2, lambda_k2, subln_g, w_out, g_ffn, w_group, b_group, w_erouter, b_erouter,
           w_gate, w_up, w_down, g_final):
    batch, seq, d = x_prompt.shape
    n_seq, t_new, _ = x_sample.shape
    n_pool, page, depth, _, h_diff, diff_hw = cache_kv_diff.shape
    _, _, _, _, h_sb, sb_hw = cache_kv_sb.shape
    n_pages = page_table.shape[1]
    past_len = n_pages * page
    n_groups, n_experts = w_erouter.shape[2], w_erouter.shape[3]
    half_w = h_diff * diff_hw
    assert diff_hw == 2 * HEAD_DIM == LANES and sb_hw == HEAD_DIM and h_sb * sb_hw == half_w
    assert seq % DIFF_TILE == 0 and seq % SB_TILE == 0 and n_pages % PAGES_PER_STEP == 0 and page == LANES
    assert n_groups + n_groups * n_experts <= LANES

    xp = x_prompt.reshape(batch * seq, d)
    xs = x_sample.reshape(n_seq * t_new, d)
    cache_d = cache_kv_diff.reshape(n_pool, page * depth * 2 * h_diff, diff_hw)
    cache_s = jnp.transpose(cache_kv_sb, (0, 2, 3, 4, 5, 1)).reshape(n_pool, depth * 2 * half_w, page)
    bias_tiles = _prompt_bias_tiles(rel_bias, DIFF_TILE)
    bias_tab = _sample_bias_table(rel_bias, past_len, page, t_new)
    tm_p = _row_tile(seq, 1024)
    tm_s = _row_tile(n_seq * t_new, 512)
    tm_moe_p = _row_tile(batch * seq, 512)

    kvd_p, kvs_p, kvd_s, kvs_s = [], [], [], []
    for l in range(depth):
        li = _lambda_init(l)
        lam_params = [a[l].reshape(1, HEAD_DIM).astype(F32) for a in (lambda_q1, lambda_k1, lambda_q2, lambda_k2)]
        w16 = w_in[l].astype(BF16)
        wo16 = w_out[l].astype(BF16)
        wr = jnp.concatenate([w_group[l].astype(F32), w_erouter[l].astype(F32).reshape(d, n_groups * n_experts)],
                             axis=1)
        wr = jnp.pad(wr, ((0, 0), (0, LANES - wr.shape[1])))
        wr_hi = wr.astype(BF16)
        wr16 = jnp.concatenate([wr_hi, (wr - wr_hi.astype(F32)).astype(BF16)], axis=1)
        br = jnp.concatenate([b_group[l].astype(F32), b_erouter[l].astype(F32).reshape(-1)])
        br = jnp.pad(br, (0, LANES - br.shape[0])).reshape(1, LANES)
        d_exp = w_gate.shape[-1]
        wg16 = w_gate[l].astype(BF16).reshape(n_groups * n_experts, d, d_exp)
        wu16 = w_up[l].astype(BF16).reshape(n_groups * n_experts, d, d_exp)
        wd16 = w_down[l].astype(BF16).reshape(n_groups * n_experts * d_exp, d)
        last = l == depth - 1

        qd16, qs16, kvd, kvst, kd16, vdt16, ks16, vst16 = _project_prompt(xp, g_mix[l], w16, tm=tm_p, batch=batch,
                                                                         seq=seq)
        od16 = _diff_prompt_attention(qd16, kd16, vdt16, bias_tiles, rel_bias, lam_params, subln_g[l],
                                      batch=batch, seq=seq, lam_init=li)
        os16 = _sb_prompt_attention(qs16, ks16, vst16, batch=batch, seq=seq)
        xp = _post_attention(xp, od16, os16, wo16, g_ffn[l], wr16, br, wg16, wu16, wd16, g_final,
                             tm=tm_moe_p, n_groups=n_groups, n_experts=n_experts, final_norm=last)
        kvd_p.append(kvd.reshape(batch, seq, 2, h_diff, diff_hw))
        kvs_p.append(jnp.transpose(kvst.reshape(batch, 2, h_sb, sb_hw, seq), (0, 4, 1, 2, 3)))

        qd32, qs32, kvd, kvs = _project_sample(xs, g_mix[l], w16, tm=tm_s)
        od = _diff_sample_attention(qd32, kvd, cache_d, l, depth, page_table, bias_tab, lam_params, subln_g[l],
                                    t_new=t_new, lam_init=li)
        osb = _sb_sample_attention(qs32, kvs, cache_s, l, page_table, t_new=t_new)
        xs = _post_attention(xs, od.astype(BF16), osb.astype(BF16), wo16, g_ffn[l], wr16, br, wg16, wu16,
                             wd16, g_final, tm=tm_s, n_groups=n_groups, n_experts=n_experts, final_norm=last)
        kvd_s.append(kvd.reshape(n_seq, t_new, 2, h_diff, diff_hw))
        kvs_s.append(kvs.reshape(n_seq, t_new, 2, h_sb, sb_hw))

    y_prompt = xp.reshape(batch, seq, d)
    y_sample = xs.reshape(n_seq, t_new, d)
    return (y_prompt, y_sample, jnp.stack(kvd_p, axis=2), jnp.stack(kvs_p, axis=2),
            jnp.stack(kvd_s, axis=2), jnp.stack(kvs_s, axis=2))
```

```python
import functools
import math

import numpy as np
import jax
import jax.numpy as jnp
from jax import lax
from jax.experimental import pallas as pl
from jax.experimental.pallas import tpu as pltpu

HEAD_DIM = 64
MAX_DISTANCE = 128
NORM_EPS = 1e-6
SUBLN_EPS = 1e-5
TOP_K_INNER = 2

LANES = 128
VMEM_LIMIT_BYTES = 56 * 1024 * 1024

DIFF_TILE = 256
SB_TILE = 256
SUM_ROWS = 16
SB_SKIP_LOG = -90.0
PAGES_PER_STEP = 32

F32 = jnp.float32
BF16 = jnp.bfloat16
NT_DIMS = (((1,), (1,)), ((), ()))
LOG2_E = math.log2(math.e)


def _lambda_init(layer):
    return 0.8 - 0.6 * math.exp(-0.3 * layer)


def _rel_bucket_np(rel, n_buckets):
    n = np.maximum(-rel, 0)
    max_exact = n_buckets // 2
    nf = np.maximum(n, 1).astype(np.float32)
    large = max_exact + (np.log(nf / np.float32(max_exact)) / np.float32(math.log(MAX_DISTANCE / max_exact))
                         * np.float32(n_buckets - max_exact)).astype(np.int32)
    large = np.minimum(large, n_buckets - 1)
    return np.where(n < max_exact, n, large).astype(np.int32)


def _params(sem):
    return pltpu.CompilerParams(dimension_semantics=sem, vmem_limit_bytes=VMEM_LIMIT_BYTES)


def _normed_proj_cols(x_ref, g_ref, w_ref, half_w):
    x = x_ref[...]
    ms = jnp.mean(x * x, axis=-1, keepdims=True)
    h = (x * lax.rsqrt(ms + NORM_EPS) * g_ref[...]).astype(BF16)
    return lambda i: jnp.dot(h, w_ref[:, i * half_w:(i + 1) * half_w], preferred_element_type=F32)


def _proj_prompt_kernel(x_ref, g_ref, w_ref, qd_ref, qs_ref, kvd_ref, kvst_ref, kd16_ref, vdt16_ref, ks16_ref,
                        vst16_ref, *, half_w, scale):
    cols = _normed_proj_cols(x_ref, g_ref, w_ref, half_w)
    tm = x_ref.shape[0]
    n_rows = 2 * half_w // LANES

    def store_diff_rows(first, val):
        for r in range(n_rows // 2):
            kvd_ref[pl.ds(first + r, tm, stride=n_rows), :] = val[:, r * LANES:(r + 1) * LANES]

    qd = cols(0)
    kd = cols(1)
    qd_ref[...] = (qd * (scale * LOG2_E)).astype(BF16)
    vd = cols(2)
    store_diff_rows(0, kd)
    kd16_ref[...] = kd.astype(BF16)
    qs = cols(3)
    store_diff_rows(n_rows // 2, vd)
    vdt16_ref[...] = vd.T.astype(BF16)
    ks = cols(4)
    qs_ref[...] = (qs * (scale * LOG2_E)).astype(BF16)
    vs = cols(5)
    ks16_ref[...] = ks.astype(BF16)
    kvst_ref[0, :half_w, :] = ks.T
    vst = vs.T
    kvst_ref[0, half_w:, :] = vst
    vst16_ref[...] = vst.astype(BF16)


def _project_prompt(x2d, g, w16, *, tm, batch, seq):
    n, d = x2d.shape
    half_w = w16.shape[1] // 6
    n_rows = 2 * half_w // LANES
    nbs = seq // tm
    row = lambda i: (i, 0)
    const = lambda i: (0, 0)
    colblk = lambda i: (0, i)
    out_shape = (jax.ShapeDtypeStruct((n, half_w), BF16), jax.ShapeDtypeStruct((n, half_w), BF16),
                 jax.ShapeDtypeStruct((n * n_rows, LANES), F32),
                 jax.ShapeDtypeStruct((batch, 2 * half_w, seq), F32),
                 jax.ShapeDtypeStruct((n, half_w), BF16), jax.ShapeDtypeStruct((half_w, n), BF16),
                 jax.ShapeDtypeStruct((n, half_w), BF16), jax.ShapeDtypeStruct((half_w, n), BF16))
    out_specs = (pl.BlockSpec((tm, half_w), row), pl.BlockSpec((tm, half_w), row),
                 pl.BlockSpec((tm * n_rows, LANES), row),
                 pl.BlockSpec((1, 2 * half_w, tm), lambda i: (i // nbs, 0, i % nbs)),
                 pl.BlockSpec((tm, half_w), row), pl.BlockSpec((half_w, tm), colblk),
                 pl.BlockSpec((tm, half_w), row), pl.BlockSpec((half_w, tm), colblk))
    return pl.pallas_call(
        functools.partial(_proj_prompt_kernel, half_w=half_w, scale=HEAD_DIM ** -0.5),
        out_shape=out_shape,
        grid=(n // tm,),
        in_specs=[pl.BlockSpec((tm, d), row), pl.BlockSpec((1, d), const),
                  pl.BlockSpec((d, 6 * half_w), const)],
        out_specs=out_specs,
        compiler_params=_params(("parallel",)),
        name="rmsnorm_qkv_proj_prompt",
    )(x2d, g.reshape(1, d), w16)


def _proj_sample_kernel(x_ref, g_ref, w_ref, qd_ref, qs_ref, kvd_ref, kvs_ref, *, half_w, scale):
    cols = _normed_proj_cols(x_ref, g_ref, w_ref, half_w)
    qd_ref[...] = cols(0) * scale
    qs_ref[...] = cols(3) * scale
    kvd_ref[:, :half_w] = cols(1)
    kvd_ref[:, half_w:] = cols(2)
    kvs_ref[:, :half_w] = cols(4)
    kvs_ref[:, half_w:] = cols(5)


def _project_sample(x2d, g, w16, *, tm):
    n, d = x2d.shape
    half_w = w16.shape[1] // 6
    row = lambda i: (i, 0)
    const = lambda i: (0, 0)
    return pl.pallas_call(
        functools.partial(_proj_sample_kernel, half_w=half_w, scale=HEAD_DIM ** -0.5),
        out_shape=(jax.ShapeDtypeStruct((n, half_w), F32), jax.ShapeDtypeStruct((n, half_w), F32),
                   jax.ShapeDtypeStruct((n, 2 * half_w), F32), jax.ShapeDtypeStruct((n, 2 * half_w), F32)),
        grid=(n // tm,),
        in_specs=[pl.BlockSpec((tm, d), row), pl.BlockSpec((1, d), const),
                  pl.BlockSpec((d, 6 * half_w), const)],
        out_specs=(pl.BlockSpec((tm, half_w), row), pl.BlockSpec((tm, half_w), row),
                   pl.BlockSpec((tm, 2 * half_w), row), pl.BlockSpec((tm, 2 * half_w), row)),
        compiler_params=_params(("parallel",)),
        name="rmsnorm_qkv_proj_sample",
    )(x2d, g.reshape(1, d), w16)


def _bias_select(bucket, rb_ref, h, n_buckets):
    out = jnp.zeros(bucket.shape, F32)
    for b in range(n_buckets):
        out = jnp.where(bucket == b, rb_ref[b, h], out)
    return out


def _prompt_bias_kernel(rb_ref, bucket_ref, out_ref, *, n_buckets):
    h = pl.program_id(0)
    t = bucket_ref.shape[1]
    for d in range(bucket_ref.shape[0]):
        tile = _bias_select(bucket_ref[d], rb_ref, h, n_buckets) * LOG2_E
        if d == 0:
            key = lax.broadcasted_iota(jnp.int32, tile.shape, 0)
            qry = lax.broadcasted_iota(jnp.int32, tile.shape, 1)
            tile = jnp.where(key <= qry, tile, -jnp.inf)
        out_ref[0, (1 - d) * t:(2 - d) * t, :] = jnp.concatenate([tile, tile], axis=1)


def _prompt_bias_tiles(rel_bias, t):
    n_buckets, n_heads = rel_bias.shape
    key = np.arange(t)[:, None]
    qry = np.arange(t)[None, :]
    bucket = np.stack([_rel_bucket_np(key - qry - d * t, n_buckets) for d in range(2)])
    return pl.pallas_call(
        functools.partial(_prompt_bias_kernel, n_buckets=n_buckets),
        out_shape=jax.ShapeDtypeStruct((n_heads, 2 * t, 2 * t), F32),
        grid=(n_heads,),
        in_specs=[pl.BlockSpec(memory_space=pltpu.SMEM),
                  pl.BlockSpec((2, t, t), lambda h: (0, 0, 0))],
        out_specs=pl.BlockSpec((1, 2 * t, 2 * t), lambda h: (h, 0, 0)),
        compiler_params=_params(("parallel",)),
        name="prompt_rel_bias_tiles",
    )(rel_bias.astype(F32), jnp.asarray(bucket))


def _sample_bias_kernel(rb_ref, bucket_ref, out_ref, *, n_buckets, n_heads):
    rows = bucket_ref.shape[0] // n_heads
    for h in range(n_heads):
        sl = slice(h * rows, (h + 1) * rows)
        out_ref[sl, :] = _bias_select(bucket_ref[sl, :], rb_ref, h, n_buckets)


def _sample_bias_table(rel_bias, past_len, page, t_new):
    n_buckets, n_heads = rel_bias.shape
    rows = n_heads * 2 * t_new
    q_pos = past_len + (np.arange(rows) % t_new)[:, None]
    far = np.full((rows, page), n_buckets - 1, np.int32)
    last = _rel_bucket_np((past_len - page + np.arange(page))[None, :] - q_pos, n_buckets)
    new = _rel_bucket_np((past_len + np.arange(page))[None, :] - q_pos, n_buckets)
    bucket = np.concatenate([far, last, new], axis=1)
    return pl.pallas_call(
        functools.partial(_sample_bias_kernel, n_buckets=n_buckets, n_heads=n_heads),
        out_shape=jax.ShapeDtypeStruct(bucket.shape, F32),
        in_specs=[pl.BlockSpec(memory_space=pltpu.SMEM), pl.BlockSpec(memory_space=pltpu.VMEM)],
        out_specs=pl.BlockSpec(memory_space=pltpu.VMEM),
        name="sample_rel_bias_table",
    )(rel_bias.astype(F32), jnp.asarray(bucket))


def _split_heads_rows(q):
    lane = lax.broadcasted_iota(jnp.int32, q.shape, 1)
    zero = jnp.zeros_like(q)
    return jnp.concatenate([jnp.where(lane < HEAD_DIM, q, zero), jnp.where(lane >= HEAD_DIM, q, zero)], axis=0)


def _lambda_value(lq1_ref, lk1_ref, lq2_ref, lk2_ref, lam_init):
    a = jnp.sum(lq1_ref[...] * lk1_ref[...], axis=-1, keepdims=True)
    b = jnp.sum(lq2_ref[...] * lk2_ref[...], axis=-1, keepdims=True)
    return jnp.exp(a) - jnp.exp(b) + lam_init


def _diff_prompt_kernel(rb_ref, lq1_ref, lk1_ref, lq2_ref, lk2_ref, g_ref, q_ref, k_ref, vt_ref, bias_ref,
                        o_ref, m_ref, acc_ref, s_ref, *, t, far_bucket, lam_init):
    qi = pl.program_id(1)
    n_heads = q_ref.shape[1] // LANES
    head = lambda h: slice(h * LANES, (h + 1) * LANES)
    qq = [_split_heads_rows(q_ref[:, head(h)]) for h in range(n_heads)]
    keys = lambda ki, nb: pl.ds(pl.multiple_of(ki * t, t), nb * t)

    def scores(h, ki, nb):
        return lax.dot_general(k_ref[keys(ki, nb), head(h)], qq[h], NT_DIMS,
                               preferred_element_type=F32)

    def accumulate(h, ki, nb, s, bias, first=False):
        if callable(bias):
            m_blk = jnp.max(s + bias(), axis=0, keepdims=True)
        else:
            m_blk = jnp.max(s, axis=0, keepdims=True) + bias
        if first:
            m_new = m_blk
        else:
            m_old = m_ref[h]
            m_new = jnp.maximum(m_old, m_blk)
        p = jnp.exp2(s - (m_new - (bias() if callable(bias) else bias)))
        vt1 = jnp.concatenate([vt_ref[head(h), keys(ki, nb)], jnp.ones((SUM_ROWS, nb * t), BF16)], axis=0)
        pv = jnp.dot(vt1, p.astype(BF16), preferred_element_type=F32)
        acc_ref[h] = pv if first else jnp.exp2(m_old - m_new) * acc_ref[h] + pv
        m_ref[h] = m_new

    def run_heads(ki, nb, s_first, bias, first, prefetch):
        s_next = s_first
        for h in range(n_heads):
            s_cur = s_next
            s_next = scores(h + 1, ki, nb) if h + 1 < n_heads else (prefetch() if prefetch else None)
            accumulate(h, ki, nb, s_cur, bias(h), first)
        return s_next

    @pl.when(qi == 0)
    def _():
        run_heads(0, 1, scores(0, 0, 1), lambda h: lambda: bias_ref[h, t:, :], True, None)

    @pl.when(qi >= 1)
    def _():
        s_ref[...] = run_heads(qi - 1, 2, scores(0, qi - 1, 2), lambda h: lambda: bias_ref[h], True,
                               lambda: scores(0, 0, 2))

    far_bias = [rb_ref[far_bucket, h] * LOG2_E for h in range(n_heads)]
    n_far = jnp.maximum(qi - 1, 0)

    def far_pair(i, carry):
        s_ref[...] = run_heads(2 * i, 2, s_ref[...], lambda h: far_bias[h], False,
                               lambda: scores(0, 2 * i + 2, 2))
        return carry

    lax.fori_loop(0, n_far // 2, far_pair, 0)

    @pl.when(n_far % 2 == 1)
    def _():
        run_heads(n_far - 1, 1, s_ref[:t, :], lambda h: far_bias[h], False, None)

    lam = _lambda_value(lq1_ref, lk1_ref, lq2_ref, lk2_ref, lam_init)
    for h in range(n_heads):
        o = acc_ref[h, :LANES, :] / acc_ref[h, LANES:LANES + 1, :]
        od = o[:, :t] - lam * o[:, t:]
        od = od * lax.rsqrt(jnp.mean(od * od, axis=0, keepdims=True) + SUBLN_EPS)
        od = od * g_ref[...] * (1.0 - lam_init)
        o_ref[:, head(h)] = od.T.astype(o_ref.dtype)


def _diff_prompt_attention(qd16, kd16, vdt16, bias_tiles, rel_bias, lam_params, subln_g, *, batch, seq, lam_init):
    n, width = qd16.shape
    n_heads = width // LANES
    t = DIFF_TILE
    nq = seq // t
    n_buckets = rel_bias.shape[0]
    assert int(_rel_bucket_np(np.array([-(t + 1)]), n_buckets)[0]) == n_buckets - 1
    small = lambda b, i: (0, 0)
    return pl.pallas_call(
        functools.partial(_diff_prompt_kernel, t=t, far_bucket=n_buckets - 1, lam_init=lam_init),
        out_shape=jax.ShapeDtypeStruct((n, width), BF16),
        grid=(batch, nq),
        in_specs=[pl.BlockSpec(memory_space=pltpu.SMEM)]
                 + [pl.BlockSpec((1, HEAD_DIM), small)] * 4
                 + [pl.BlockSpec((LANES, 1), small),
                    pl.BlockSpec((t, width), lambda b, i: (b * nq + i, 0)),
                    pl.BlockSpec((seq, width), lambda b, i: (b, 0)),
                    pl.BlockSpec((width, seq), lambda b, i: (0, b)),
                    pl.BlockSpec((n_heads, 2 * t, 2 * t), lambda b, i: (0, 0, 0), pipeline_mode=pl.Buffered(1))],
        out_specs=pl.BlockSpec((t, width), lambda b, i: (b * nq + i, 0)),
        scratch_shapes=[pltpu.VMEM((n_heads, 1, 2 * t), F32),
                        pltpu.VMEM((n_heads, LANES + SUM_ROWS, 2 * t), F32), pltpu.VMEM((2 * t, 2 * t), F32)],
        compiler_params=_params(("parallel", "arbitrary")),
        name="diff_attention_prompt",
    )(rel_bias.astype(F32), *lam_params, subln_g.reshape(LANES, 1), qd16, kd16, vdt16, bias_tiles)


def _log_sigmoid_pair(z):
    soft = jnp.log(1.0 + jnp.exp(-jnp.abs(z)))
    return jnp.minimum(z, 0.0) - soft, jnp.minimum(-z, 0.0) - soft


def _split_hi_lo(x):
    hi = x.astype(BF16)
    lo = (x - hi.astype(F32)).astype(BF16)
    return hi, lo


def _sb_prompt_kernel(q_ref, k_ref, vt_ref, o_ref, used_ref, acc_ref, z_ref, *, t):
    qi = pl.program_id(1)
    n_pairs = q_ref.shape[1] // LANES
    pair = lambda p: slice(p * LANES, (p + 1) * LANES)
    qq = [_split_heads_rows(q_ref[:, pair(p)]) for p in range(n_pairs)]
    row = lax.broadcasted_iota(jnp.int32, (t, t), 0)
    colk = lax.broadcasted_iota(jnp.int32, (t, t), 1)
    later = (colk > row).astype(BF16)
    later2 = jnp.concatenate([later, later], axis=1)

    def logits(p, ki):
        k = k_ref[pl.ds(pl.multiple_of(ki * t, t), t), pair(p)]
        return lax.dot_general(k, qq[p], NT_DIMS, preferred_element_type=F32)

    def stage_a(z, mask):
        soft = jnp.maximum(z, 0.0) + jnp.log2(1.0 + jnp.exp2(-jnp.abs(z)))
        log_beta = z - soft
        if mask is not None:
            soft = jnp.where(mask, soft, 0.0)
        hi, lo = _split_hi_lo(soft)
        after = jnp.dot(later2, jnp.concatenate([hi, lo], axis=0), preferred_element_type=F32)
        return log_beta, after, after[0:1, :] + soft[0:1, :]

    def stage_b(p, ki, mask, log_beta, after, total):
        used = used_ref[p]
        a = jnp.exp2(log_beta - after)
        if mask is not None:
            a = jnp.where(mask, a, 0.0)
        vt = vt_ref[pair(p), pl.ds(pl.multiple_of(ki * t, t), t)]
        acc_ref[p] += jnp.dot(vt, a.astype(BF16), preferred_element_type=F32) * jnp.exp2(-used)
        used_ref[p] = used + total

    def run_steps(steps, z_first, prefetch_ki):
        z_next = z_first
        pending = None
        for i, (p, ki, mask) in enumerate(steps):
            z_cur = z_next
            z_next = logits(steps[i + 1][0], steps[i + 1][1]) if i + 1 < len(steps) else logits(0, prefetch_ki)
            staged = stage_a(z_cur, mask)
            if pending is not None:
                stage_b(*pending)
            pending = (p, ki, mask) + staged
        stage_b(*pending)
        return z_next

    used_ref[...] = jnp.zeros_like(used_ref)
    acc_ref[...] = jnp.zeros_like(acc_ref)
    key = lax.broadcasted_iota(jnp.int32, (t, 2 * t), 0)
    col = lax.broadcasted_iota(jnp.int32, (t, 2 * t), 1)
    causal = key < jnp.where(col >= t, col - t, col)
    diagonal = [(p, qi, causal) for p in range(n_pairs)]

    @pl.when(qi == 0)
    def _():
        z_ref[...] = run_steps(diagonal, logits(0, qi), 0)

    @pl.when(qi >= 1)
    def _():
        z_ref[...] = run_steps(diagonal + [(p, qi - 1, None) for p in range(n_pairs)], logits(0, qi),
                               jnp.maximum(qi - 2, 0))

    def cond(c):
        ki, least_used = c
        return jnp.logical_and(ki >= 0, least_used < -SB_SKIP_LOG * LOG2_E)

    def body(c):
        ki, _ = c
        z_ref[...] = run_steps([(p, ki, None) for p in range(n_pairs)], z_ref[...], jnp.maximum(ki - 1, 0))
        return ki - 1, jnp.min(used_ref[...])

    lax.while_loop(cond, body, (qi - 2, jnp.min(used_ref[...])))

    for p in range(n_pairs):
        o = jnp.concatenate([acc_ref[p, :HEAD_DIM, :t], acc_ref[p, HEAD_DIM:, t:]], axis=0)
        o_ref[:, pair(p)] = o.T.astype(o_ref.dtype)


def _sb_prompt_attention(qs16, ks16, vst16, *, batch, seq):
    n, width = qs16.shape
    n_pairs = width // LANES
    t = SB_TILE
    nq = seq // t
    return pl.pallas_call(
        functools.partial(_sb_prompt_kernel, t=t),
        out_shape=jax.ShapeDtypeStruct((n, width), BF16),
        grid=(batch, nq),
        in_specs=[pl.BlockSpec((t, width), lambda b, i: (b * nq + i, 0)),
                  pl.BlockSpec((seq, width), lambda b, i: (b, 0)),
                  pl.BlockSpec((width, seq), lambda b, i: (0, b))],
        out_specs=pl.BlockSpec((t, width), lambda b, i: (b * nq + i, 0)),
        scratch_shapes=[pltpu.VMEM((n_pairs, 1, 2 * t), F32), pltpu.VMEM((n_pairs, LANES, 2 * t), F32),
                        pltpu.VMEM((t, 2 * t), F32)],
        compiler_params=_params(("parallel", "arbitrary")),
        name="stick_breaking_attention_prompt",
    )(qs16, ks16, vst16)


def _block_rows(q, group_w):
    t, w = q.shape
    n_groups = w // group_w
    qt = jnp.concatenate([q] * n_groups, axis=0)
    row_g = lax.broadcasted_iota(jnp.int32, qt.shape, 0) // t
    col_g = lax.broadcasted_iota(jnp.int32, qt.shape, 1) // group_w
    return jnp.where(row_g == col_g, qt, 0.0).astype(BF16)


def _pad_rows(x, rows):
    return jnp.concatenate([x, jnp.zeros((rows - x.shape[0], x.shape[1]), x.dtype)], axis=0)


def _gather_head_lanes(acc, n_groups, t):
    w = acc.shape[1]
    col_g = lax.broadcasted_iota(jnp.int32, (t, w), 1) // (w // n_groups)
    out = jnp.zeros((t, w), acc.dtype)
    for g in range(n_groups):
        out = jnp.where(col_g == g, acc[g * t:(g + 1) * t, :], out)
    return out


def _sample_attention_kernel(pt_ref, lq1_ref, lk1_ref, lq2_ref, lk2_ref, g_ref, q_ref, kvn_ref, bias_ref,
                             qs_ref, kvns_ref, cache_s_ref, *rest, n_pages_step, page, half_w, t_new, lam_init,
                             row0, slot_rows, sb_row0):
    page_refs = rest[:n_pages_step]
    o_ref, osb_ref, qb_ref, m_ref, l_ref, acc_ref = rest[n_pages_step:n_pages_step + 6]
    sb_scratch = rest[n_pages_step + 6:]
    j = pl.program_id(1)
    n_steps = pl.num_programs(1)
    n_heads = half_w // LANES

    def accumulate(k, v, bias, mask):
        s = lax.dot_general(qb_ref[...], k, NT_DIMS, preferred_element_type=F32) + bias
        if mask is not None:
            s = jnp.where(mask, s, -jnp.inf)
        m_old = m_ref[...]
        m_new = jnp.maximum(m_old, jnp.max(s, axis=1, keepdims=True))
        alpha = jnp.exp(m_old - m_new)
        p = jnp.exp(s - m_new)
        l_ref[...] = alpha * l_ref[...] + jnp.sum(p, axis=1, keepdims=True)
        acc_ref[...] = alpha * acc_ref[...] + jnp.dot(p.astype(BF16), v, preferred_element_type=F32)
        m_ref[...] = m_new

    @pl.when(j == 0)
    def _():
        qb_ref[...] = _block_rows(q_ref[...], HEAD_DIM)
        rows = qb_ref.shape[0]
        m_ref[...] = jnp.full(m_ref.shape, -jnp.inf, F32)
        l_ref[...] = jnp.zeros_like(l_ref)
        acc_ref[...] = jnp.zeros_like(acc_ref)
        kvn = _pad_rows(kvn_ref[...], page)
        tq = lax.broadcasted_iota(jnp.int32, (rows, page), 0) % t_new
        tk = lax.broadcasted_iota(jnp.int32, (rows, page), 1)
        accumulate(kvn[:, :half_w].astype(BF16), kvn[:, half_w:].astype(BF16),
                   bias_ref[:, 2 * page:3 * page], tk <= tq)
        _sb_sample_walk(pl.program_id(0), pt_ref, qs_ref, kvns_ref, cache_s_ref, osb_ref, *sb_scratch,
                        n_pages=pt_ref.shape[1], page=page, half_w=half_w, t_new=t_new, row0=sb_row0)

    def heads(ref, first):
        parts = [ref[0, pl.ds(row0 + first + h, page, stride=slot_rows), :] for h in range(n_heads)]
        return jnp.concatenate(parts, axis=1).astype(BF16)

    far = bias_ref[:, 0:page]
    last = jnp.where(j == n_steps - 1, bias_ref[:, page:2 * page], far)
    accumulate(jnp.concatenate([heads(r, 0) for r in page_refs], axis=0),
               jnp.concatenate([heads(r, n_heads) for r in page_refs], axis=0),
               jnp.concatenate([far] * (n_pages_step - 1) + [last], axis=1), None)

    @pl.when(j == n_steps - 1)
    def _():
        o = acc_ref[...] / l_ref[...]
        o = _gather_head_lanes(o, n_heads, 2 * t_new)
        lam = _lambda_value(lq1_ref, lk1_ref, lq2_ref, lk2_ref, lam_init)
        od = o[:t_new] - lam * o[t_new:]
        outs = []
        for h in range(n_heads):
            oh = od[:, h * LANES:(h + 1) * LANES]
            oh = oh * lax.rsqrt(jnp.mean(oh * oh, axis=-1, keepdims=True) + SUBLN_EPS)
            outs.append(oh * g_ref[...] * (1.0 - lam_init))
        o_ref[...] = jnp.concatenate(outs, axis=1).astype(o_ref.dtype)


def _sample_attention(q32, kvn32, cache_rows, qs32, kvns32, cache_t, layer, depth, page_table, bias_tab,
                      lam_params, subln_g, *, t_new, lam_init):
    n, half_w = q32.shape
    n_seq, n_pages = page_table.shape
    n_heads = half_w // LANES
    slot_rows = depth * 2 * n_heads
    page = cache_rows.shape[1] // slot_rows
    gp = PAGES_PER_STEP
    rows = n_heads * 2 * t_new
    sb_rows = (half_w // HEAD_DIM) * t_new
    small = lambda b, j, pt: (0, 0)
    seq_row = lambda b, j, pt: (b, 0)

    def page_spec(g):
        return pl.BlockSpec((1, page * slot_rows, LANES), lambda b, j, pt: (pt[b, j * gp + g], 0, 0))

    return pl.pallas_call(
        functools.partial(_sample_attention_kernel, n_pages_step=gp, page=page, half_w=half_w, t_new=t_new,
                          lam_init=lam_init, row0=layer * 2 * n_heads, slot_rows=slot_rows,
                          sb_row0=layer * 2 * half_w),
        out_shape=(jax.ShapeDtypeStruct((n, half_w), F32), jax.ShapeDtypeStruct((n, half_w), F32)),
        grid_spec=pltpu.PrefetchScalarGridSpec(
            num_scalar_prefetch=1,
            grid=(n_seq, n_pages // gp),
            in_specs=[pl.BlockSpec((1, HEAD_DIM), small)] * 4
                     + [pl.BlockSpec((1, LANES), small),
                        pl.BlockSpec((t_new, half_w), seq_row),
                        pl.BlockSpec((t_new, 2 * half_w), seq_row),
                        pl.BlockSpec((rows, 3 * page), small),
                        pl.BlockSpec((t_new, half_w), seq_row),
                        pl.BlockSpec((t_new, 2 * half_w), seq_row),
                        pl.BlockSpec(memory_space=pl.ANY)]
                     + [page_spec(g) for g in range(gp)],
            out_specs=(pl.BlockSpec((t_new, half_w), seq_row), pl.BlockSpec((t_new, half_w), seq_row)),
            scratch_shapes=[pltpu.VMEM((rows, half_w), BF16), pltpu.VMEM((rows, 1), F32),
                            pltpu.VMEM((rows, 1), F32), pltpu.VMEM((rows, half_w), F32),
                            pltpu.VMEM((sb_rows, half_w), BF16), pltpu.VMEM((sb_rows, 1), F32),
                            pltpu.VMEM((sb_rows, half_w), F32), pltpu.VMEM((2, 2 * half_w, cache_t.shape[2]), F32),
                            pltpu.SemaphoreType.DMA((2,))]),
        compiler_params=_params(("arbitrary", "arbitrary")),
        name="attention_sample",
    )(page_table, *lam_params, subln_g.reshape(1, LANES), q32, kvn32, bias_tab, qs32, kvns32, cache_t,
      *([cache_rows] * gp))


def _sb_sample_walk(b, pt_ref, q_ref, kvn_ref, cache_ref, o_ref, qb_ref, run_ref, acc_ref, buf_ref, sem_ref, *,
                    n_pages, page, half_w, t_new, row0):
    def page_copy(j, slot):
        return pltpu.make_async_copy(cache_ref.at[pt_ref[b, j], pl.ds(row0, 2 * half_w), :],
                                     buf_ref.at[slot], sem_ref.at[slot])

    slot_of = lambda j: (n_pages - 1 - j) & 1
    page_copy(n_pages - 1, 0).start()
    row = lax.broadcasted_iota(jnp.int32, (page, page), 0)
    colk = lax.broadcasted_iota(jnp.int32, (page, page), 1)
    later = (row > colk).astype(BF16)
    later2 = jnp.concatenate([later, later], axis=0)

    def accumulate(z, pv, mask):
        log_beta, log_rest = _log_sigmoid_pair(z)
        if mask is not None:
            log_rest = jnp.where(mask, log_rest, 0.0)
        hi, lo = _split_hi_lo(log_rest)
        suffix = jnp.dot(jnp.concatenate([hi, lo], axis=1), later2, preferred_element_type=F32)
        a = jnp.exp(log_beta + suffix + run_ref[...])
        if mask is not None:
            a = jnp.where(mask, a, 0.0)
        acc_ref[...] += pv(a.astype(BF16))
        run_ref[...] += jnp.sum(log_rest, axis=1, keepdims=True)

    qb_ref[...] = _block_rows(q_ref[...], HEAD_DIM)
    rows = qb_ref.shape[0]
    run_ref[...] = jnp.zeros_like(run_ref)
    acc_ref[...] = jnp.zeros_like(acc_ref)
    kvn = _pad_rows(kvn_ref[...], page)
    k_new = kvn[:, :half_w].astype(BF16)
    v_new = kvn[:, half_w:].astype(BF16)
    tq = lax.broadcasted_iota(jnp.int32, (rows, page), 0) % t_new
    tk = lax.broadcasted_iota(jnp.int32, (rows, page), 1)
    accumulate(lax.dot_general(qb_ref[...], k_new, NT_DIMS, preferred_element_type=F32),
               lambda a: jnp.dot(a, v_new, preferred_element_type=F32), tk < tq)

    def cond(c):
        j, live = c
        return jnp.logical_and(j >= 0, live > SB_SKIP_LOG)

    def body(c):
        j, _ = c
        slot = slot_of(j)
        page_copy(j, slot).wait()

        @pl.when(j >= 1)
        def _():
            page_copy(j - 1, 1 - slot).start()

        kt = buf_ref[slot, :half_w, :].astype(BF16)
        vt = buf_ref[slot, half_w:, :].astype(BF16)
        accumulate(jnp.dot(qb_ref[...], kt, preferred_element_type=F32),
                   lambda a: lax.dot_general(a, vt, NT_DIMS, preferred_element_type=F32), None)
        return j - 1, jnp.max(run_ref[...])

    j_end, _ = lax.while_loop(cond, body, (n_pages - 1, jnp.max(run_ref[...])))

    @pl.when(j_end >= 0)
    def _():
        page_copy(j_end, slot_of(j_end)).wait()

    n_heads = half_w // HEAD_DIM
    o_ref[...] = _gather_head_lanes(acc_ref[...], n_heads, t_new).astype(o_ref.dtype)


def _router_gates(logits, n_groups, n_experts):
    lane = lax.broadcasted_iota(jnp.int32, logits.shape, 1).astype(F32)
    neg = -jnp.inf
    big = float(LANES)
    is_group = lane < n_groups
    gl = jnp.where(is_group, logits, neg)
    gmax = jnp.max(gl, axis=1, keepdims=True)
    pg_sel = 1.0 / jnp.sum(jnp.where(is_group, jnp.exp(gl - gmax), 0.0), axis=1, keepdims=True)
    g = jnp.min(jnp.where(gl == gmax, lane, big), axis=1, keepdims=True)
    lo = n_groups + g * n_experts
    in_group = jnp.logical_and(lane >= lo, lane < lo + n_experts)
    el = jnp.where(in_group, logits, neg)
    v1 = jnp.max(el, axis=1, keepdims=True)
    i1 = jnp.min(jnp.where(el == v1, lane, big), axis=1, keepdims=True)
    el2 = jnp.where(lane == i1, neg, el)
    v2 = jnp.max(el2, axis=1, keepdims=True)
    i2 = jnp.min(jnp.where(el2 == v2, lane, big), axis=1, keepdims=True)
    e2 = jnp.exp(v2 - v1)
    w1 = 1.0 / (1.0 + e2)
    w2 = e2 * w1
    return pg_sel * (jnp.where(lane == i1, w1, 0.0) + jnp.where(lane == i2, w2, 0.0))


def _post_kernel(x_ref, od_ref, os_ref, wo_ref, gf_ref, wr_ref, br_ref, wg_ref, wu_ref, wd_ref, gfin_ref,
                 y_ref, act_ref, *, n_groups, n_experts, final_norm):
    half = od_ref.shape[1]
    attn = (jnp.dot(od_ref[...], wo_ref[:half, :], preferred_element_type=F32)
            + jnp.dot(os_ref[...], wo_ref[half:, :], preferred_element_type=F32))
    x2 = x_ref[...] + attn
    hf = x2 * lax.rsqrt(jnp.mean(x2 * x2, axis=-1, keepdims=True) + NORM_EPS) * gf_ref[...]
    h, h_lo = _split_hi_lo(hf)
    both = jnp.dot(h, wr_ref[...], preferred_element_type=F32)
    logits = (both[:, :LANES] + both[:, LANES:]
              + jnp.dot(h_lo, wr_ref[:, :LANES], preferred_element_type=F32) + br_ref[...])
    gates = _router_gates(logits, n_groups, n_experts)

    n_total, _, d_exp = wg_ref.shape
    hidden = lambda e: (jnp.dot(h, wg_ref[e], preferred_element_type=F32),
                        jnp.dot(h, wu_ref[e], preferred_element_type=F32))
    nxt = hidden(0)
    for e in range(n_total):
        hg, hu = nxt
        if e + 1 < n_total:
            nxt = hidden(e + 1)
        gate = gates[:, n_groups + e:n_groups + e + 1]
        act_ref[:, e * d_exp:(e + 1) * d_exp] = (hg * (1.0 / (1.0 + jnp.exp(-hg))) * hu * gate).astype(BF16)
    x3 = x2 + jnp.dot(act_ref[...], wd_ref[...], preferred_element_type=F32)
    if final_norm:
        x3 = x3 * lax.rsqrt(jnp.mean(x3 * x3, axis=-1, keepdims=True) + NORM_EPS) * gfin_ref[...]
    y_ref[...] = x3


def _post_attention(x2d, od16, os16, wo16, g_ffn, wr16, b_router, wg16, wu16, wd16, g_final, *,
                    tm, n_groups, n_experts, final_norm):
    n, d = x2d.shape
    half = od16.shape[1]
    n_total, _, d_exp = wg16.shape
    row = lambda i: (i, 0)
    once = dict(pipeline_mode=pl.Buffered(1))
    const2 = lambda i: (0, 0)
    const3 = lambda i: (0, 0, 0)
    return pl.pallas_call(
        functools.partial(_post_kernel, n_groups=n_groups, n_experts=n_experts, final_norm=final_norm),
        out_shape=jax.ShapeDtypeStruct((n, d), F32),
        grid=(n // tm,),
        in_specs=[pl.BlockSpec((tm, d), row), pl.BlockSpec((tm, half), row), pl.BlockSpec((tm, half), row),
                  pl.BlockSpec((d, d), const2, **once), pl.BlockSpec((1, d), const2),
                  pl.BlockSpec((d, 2 * LANES), const2, **once), pl.BlockSpec((1, LANES), const2),
                  pl.BlockSpec((n_total, d, d_exp), const3, **once),
                  pl.BlockSpec((n_total, d, d_exp), const3, **once),
                  pl.BlockSpec((n_total * d_exp, d), const2, **once),
                  pl.BlockSpec((1, d), const2)],
        out_specs=pl.BlockSpec((tm, d), row),
        scratch_shapes=[pltpu.VMEM((tm, n_total * d_exp), BF16)],
        compiler_params=_params(("parallel",)),
        name="out_proj_hmoe",
    )(x2d, od16, os16, wo16, g_ffn.reshape(1, d), wr16, b_router, wg16, wu16, wd16, g_final.reshape(1, d))


def _row_tile(n, pref):
    tm = min(n, pref)
    while n % tm:
        tm //= 2
    return tm


def kernel(x_prompt, x_sample, cache_kv_diff, cache_kv_sb, page_table, rel_bias, g_mix, w_in, lambda_q1,
           lambda_k1, lambda_q2, lambda_k2, subln_g, w_out, g_ffn, w_group, b_group, w_erouter, b_erouter,
           w_gate, w_up, w_down, g_final):
    batch, seq, d = x_prompt.shape
    n_seq, t_new, _ = x_sample.shape
    n_pool, page, depth, _, h_diff, diff_hw = cache_kv_diff.shape
    _, _, _, _, h_sb, sb_hw = cache_kv_sb.shape
    n_pages = page_table.shape[1]
    past_len = n_pages * page
    n_groups, n_experts = w_erouter.shape[2], w_erouter.shape[3]
    half_w = h_diff * diff_hw
    assert diff_hw == 2 * HEAD_DIM == LANES and sb_hw == HEAD_DIM and h_sb * sb_hw == half_w
    assert seq % DIFF_TILE == 0 and seq % SB_TILE == 0 and n_pages % PAGES_PER_STEP == 0 and page == LANES
    assert n_groups + n_groups * n_experts <= LANES

    xp = x_prompt.reshape(batch * seq, d)
    xs = x_sample.reshape(n_seq * t_new, d)
    cache_d = cache_kv_diff.reshape(n_pool, page * depth * 2 * h_diff, diff_hw)
    cache_s = jnp.transpose(cache_kv_sb, (0, 2, 3, 4, 5, 1)).reshape(n_pool, depth * 2 * half_w, page)
    bias_tiles = _prompt_bias_tiles(rel_bias, DIFF_TILE)
    bias_tab = _sample_bias_table(rel_bias, past_len, page, t_new)
    tm_p = _row_tile(seq, 1024)
    tm_s = _row_tile(n_seq * t_new, 512)
    tm_moe_p = _row_tile(batch * seq, 512)

    kvd_p, kvs_p, kvd_s, kvs_s = [], [], [], []
    for l in range(depth):
        li = _lambda_init(l)
        lam_params = [a[l].reshape(1, HEAD_DIM).astype(F32) for a in (lambda_q1, lambda_k1, lambda_q2, lambda_k2)]
        w16 = w_in[l].astype(BF16)
        wo16 = w_out[l].astype(BF16)
        wr = jnp.concatenate([w_group[l].astype(F32), w_erouter[l].astype(F32).reshape(d, n_groups * n_experts)],
                             axis=1)
        wr = jnp.pad(wr, ((0, 0), (0, LANES - wr.shape[1])))
        wr_hi = wr.astype(BF16)
        wr16 = jnp.concatenate([wr_hi, (wr - wr_hi.astype(F32)).astype(BF16)], axis=1)
        br = jnp.concatenate([b_group[l].astype(F32), b_erouter[l].astype(F32).reshape(-1)])
        br = jnp.pad(br, (0, LANES - br.shape[0])).reshape(1, LANES)
        d_exp = w_gate.shape[-1]
        wg16 = w_gate[l].astype(BF16).reshape(n_groups * n_experts, d, d_exp)
        wu16 = w_up[l].astype(BF16).reshape(n_groups * n_experts, d, d_exp)
        wd16 = w_down[l].astype(BF16).reshape(n_groups * n_experts * d_exp, d)
        last = l == depth - 1

        qd16, qs16, kvd, kvst, kd16, vdt16, ks16, vst16 = _project_prompt(xp, g_mix[l], w16, tm=tm_p, batch=batch,
                                                                         seq=seq)
        od16 = _diff_prompt_attention(qd16, kd16, vdt16, bias_tiles, rel_bias, lam_params, subln_g[l],
                                      batch=batch, seq=seq, lam_init=li)
        os16 = _sb_prompt_attention(qs16, ks16, vst16, batch=batch, seq=seq)
        xp = _post_attention(xp, od16, os16, wo16, g_ffn[l], wr16, br, wg16, wu16, wd16, g_final,
                             tm=tm_moe_p, n_groups=n_groups, n_experts=n_experts, final_norm=last)
        kvd_p.append(kvd.reshape(batch, seq, 2, h_diff, diff_hw))
        kvs_p.append(jnp.transpose(kvst.reshape(batch, 2, h_sb, sb_hw, seq), (0, 4, 1, 2, 3)))

        qd32, qs32, kvd, kvs = _project_sample(xs, g_mix[l], w16, tm=tm_s)
        od, osb = _sample_attention(qd32, kvd, cache_d, qs32, kvs, cache_s, l, depth, page_table, bias_tab,
                                    lam_params, subln_g[l], t_new=t_new, lam_init=li)
        xs = _post_attention(xs, od.astype(BF16), osb.astype(BF16), wo16, g_ffn[l], wr16, br, wg16, wu16,
                             wd16, g_final, tm=tm_s, n_groups=n_groups, n_experts=n_experts, final_norm=last)
        kvd_s.append(kvd.reshape(n_seq, t_new, 2, h_diff, diff_hw))
        kvs_s.append(kvs.reshape(n_seq, t_new, 2, h_sb, sb_hw))

    y_prompt = xp.reshape(batch, seq, d)
    y_sample = xs.reshape(n_seq, t_new, d)
    return (y_prompt, y_sample, jnp.stack(kvd_p, axis=2), jnp.stack(kvs_p, axis=2),
            jnp.stack(kvd_s, axis=2), jnp.stack(kvs_s, axis=2))
```

```python
import functools
import math

import numpy as np
import jax
import jax.numpy as jnp
from jax import lax
from jax.experimental import pallas as pl
from jax.experimental.pallas import tpu as pltpu

HEAD_DIM = 64
MAX_DISTANCE = 128
NORM_EPS = 1e-6
SUBLN_EPS = 1e-5
TOP_K_INNER = 2

LANES = 128
VMEM_LIMIT_BYTES = 56 * 1024 * 1024

DIFF_TILE = 256
SB_TILE = 256
SUM_ROWS = 16
SB_SKIP_LOG = -90.0
PAGES_PER_STEP = 32

F32 = jnp.float32
BF16 = jnp.bfloat16
NT_DIMS = (((1,), (1,)), ((), ()))
LOG2_E = math.log2(math.e)


def _lambda_init(layer):
    return 0.8 - 0.6 * math.exp(-0.3 * layer)


def _rel_bucket_np(rel, n_buckets):
    n = np.maximum(-rel, 0)
    max_exact = n_buckets // 2
    nf = np.maximum(n, 1).astype(np.float32)
    large = max_exact + (np.log(nf / np.float32(max_exact)) / np.float32(math.log(MAX_DISTANCE / max_exact))
                         * np.float32(n_buckets - max_exact)).astype(np.int32)
    large = np.minimum(large, n_buckets - 1)
    return np.where(n < max_exact, n, large).astype(np.int32)


def _params(sem):
    return pltpu.CompilerParams(dimension_semantics=sem, vmem_limit_bytes=VMEM_LIMIT_BYTES)


def _normed_proj_cols(x_ref, g_ref, w_ref, half_w):
    x = x_ref[...]
    ms = jnp.mean(x * x, axis=-1, keepdims=True)
    h = (x * lax.rsqrt(ms + NORM_EPS) * g_ref[...]).astype(BF16)
    return lambda i: jnp.dot(h, w_ref[:, i * half_w:(i + 1) * half_w], preferred_element_type=F32)


def _proj_prompt_kernel(x_ref, g_ref, w_ref, qd_ref, qs_ref, kvd_ref, kvst_ref, kd16_ref, vdt16_ref, ks16_ref,
                        vst16_ref, *, half_w, scale):
    cols = _normed_proj_cols(x_ref, g_ref, w_ref, half_w)
    tm = x_ref.shape[0]
    n_rows = 2 * half_w // LANES

    def store_diff_rows(first, val):
        for r in range(n_rows // 2):
            kvd_ref[pl.ds(first + r, tm, stride=n_rows), :] = val[:, r * LANES:(r + 1) * LANES]

    qd = cols(0)
    kd = cols(1)
    qd_ref[...] = (qd * (scale * LOG2_E)).astype(BF16)
    vd = cols(2)
    store_diff_rows(0, kd)
    kd16_ref[...] = kd.astype(BF16)
    qs = cols(3)
    store_diff_rows(n_rows // 2, vd)
    vdt16_ref[...] = vd.T.astype(BF16)
    ks = cols(4)
    qs_ref[...] = (qs * (scale * LOG2_E)).astype(BF16)
    vs = cols(5)
    ks16_ref[...] = ks.astype(BF16)
    kvst_ref[0, :half_w, :] = ks.T
    vst = vs.T
    kvst_ref[0, half_w:, :] = vst
    vst16_ref[...] = vst.astype(BF16)


def _project_prompt(x2d, g, w16, *, tm, batch, seq):
    n, d = x2d.shape
    half_w = w16.shape[1] // 6
    n_rows = 2 * half_w // LANES
    nbs = seq // tm
    row = lambda i: (i, 0)
    const = lambda i: (0, 0)
    colblk = lambda i: (0, i)
    out_shape = (jax.ShapeDtypeStruct((n, half_w), BF16), jax.ShapeDtypeStruct((n, half_w), BF16),
                 jax.ShapeDtypeStruct((n * n_rows, LANES), F32),
                 jax.ShapeDtypeStruct((batch, 2 * half_w, seq), F32),
                 jax.ShapeDtypeStruct((n, half_w), BF16), jax.ShapeDtypeStruct((half_w, n), BF16),
                 jax.ShapeDtypeStruct((n, half_w), BF16), jax.ShapeDtypeStruct((half_w, n), BF16))
    out_specs = (pl.BlockSpec((tm, half_w), row), pl.BlockSpec((tm, half_w), row),
                 pl.BlockSpec((tm * n_rows, LANES), row),
                 pl.BlockSpec((1, 2 * half_w, tm), lambda i: (i // nbs, 0, i % nbs)),
                 pl.BlockSpec((tm, half_w), row), pl.BlockSpec((half_w, tm), colblk),
                 pl.BlockSpec((tm, half_w), row), pl.BlockSpec((half_w, tm), colblk))
    return pl.pallas_call(
        functools.partial(_proj_prompt_kernel, half_w=half_w, scale=HEAD_DIM ** -0.5),
        out_shape=out_shape,
        grid=(n // tm,),
        in_specs=[pl.BlockSpec((tm, d), row), pl.BlockSpec((1, d), const),
                  pl.BlockSpec((d, 6 * half_w), const)],
        out_specs=out_specs,
        compiler_params=_params(("parallel",)),
        name="rmsnorm_qkv_proj_prompt",
    )(x2d, g.reshape(1, d), w16)


def _proj_sample_kernel(x_ref, g_ref, w_ref, qd_ref, qs_ref, kvd_ref, kvs_ref, *, half_w, scale):
    cols = _normed_proj_cols(x_ref, g_ref, w_ref, half_w)
    qd_ref[...] = cols(0) * scale
    qs_ref[...] = cols(3) * scale
    kvd_ref[:, :half_w] = cols(1)
    kvd_ref[:, half_w:] = cols(2)
    kvs_ref[:, :half_w] = cols(4)
    kvs_ref[:, half_w:] = cols(5)


def _project_sample(x2d, g, w16, *, tm):
    n, d = x2d.shape
    half_w = w16.shape[1] // 6
    row = lambda i: (i, 0)
    const = lambda i: (0, 0)
    return pl.pallas_call(
        functools.partial(_proj_sample_kernel, half_w=half_w, scale=HEAD_DIM ** -0.5),
        out_shape=(jax.ShapeDtypeStruct((n, half_w), F32), jax.ShapeDtypeStruct((n, half_w), F32),
                   jax.ShapeDtypeStruct((n, 2 * half_w), F32), jax.ShapeDtypeStruct((n, 2 * half_w), F32)),
        grid=(n // tm,),
        in_specs=[pl.BlockSpec((tm, d), row), pl.BlockSpec((1, d), const),
                  pl.BlockSpec((d, 6 * half_w), const)],
        out_specs=(pl.BlockSpec((tm, half_w), row), pl.BlockSpec((tm, half_w), row),
                   pl.BlockSpec((tm, 2 * half_w), row), pl.BlockSpec((tm, 2 * half_w), row)),
        compiler_params=_params(("parallel",)),
        name="rmsnorm_qkv_proj_sample",
    )(x2d, g.reshape(1, d), w16)


def _bias_select(bucket, rb_ref, h, n_buckets):
    out = jnp.zeros(bucket.shape, F32)
    for b in range(n_buckets):
        out = jnp.where(bucket == b, rb_ref[b, h], out)
    return out


def _prompt_bias_kernel(rb_ref, bucket_ref, out_ref, *, n_buckets):
    h = pl.program_id(0)
    t = bucket_ref.shape[1]
    for d in range(bucket_ref.shape[0]):
        tile = _bias_select(bucket_ref[d], rb_ref, h, n_buckets) * LOG2_E
        if d == 0:
            key = lax.broadcasted_iota(jnp.int32, tile.shape, 0)
            qry = lax.broadcasted_iota(jnp.int32, tile.shape, 1)
            tile = jnp.where(key <= qry, tile, -jnp.inf)
        out_ref[0, (1 - d) * t:(2 - d) * t, :] = jnp.concatenate([tile, tile], axis=1)


def _prompt_bias_tiles(rel_bias, t):
    n_buckets, n_heads = rel_bias.shape
    key = np.arange(t)[:, None]
    qry = np.arange(t)[None, :]
    bucket = np.stack([_rel_bucket_np(key - qry - d * t, n_buckets) for d in range(2)])
    return pl.pallas_call(
        functools.partial(_prompt_bias_kernel, n_buckets=n_buckets),
        out_shape=jax.ShapeDtypeStruct((n_heads, 2 * t, 2 * t), F32),
        grid=(n_heads,),
        in_specs=[pl.BlockSpec(memory_space=pltpu.SMEM),
                  pl.BlockSpec((2, t, t), lambda h: (0, 0, 0))],
        out_specs=pl.BlockSpec((1, 2 * t, 2 * t), lambda h: (h, 0, 0)),
        compiler_params=_params(("parallel",)),
        name="prompt_rel_bias_tiles",
    )(rel_bias.astype(F32), jnp.asarray(bucket))


def _sample_bias_kernel(rb_ref, bucket_ref, out_ref, *, n_buckets, n_heads):
    rows = bucket_ref.shape[0] // n_heads
    for h in range(n_heads):
        sl = slice(h * rows, (h + 1) * rows)
        out_ref[sl, :] = _bias_select(bucket_ref[sl, :], rb_ref, h, n_buckets)


def _sample_bias_table(rel_bias, past_len, page, t_new):
    n_buckets, n_heads = rel_bias.shape
    rows = n_heads * 2 * t_new
    q_pos = past_len + (np.arange(rows) % t_new)[:, None]
    far = np.full((rows, page), n_buckets - 1, np.int32)
    last = _rel_bucket_np((past_len - page + np.arange(page))[None, :] - q_pos, n_buckets)
    new = _rel_bucket_np((past_len + np.arange(page))[None, :] - q_pos, n_buckets)
    bucket = np.concatenate([far, last, new], axis=1)
    return pl.pallas_call(
        functools.partial(_sample_bias_kernel, n_buckets=n_buckets, n_heads=n_heads),
        out_shape=jax.ShapeDtypeStruct(bucket.shape, F32),
        in_specs=[pl.BlockSpec(memory_space=pltpu.SMEM), pl.BlockSpec(memory_space=pltpu.VMEM)],
        out_specs=pl.BlockSpec(memory_space=pltpu.VMEM),
        name="sample_rel_bias_table",
    )(rel_bias.astype(F32), jnp.asarray(bucket))


def _split_heads_rows(q):
    lane = lax.broadcasted_iota(jnp.int32, q.shape, 1)
    zero = jnp.zeros_like(q)
    return jnp.concatenate([jnp.where(lane < HEAD_DIM, q, zero), jnp.where(lane >= HEAD_DIM, q, zero)], axis=0)


def _lambda_value(lq1_ref, lk1_ref, lq2_ref, lk2_ref, lam_init):
    a = jnp.sum(lq1_ref[...] * lk1_ref[...], axis=-1, keepdims=True)
    b = jnp.sum(lq2_ref[...] * lk2_ref[...], axis=-1, keepdims=True)
    return jnp.exp(a) - jnp.exp(b) + lam_init


def _diff_prompt_kernel(rb_ref, lq1_ref, lk1_ref, lq2_ref, lk2_ref, g_ref, q_ref, k_ref, vt_ref, bias_ref,
                        o_ref, m_ref, acc_ref, s_ref, *, t, far_bucket, lam_init):
    qi = pl.program_id(1)
    n_heads = q_ref.shape[1] // LANES
    head = lambda h: slice(h * LANES, (h + 1) * LANES)
    qq = [_split_heads_rows(q_ref[:, head(h)]) for h in range(n_heads)]
    keys = lambda ki, nb: pl.ds(pl.multiple_of(ki * t, t), nb * t)

    def scores(h, ki, nb):
        return lax.dot_general(k_ref[keys(ki, nb), head(h)], qq[h], NT_DIMS,
                               preferred_element_type=F32)

    def accumulate(h, ki, nb, s, bias, first=False):
        if callable(bias):
            m_blk = jnp.max(s + bias(), axis=0, keepdims=True)
        else:
            m_blk = jnp.max(s, axis=0, keepdims=True) + bias
        if first:
            m_new = m_blk
        else:
            m_old = m_ref[h]
            m_new = jnp.maximum(m_old, m_blk)
        p = jnp.exp2(s - (m_new - (bias() if callable(bias) else bias)))
        vt1 = jnp.concatenate([vt_ref[head(h), keys(ki, nb)], jnp.ones((SUM_ROWS, nb * t), BF16)], axis=0)
        pv = jnp.dot(vt1, p.astype(BF16), preferred_element_type=F32)
        acc_ref[h] = pv if first else jnp.exp2(m_old - m_new) * acc_ref[h] + pv
        m_ref[h] = m_new

    def run_heads(ki, nb, s_first, bias, first, prefetch):
        s_next = s_first
        for h in range(n_heads):
            s_cur = s_next
            s_next = scores(h + 1, ki, nb) if h + 1 < n_heads else (prefetch() if prefetch else None)
            accumulate(h, ki, nb, s_cur, bias(h), first)
        return s_next

    @pl.when(qi == 0)
    def _():
        run_heads(0, 1, scores(0, 0, 1), lambda h: lambda: bias_ref[h, t:, :], True, None)

    @pl.when(qi >= 1)
    def _():
        s_ref[...] = run_heads(qi - 1, 2, scores(0, qi - 1, 2), lambda h: lambda: bias_ref[h], True,
                               lambda: scores(0, 0, 2))

    far_bias = [rb_ref[far_bucket, h] * LOG2_E for h in range(n_heads)]
    n_far = jnp.maximum(qi - 1, 0)

    def far_pair(i, carry):
        s_ref[...] = run_heads(2 * i, 2, s_ref[...], lambda h: far_bias[h], False,
                               lambda: scores(0, 2 * i + 2, 2))
        return carry

    lax.fori_loop(0, n_far // 2, far_pair, 0)

    @pl.when(n_far % 2 == 1)
    def _():
        run_heads(n_far - 1, 1, s_ref[:t, :], lambda h: far_bias[h], False, None)

    lam = _lambda_value(lq1_ref, lk1_ref, lq2_ref, lk2_ref, lam_init)
    for h in range(n_heads):
        o = acc_ref[h, :LANES, :] / acc_ref[h, LANES:LANES + 1, :]
        od = o[:, :t] - lam * o[:, t:]
        od = od * lax.rsqrt(jnp.mean(od * od, axis=0, keepdims=True) + SUBLN_EPS)
        od = od * g_ref[...] * (1.0 - lam_init)
        o_ref[:, head(h)] = od.T.astype(o_ref.dtype)


def _diff_prompt_attention(qd16, kd16, vdt16, bias_tiles, rel_bias, lam_params, subln_g, *, batch, seq, lam_init):
    n, width = qd16.shape
    n_heads = width // LANES
    t = DIFF_TILE
    nq = seq // t
    n_buckets = rel_bias.shape[0]
    assert int(_rel_bucket_np(np.array([-(t + 1)]), n_buckets)[0]) == n_buckets - 1
    small = lambda b, i: (0, 0)
    return pl.pallas_call(
        functools.partial(_diff_prompt_kernel, t=t, far_bucket=n_buckets - 1, lam_init=lam_init),
        out_shape=jax.ShapeDtypeStruct((n, width), BF16),
        grid=(batch, nq),
        in_specs=[pl.BlockSpec(memory_space=pltpu.SMEM)]
                 + [pl.BlockSpec((1, HEAD_DIM), small)] * 4
                 + [pl.BlockSpec((LANES, 1), small),
                    pl.BlockSpec((t, width), lambda b, i: (b * nq + i, 0)),
                    pl.BlockSpec((seq, width), lambda b, i: (b, 0)),
                    pl.BlockSpec((width, seq), lambda b, i: (0, b)),
                    pl.BlockSpec((n_heads, 2 * t, 2 * t), lambda b, i: (0, 0, 0), pipeline_mode=pl.Buffered(1))],
        out_specs=pl.BlockSpec((t, width), lambda b, i: (b * nq + i, 0)),
        scratch_shapes=[pltpu.VMEM((n_heads, 1, 2 * t), F32),
                        pltpu.VMEM((n_heads, LANES + SUM_ROWS, 2 * t), F32), pltpu.VMEM((2 * t, 2 * t), F32)],
        compiler_params=_params(("parallel", "arbitrary")),
        name="diff_attention_prompt",
    )(rel_bias.astype(F32), *lam_params, subln_g.reshape(LANES, 1), qd16, kd16, vdt16, bias_tiles)


def _log_sigmoid_pair(z):
    soft = jnp.log(1.0 + jnp.exp(-jnp.abs(z)))
    return jnp.minimum(z, 0.0) - soft, jnp.minimum(-z, 0.0) - soft


def _split_hi_lo(x):
    hi = x.astype(BF16)
    lo = (x - hi.astype(F32)).astype(BF16)
    return hi, lo


def _sb_prompt_kernel(q_ref, k_ref, vt_ref, o_ref, used_ref, acc_ref, z_ref, *, t):
    qi = pl.program_id(1)
    n_pairs = q_ref.shape[1] // LANES
    pair = lambda p: slice(p * LANES, (p + 1) * LANES)
    qq = [_split_heads_rows(q_ref[:, pair(p)]) for p in range(n_pairs)]
    row = lax.broadcasted_iota(jnp.int32, (t, t), 0)
    colk = lax.broadcasted_iota(jnp.int32, (t, t), 1)
    later = (colk > row).astype(BF16)
    later2 = jnp.concatenate([later, later], axis=1)

    def logits(p, ki):
        k = k_ref[pl.ds(pl.multiple_of(ki * t, t), t), pair(p)]
        return lax.dot_general(k, qq[p], NT_DIMS, preferred_element_type=F32)

    def stage_a(z, mask):
        soft = jnp.maximum(z, 0.0) + jnp.log2(1.0 + jnp.exp2(-jnp.abs(z)))
        log_beta = z - soft
        if mask is not None:
            soft = jnp.where(mask, soft, 0.0)
        hi, lo = _split_hi_lo(soft)
        after = jnp.dot(later2, jnp.concatenate([hi, lo], axis=0), preferred_element_type=F32)
        return log_beta, after, after[0:1, :] + soft[0:1, :]

    def stage_b(p, ki, mask, log_beta, after, total):
        used = used_ref[p]
        a = jnp.exp2(log_beta - after)
        if mask is not None:
            a = jnp.where(mask, a, 0.0)
        vt = vt_ref[pair(p), pl.ds(pl.multiple_of(ki * t, t), t)]
        acc_ref[p] += jnp.dot(vt, a.astype(BF16), preferred_element_type=F32) * jnp.exp2(-used)
        used_ref[p] = used + total

    def run_steps(steps, z_first, prefetch_ki):
        z_next = z_first
        pending = None
        for i, (p, ki, mask) in enumerate(steps):
            z_cur = z_next
            z_next = logits(steps[i + 1][0], steps[i + 1][1]) if i + 1 < len(steps) else logits(0, prefetch_ki)
            staged = stage_a(z_cur, mask)
            if pending is not None:
                stage_b(*pending)
            pending = (p, ki, mask) + staged
        stage_b(*pending)
        return z_next

    used_ref[...] = jnp.zeros_like(used_ref)
    acc_ref[...] = jnp.zeros_like(acc_ref)
    key = lax.broadcasted_iota(jnp.int32, (t, 2 * t), 0)
    col = lax.broadcasted_iota(jnp.int32, (t, 2 * t), 1)
    causal = key < jnp.where(col >= t, col - t, col)
    diagonal = [(p, qi, causal) for p in range(n_pairs)]

    @pl.when(qi == 0)
    def _():
        z_ref[...] = run_steps(diagonal, logits(0, qi), 0)

    @pl.when(qi >= 1)
    def _():
        z_ref[...] = run_steps(diagonal + [(p, qi - 1, None) for p in range(n_pairs)], logits(0, qi),
                               jnp.maximum(qi - 2, 0))

    def cond(c):
        ki, least_used = c
        return jnp.logical_and(ki >= 0, least_used < -SB_SKIP_LOG * LOG2_E)

    def body(c):
        ki, _ = c
        z_ref[...] = run_steps([(p, ki, None) for p in range(n_pairs)], z_ref[...], jnp.maximum(ki - 1, 0))
        return ki - 1, jnp.min(used_ref[...])

    lax.while_loop(cond, body, (qi - 2, jnp.min(used_ref[...])))

    for p in range(n_pairs):
        o = jnp.concatenate([acc_ref[p, :HEAD_DIM, :t], acc_ref[p, HEAD_DIM:, t:]], axis=0)
        o_ref[:, pair(p)] = o.T.astype(o_ref.dtype)


def _sb_prompt_attention(qs16, ks16, vst16, *, batch, seq):
    n, width = qs16.shape
    n_pairs = width // LANES
    t = SB_TILE
    nq = seq // t
    return pl.pallas_call(
        functools.partial(_sb_prompt_kernel, t=t),
        out_shape=jax.ShapeDtypeStruct((n, width), BF16),
        grid=(batch, nq),
        in_specs=[pl.BlockSpec((t, width), lambda b, i: (b * nq + i, 0)),
                  pl.BlockSpec((seq, width), lambda b, i: (b, 0)),
                  pl.BlockSpec((width, seq), lambda b, i: (0, b))],
        out_specs=pl.BlockSpec((t, width), lambda b, i: (b * nq + i, 0)),
        scratch_shapes=[pltpu.VMEM((n_pairs, 1, 2 * t), F32), pltpu.VMEM((n_pairs, LANES, 2 * t), F32),
                        pltpu.VMEM((t, 2 * t), F32)],
        compiler_params=_params(("parallel", "arbitrary")),
        name="stick_breaking_attention_prompt",
    )(qs16, ks16, vst16)


def _block_rows(q, group_w):
    t, w = q.shape
    n_groups = w // group_w
    qt = jnp.concatenate([q] * n_groups, axis=0)
    row_g = lax.broadcasted_iota(jnp.int32, qt.shape, 0) // t
    col_g = lax.broadcasted_iota(jnp.int32, qt.shape, 1) // group_w
    return jnp.where(row_g == col_g, qt, 0.0).astype(BF16)


def _pad_rows(x, rows):
    return jnp.concatenate([x, jnp.zeros((rows - x.shape[0], x.shape[1]), x.dtype)], axis=0)


def _gather_head_lanes(acc, n_groups, t):
    w = acc.shape[1]
    col_g = lax.broadcasted_iota(jnp.int32, (t, w), 1) // (w // n_groups)
    out = jnp.zeros((t, w), acc.dtype)
    for g in range(n_groups):
        out = jnp.where(col_g == g, acc[g * t:(g + 1) * t, :], out)
    return out


def _diff_sample_kernel(pt_ref, lq1_ref, lk1_ref, lq2_ref, lk2_ref, g_ref, q_ref, kvn_ref, bias_ref, *rest,
                        n_pages_step, page, half_w, t_new, lam_init, row0, slot_rows):
    page_refs = rest[:n_pages_step]
    o_ref, qb_ref, m_ref, l_ref, acc_ref = rest[n_pages_step:]
    j = pl.program_id(1)
    n_steps = pl.num_programs(1)
    n_heads = half_w // LANES

    def accumulate(k, v, bias, mask):
        s = lax.dot_general(qb_ref[...], k, NT_DIMS, preferred_element_type=F32) + bias
        if mask is not None:
            s = jnp.where(mask, s, -jnp.inf)
        m_old = m_ref[...]
        m_new = jnp.maximum(m_old, jnp.max(s, axis=1, keepdims=True))
        alpha = jnp.exp(m_old - m_new)
        p = jnp.exp(s - m_new)
        l_ref[...] = alpha * l_ref[...] + jnp.sum(p, axis=1, keepdims=True)
        acc_ref[...] = alpha * acc_ref[...] + jnp.dot(p.astype(BF16), v, preferred_element_type=F32)
        m_ref[...] = m_new

    @pl.when(j == 0)
    def _():
        qb_ref[...] = _block_rows(q_ref[...], HEAD_DIM)
        rows = qb_ref.shape[0]
        m_ref[...] = jnp.full(m_ref.shape, -jnp.inf, F32)
        l_ref[...] = jnp.zeros_like(l_ref)
        acc_ref[...] = jnp.zeros_like(acc_ref)
        kvn = _pad_rows(kvn_ref[...], page)
        tq = lax.broadcasted_iota(jnp.int32, (rows, page), 0) % t_new
        tk = lax.broadcasted_iota(jnp.int32, (rows, page), 1)
        accumulate(kvn[:, :half_w].astype(BF16), kvn[:, half_w:].astype(BF16),
                   bias_ref[:, 2 * page:3 * page], tk <= tq)

    def heads(ref, first):
        parts = [ref[0, pl.ds(row0 + first + h, page, stride=slot_rows), :] for h in range(n_heads)]
        return jnp.concatenate(parts, axis=1).astype(BF16)

    far = bias_ref[:, 0:page]
    last = jnp.where(j == n_steps - 1, bias_ref[:, page:2 * page], far)
    accumulate(jnp.concatenate([heads(r, 0) for r in page_refs], axis=0),
               jnp.concatenate([heads(r, n_heads) for r in page_refs], axis=0),
               jnp.concatenate([far] * (n_pages_step - 1) + [last], axis=1), None)

    @pl.when(j == n_steps - 1)
    def _():
        o = acc_ref[...] / l_ref[...]
        o = _gather_head_lanes(o, n_heads, 2 * t_new)
        lam = _lambda_value(lq1_ref, lk1_ref, lq2_ref, lk2_ref, lam_init)
        od = o[:t_new] - lam * o[t_new:]
        outs = []
        for h in range(n_heads):
            oh = od[:, h * LANES:(h + 1) * LANES]
            oh = oh * lax.rsqrt(jnp.mean(oh * oh, axis=-1, keepdims=True) + SUBLN_EPS)
            outs.append(oh * g_ref[...] * (1.0 - lam_init))
        o_ref[...] = jnp.concatenate(outs, axis=1).astype(o_ref.dtype)


def _diff_sample_attention(q32, kvn32, cache_rows, layer, depth, page_table, bias_tab, lam_params, subln_g, *,
                           t_new, lam_init):
    n, half_w = q32.shape
    n_seq, n_pages = page_table.shape
    n_heads = half_w // LANES
    slot_rows = depth * 2 * n_heads
    page = cache_rows.shape[1] // slot_rows
    gp = PAGES_PER_STEP
    rows = n_heads * 2 * t_new
    small = lambda b, j, pt: (0, 0)
    seq_row = lambda b, j, pt: (b, 0)

    def page_spec(g):
        return pl.BlockSpec((1, page * slot_rows, LANES), lambda b, j, pt: (pt[b, j * gp + g], 0, 0))

    return pl.pallas_call(
        functools.partial(_diff_sample_kernel, n_pages_step=gp, page=page, half_w=half_w, t_new=t_new,
                          lam_init=lam_init, row0=layer * 2 * n_heads, slot_rows=slot_rows),
        out_shape=jax.ShapeDtypeStruct((n, half_w), F32),
        grid_spec=pltpu.PrefetchScalarGridSpec(
            num_scalar_prefetch=1,
            grid=(n_seq, n_pages // gp),
            in_specs=[pl.BlockSpec((1, HEAD_DIM), small)] * 4
                     + [pl.BlockSpec((1, LANES), small),
                        pl.BlockSpec((t_new, half_w), seq_row),
                        pl.BlockSpec((t_new, 2 * half_w), seq_row),
                        pl.BlockSpec((rows, 3 * page), small)]
                     + [page_spec(g) for g in range(gp)],
            out_specs=pl.BlockSpec((t_new, half_w), seq_row),
            scratch_shapes=[pltpu.VMEM((rows, half_w), BF16), pltpu.VMEM((rows, 1), F32),
                            pltpu.VMEM((rows, 1), F32), pltpu.VMEM((rows, half_w), F32)]),
        compiler_params=_params(("parallel", "arbitrary")),
        name="diff_attention_sample",
    )(page_table, *lam_params, subln_g.reshape(1, LANES), q32, kvn32, bias_tab, *([cache_rows] * gp))


def _sb_sample_kernel(pt_ref, q_ref, kvn_ref, cache_ref, o_ref, qb_ref, run_ref, acc_ref, buf_ref, sem_ref, *,
                      n_pages, page, half_w, t_new, row0):
    b = pl.program_id(0)

    def page_copy(j, slot):
        return pltpu.make_async_copy(cache_ref.at[pt_ref[b, j], pl.ds(row0, 2 * half_w), :],
                                     buf_ref.at[slot], sem_ref.at[slot])

    slot_of = lambda j: (n_pages - 1 - j) & 1
    page_copy(n_pages - 1, 0).start()
    row = lax.broadcasted_iota(jnp.int32, (page, page), 0)
    colk = lax.broadcasted_iota(jnp.int32, (page, page), 1)
    later = (row > colk).astype(BF16)
    later2 = jnp.concatenate([later, later], axis=0)

    def accumulate(z, pv, mask):
        log_beta, log_rest = _log_sigmoid_pair(z)
        if mask is not None:
            log_rest = jnp.where(mask, log_rest, 0.0)
        hi, lo = _split_hi_lo(log_rest)
        suffix = jnp.dot(jnp.concatenate([hi, lo], axis=1), later2, preferred_element_type=F32)
        a = jnp.exp(log_beta + suffix + run_ref[...])
        if mask is not None:
            a = jnp.where(mask, a, 0.0)
        acc_ref[...] += pv(a.astype(BF16))
        run_ref[...] += jnp.sum(log_rest, axis=1, keepdims=True)

    qb_ref[...] = _block_rows(q_ref[...], HEAD_DIM)
    rows = qb_ref.shape[0]
    run_ref[...] = jnp.zeros_like(run_ref)
    acc_ref[...] = jnp.zeros_like(acc_ref)
    kvn = _pad_rows(kvn_ref[...], page)
    k_new = kvn[:, :half_w].astype(BF16)
    v_new = kvn[:, half_w:].astype(BF16)
    tq = lax.broadcasted_iota(jnp.int32, (rows, page), 0) % t_new
    tk = lax.broadcasted_iota(jnp.int32, (rows, page), 1)
    accumulate(lax.dot_general(qb_ref[...], k_new, NT_DIMS, preferred_element_type=F32),
               lambda a: jnp.dot(a, v_new, preferred_element_type=F32), tk < tq)

    def cond(c):
        j, live = c
        return jnp.logical_and(j >= 0, live > SB_SKIP_LOG)

    def body(c):
        j, _ = c
        slot = slot_of(j)
        page_copy(j, slot).wait()

        @pl.when(j >= 1)
        def _():
            page_copy(j - 1, 1 - slot).start()

        kt = buf_ref[slot, :half_w, :].astype(BF16)
        vt = buf_ref[slot, half_w:, :].astype(BF16)
        accumulate(jnp.dot(qb_ref[...], kt, preferred_element_type=F32),
                   lambda a: lax.dot_general(a, vt, NT_DIMS, preferred_element_type=F32), None)
        return j - 1, jnp.max(run_ref[...])

    j_end, _ = lax.while_loop(cond, body, (n_pages - 1, jnp.max(run_ref[...])))

    @pl.when(j_end >= 0)
    def _():
        page_copy(j_end, slot_of(j_end)).wait()

    n_heads = half_w // HEAD_DIM
    o_ref[...] = _gather_head_lanes(acc_ref[...], n_heads, t_new).astype(o_ref.dtype)


def _sb_sample_attention(q32, kvn32, cache_t, layer, page_table, *, t_new):
    n, half_w = q32.shape
    n_seq, n_pages = page_table.shape
    page = cache_t.shape[2]
    rows = (half_w // HEAD_DIM) * t_new
    seq_row = lambda b, pt: (b, 0)
    return pl.pallas_call(
        functools.partial(_sb_sample_kernel, n_pages=n_pages, page=page, half_w=half_w, t_new=t_new,
                          row0=layer * 2 * half_w),
        out_shape=jax.ShapeDtypeStruct((n, half_w), F32),
        grid_spec=pltpu.PrefetchScalarGridSpec(
            num_scalar_prefetch=1,
            grid=(n_seq,),
            in_specs=[pl.BlockSpec((t_new, half_w), seq_row),
                      pl.BlockSpec((t_new, 2 * half_w), seq_row),
                      pl.BlockSpec(memory_space=pl.ANY)],
            out_specs=pl.BlockSpec((t_new, half_w), seq_row),
            scratch_shapes=[pltpu.VMEM((rows, half_w), BF16), pltpu.VMEM((rows, 1), F32),
                            pltpu.VMEM((rows, half_w), F32), pltpu.VMEM((2, 2 * half_w, page), F32),
                            pltpu.SemaphoreType.DMA((2,))]),
        compiler_params=_params(("arbitrary",)),
        name="stick_breaking_attention_sample",
    )(page_table, q32, kvn32, cache_t)


def _router_gates(logits, n_groups, n_experts):
    lane = lax.broadcasted_iota(jnp.int32, logits.shape, 1).astype(F32)
    neg = -jnp.inf
    big = float(LANES)
    is_group = lane < n_groups
    gl = jnp.where(is_group, logits, neg)
    gmax = jnp.max(gl, axis=1, keepdims=True)
    pg_sel = 1.0 / jnp.sum(jnp.where(is_group, jnp.exp(gl - gmax), 0.0), axis=1, keepdims=True)
    g = jnp.min(jnp.where(gl == gmax, lane, big), axis=1, keepdims=True)
    lo = n_groups + g * n_experts
    in_group = jnp.logical_and(lane >= lo, lane < lo + n_experts)
    el = jnp.where(in_group, logits, neg)
    v1 = jnp.max(el, axis=1, keepdims=True)
    i1 = jnp.min(jnp.where(el == v1, lane, big), axis=1, keepdims=True)
    el2 = jnp.where(lane == i1, neg, el)
    v2 = jnp.max(el2, axis=1, keepdims=True)
    i2 = jnp.min(jnp.where(el2 == v2, lane, big), axis=1, keepdims=True)
    e2 = jnp.exp(v2 - v1)
    w1 = 1.0 / (1.0 + e2)
    w2 = e2 * w1
    return pg_sel * (jnp.where(lane == i1, w1, 0.0) + jnp.where(lane == i2, w2, 0.0))


def _post_kernel(x_ref, od_ref, os_ref, wo_ref, gf_ref, wr_ref, br_ref, wg_ref, wu_ref, wd_ref, gfin_ref,
                 y_ref, act_ref, *, n_groups, n_experts, final_norm):
    half = od_ref.shape[1]
    attn = (jnp.dot(od_ref[...], wo_ref[:half, :], preferred_element_type=F32)
            + jnp.dot(os_ref[...], wo_ref[half:, :], preferred_element_type=F32))
    x2 = x_ref[...] + attn
    hf = x2 * lax.rsqrt(jnp.mean(x2 * x2, axis=-1, keepdims=True) + NORM_EPS) * gf_ref[...]
    h, h_lo = _split_hi_lo(hf)
    both = jnp.dot(h, wr_ref[...], preferred_element_type=F32)
    logits = (both[:, :LANES] + both[:, LANES:]
              + jnp.dot(h_lo, wr_ref[:, :LANES], preferred_element_type=F32) + br_ref[...])
    gates = _router_gates(logits, n_groups, n_experts)

    n_total, _, d_exp = wg_ref.shape
    hidden = lambda e: (jnp.dot(h, wg_ref[e], preferred_element_type=F32),
                        jnp.dot(h, wu_ref[e], preferred_element_type=F32))
    nxt = hidden(0)
    for e in range(n_total):
        hg, hu = nxt
        if e + 1 < n_total:
            nxt = hidden(e + 1)
        gate = gates[:, n_groups + e:n_groups + e + 1]
        act_ref[:, e * d_exp:(e + 1) * d_exp] = (hg * (1.0 / (1.0 + jnp.exp(-hg))) * hu * gate).astype(BF16)
    x3 = x2 + jnp.dot(act_ref[...], wd_ref[...], preferred_element_type=F32)
    if final_norm:
        x3 = x3 * lax.rsqrt(jnp.mean(x3 * x3, axis=-1, keepdims=True) + NORM_EPS) * gfin_ref[...]
    y_ref[...] = x3


def _post_attention(x2d, od16, os16, wo16, g_ffn, wr16, b_router, wg16, wu16, wd16, g_final, *,
                    tm, n_groups, n_experts, final_norm):
    n, d = x2d.shape
    half = od16.shape[1]
    n_total, _, d_exp = wg16.shape
    row = lambda i: (i, 0)
    once = dict(pipeline_mode=pl.Buffered(1))
    const2 = lambda i: (0, 0)
    const3 = lambda i: (0, 0, 0)
    return pl.pallas_call(
        functools.partial(_post_kernel, n_groups=n_groups, n_experts=n_experts, final_norm=final_norm),
        out_shape=jax.ShapeDtypeStruct((n, d), F32),
        grid=(n // tm,),
        in_specs=[pl.BlockSpec((tm, d), row), pl.BlockSpec((tm, half), row), pl.BlockSpec((tm, half), row),
                  pl.BlockSpec((d, d), const2, **once), pl.BlockSpec((1, d), const2),
                  pl.BlockSpec((d, 2 * LANES), const2, **once), pl.BlockSpec((1, LANES), const2),
                  pl.BlockSpec((n_total, d, d_exp), const3, **once),
                  pl.BlockSpec((n_total, d, d_exp), const3, **once),
                  pl.BlockSpec((n_total * d_exp, d), const2, **once),
                  pl.BlockSpec((1, d), const2)],
        out_specs=pl.BlockSpec((tm, d), row),
        scratch_shapes=[pltpu.VMEM((tm, n_total * d_exp), BF16)],
        compiler_params=_params(("parallel",)),
        name="out_proj_hmoe",
    )(x2d, od16, os16, wo16, g_ffn.reshape(1, d), wr16, b_router, wg16, wu16, wd16, g_final.reshape(1, d))


def _row_tile(n, pref):
    tm = min(n, pref)
    while n % tm:
        tm //= 2
    return tm


def kernel(x_prompt, x_sample, cache_kv_diff, cache_kv_sb, page_table, rel_bias, g_mix, w_in, lambda_q1,
           lambda_k1, lambda_q2, lambda_k2, subln_g, w_out, g_ffn, w_group, b_group, w_erouter, b_erouter,
           w_gate, w_up, w_down, g_final):
    batch, seq, d = x_prompt.shape
    n_seq, t_new, _ = x_sample.shape
    n_pool, page, depth, _, h_diff, diff_hw = cache_kv_diff.shape
    _, _, _, _, h_sb, sb_hw = cache_kv_sb.shape
    n_pages = page_table.shape[1]
    past_len = n_pages * page
    n_groups, n_experts = w_erouter.shape[2], w_erouter.shape[3]
    half_w = h_diff * diff_hw
    assert diff_hw == 2 * HEAD_DIM == LANES and sb_hw == HEAD_DIM and h_sb * sb_hw == half_w
    assert seq % DIFF_TILE == 0 and seq % SB_TILE == 0 and n_pages % PAGES_PER_STEP == 0 and page == LANES
    assert n_groups + n_groups * n_experts <= LANES

    xp = x_prompt.reshape(batch * seq, d)
    xs = x_sample.reshape(n_seq * t_new, d)
    cache_d = cache_kv_diff.reshape(n_pool, page * depth * 2 * h_diff, diff_hw)
    cache_s = jnp.transpose(cache_kv_sb, (0, 2, 3, 4, 5, 1)).reshape(n_pool, depth * 2 * half_w, page)
    bias_tiles = _prompt_bias_tiles(rel_bias, DIFF_TILE)
    bias_tab = _sample_bias_table(rel_bias, past_len, page, t_new)
    tm_p = _row_tile(seq, 1024)
    tm_s = _row_tile(n_seq * t_new, 512)
    tm_moe_p = _row_tile(batch * seq, 512)

    kvd_p, kvs_p, kvd_s, kvs_s = [], [], [], []
    for l in range(depth):
        li = _lambda_init(l)
        lam_params = [a[l].reshape(1, HEAD_DIM).astype(F32) for a in (lambda_q1, lambda_k1, lambda_q2, lambda_k2)]
        w16 = w_in[l].astype(BF16)
        wo16 = w_out[l].astype(BF16)
        wr = jnp.concatenate([w_group[l].astype(F32), w_erouter[l].astype(F32).reshape(d, n_groups * n_experts)],
                             axis=1)
        wr = jnp.pad(wr, ((0, 0), (0, LANES - wr.shape[1])))
        wr_hi = wr.astype(BF16)
        wr16 = jnp.concatenate([wr_hi, (wr - wr_hi.astype(F32)).astype(BF16)], axis=1)
        br = jnp.concatenate([b_group[l].astype(F32), b_erouter[l].astype(F32).reshape(-1)])
        br = jnp.pad(br, (0, LANES - br.shape[0])).reshape(1, LANES)
        d_exp = w_gate.shape[-1]
        wg16 = w_gate[l].astype(BF16).reshape(n_groups * n_experts, d, d_exp)
        wu16 = w_up[l].astype(BF16).reshape(n_groups * n_experts, d, d_exp)
        wd16 = w_down[l].astype(BF16).reshape(n_groups * n_experts * d_exp, d)
        last = l == depth - 1

        qd16, qs16, kvd, kvst, kd16, vdt16, ks16, vst16 = _project_prompt(xp, g_mix[l], w16, tm=tm_p, batch=batch,
                                                                         seq=seq)
        od16 = _diff_prompt_attention(qd16, kd16, vdt16, bias_tiles, rel_bias, lam_params, subln_g[l],
                                      batch=batch, seq=seq, lam_init=li)
        os16 = _sb_prompt_attention(qs16, ks16, vst16, batch=batch, seq=seq)
        xp = _post_attention(xp, od16, os16, wo16, g_ffn[l], wr16, br, wg16, wu16, wd16, g_final,
                             tm=tm_moe_p, n_groups=n_groups, n_experts=n_experts, final_norm=last)
        kvd_p.append(kvd.reshape(batch, seq, 2, h_diff, diff_hw))
        kvs_p.append(jnp.transpose(kvst.reshape(batch, 2, h_sb, sb_hw, seq), (0, 4, 1, 2, 3)))

        qd32, qs32, kvd, kvs = _project_sample(xs, g_mix[l], w16, tm=tm_s)
        od = _diff_sample_attention(qd32, kvd, cache_d, l, depth, page_table, bias_tab, lam_params, subln_g[l],
                                    t_new=t_new, lam_init=li)
        osb = _sb_sample_attention(qs32, kvs, cache_s, l, page_table, t_new=t_new)
        xs = _post_attention(xs, od.astype(BF16), osb.astype(BF16), wo16, g_ffn[l], wr16, br, wg16, wu16,
                             wd16, g_final, tm=tm_s, n_groups=n_groups, n_experts=n_experts, final_norm=last)
        kvd_s.append(kvd.reshape(n_seq, t_new, 2, h_diff, diff_hw))
        kvs_s.append(kvs.reshape(n_seq, t_new, 2, h_sb, sb_hw))

    y_prompt = xp.reshape(batch, seq, d)
    y_sample = xs.reshape(n_seq, t_new, d)
    return (y_prompt, y_sample, jnp.stack(kvd_p, axis=2), jnp.stack(kvs_p, axis=2),
            jnp.stack(kvd_s, axis=2), jnp.stack(kvs_s, axis=2))
```

```python
import functools
import math

import numpy as np
import jax
import jax.numpy as jnp
from jax import lax
from jax.experimental import pallas as pl
from jax.experimental.pallas import tpu as pltpu

HEAD_DIM = 64
MAX_DISTANCE = 128
NORM_EPS = 1e-6
SUBLN_EPS = 1e-5
TOP_K_INNER = 2

LANES = 128
VMEM_LIMIT_BYTES = 56 * 1024 * 1024

DIFF_TILE = 256
SB_TILE = 256
SUM_ROWS = 16
SB_SKIP_LOG = -90.0
PAGES_PER_STEP = 32

F32 = jnp.float32
BF16 = jnp.bfloat16
NT_DIMS = (((1,), (1,)), ((), ()))
LOG2_E = math.log2(math.e)


def _lambda_init(layer):
    return 0.8 - 0.6 * math.exp(-0.3 * layer)


def _rel_bucket_np(rel, n_buckets):
    n = np.maximum(-rel, 0)
    max_exact = n_buckets // 2
    nf = np.maximum(n, 1).astype(np.float32)
    large = max_exact + (np.log(nf / np.float32(max_exact)) / np.float32(math.log(MAX_DISTANCE / max_exact))
                         * np.float32(n_buckets - max_exact)).astype(np.int32)
    large = np.minimum(large, n_buckets - 1)
    return np.where(n < max_exact, n, large).astype(np.int32)


def _params(sem):
    return pltpu.CompilerParams(dimension_semantics=sem, vmem_limit_bytes=VMEM_LIMIT_BYTES)


def _normed_proj_cols(x_ref, g_ref, w_ref, half_w):
    x = x_ref[...]
    ms = jnp.mean(x * x, axis=-1, keepdims=True)
    h = (x * lax.rsqrt(ms + NORM_EPS) * g_ref[...]).astype(BF16)
    return lambda i: jnp.dot(h, w_ref[:, i * half_w:(i + 1) * half_w], preferred_element_type=F32)


def _proj_prompt_kernel(x_ref, g_ref, w_ref, qd_ref, qs_ref, kvd_ref, kvst_ref, kd16_ref, vdt16_ref, ks16_ref,
                        vst16_ref, *, half_w, scale):
    cols = _normed_proj_cols(x_ref, g_ref, w_ref, half_w)
    tm = x_ref.shape[0]
    n_rows = 2 * half_w // LANES

    def store_diff_rows(first, val):
        for r in range(n_rows // 2):
            kvd_ref[pl.ds(first + r, tm, stride=n_rows), :] = val[:, r * LANES:(r + 1) * LANES]

    qd = cols(0)
    kd = cols(1)
    qd_ref[...] = (qd * (scale * LOG2_E)).astype(BF16)
    vd = cols(2)
    store_diff_rows(0, kd)
    kd16_ref[...] = kd.astype(BF16)
    qs = cols(3)
    store_diff_rows(n_rows // 2, vd)
    vdt16_ref[...] = vd.T.astype(BF16)
    ks = cols(4)
    qs_ref[...] = (qs * (scale * LOG2_E)).astype(BF16)
    vs = cols(5)
    ks16_ref[...] = ks.astype(BF16)
    kvst_ref[0, :half_w, :] = ks.T
    vst = vs.T
    kvst_ref[0, half_w:, :] = vst
    vst16_ref[...] = vst.astype(BF16)


def _project_prompt(x2d, g, w16, *, tm, batch, seq):
    n, d = x2d.shape
    half_w = w16.shape[1] // 6
    n_rows = 2 * half_w // LANES
    nbs = seq // tm
    row = lambda i: (i, 0)
    const = lambda i: (0, 0)
    colblk = lambda i: (0, i)
    out_shape = (jax.ShapeDtypeStruct((n, half_w), BF16), jax.ShapeDtypeStruct((n, half_w), BF16),
                 jax.ShapeDtypeStruct((n * n_rows, LANES), F32),
                 jax.ShapeDtypeStruct((batch, 2 * half_w, seq), F32),
                 jax.ShapeDtypeStruct((n, half_w), BF16), jax.ShapeDtypeStruct((half_w, n), BF16),
                 jax.ShapeDtypeStruct((n, half_w), BF16), jax.ShapeDtypeStruct((half_w, n), BF16))
    out_specs = (pl.BlockSpec((tm, half_w), row), pl.BlockSpec((tm, half_w), row),
                 pl.BlockSpec((tm * n_rows, LANES), row),
                 pl.BlockSpec((1, 2 * half_w, tm), lambda i: (i // nbs, 0, i % nbs)),
                 pl.BlockSpec((tm, half_w), row), pl.BlockSpec((half_w, tm), colblk),
                 pl.BlockSpec((tm, half_w), row), pl.BlockSpec((half_w, tm), colblk))
    return pl.pallas_call(
        functools.partial(_proj_prompt_kernel, half_w=half_w, scale=HEAD_DIM ** -0.5),
        out_shape=out_shape,
        grid=(n // tm,),
        in_specs=[pl.BlockSpec((tm, d), row), pl.BlockSpec((1, d), const),
                  pl.BlockSpec((d, 6 * half_w), const)],
        out_specs=out_specs,
        compiler_params=_params(("parallel",)),
        name="rmsnorm_qkv_proj_prompt",
    )(x2d, g.reshape(1, d), w16)


def _proj_sample_kernel(x_ref, g_ref, w_ref, qd_ref, qs_ref, kvd_ref, kvs_ref, *, half_w, scale):
    cols = _normed_proj_cols(x_ref, g_ref, w_ref, half_w)
    qd_ref[...] = cols(0) * scale
    qs_ref[...] = cols(3) * scale
    kvd_ref[:, :half_w] = cols(1)
    kvd_ref[:, half_w:] = cols(2)
    kvs_ref[:, :half_w] = cols(4)
    kvs_ref[:, half_w:] = cols(5)


def _project_sample(x2d, g, w16, *, tm):
    n, d = x2d.shape
    half_w = w16.shape[1] // 6
    row = lambda i: (i, 0)
    const = lambda i: (0, 0)
    return pl.pallas_call(
        functools.partial(_proj_sample_kernel, half_w=half_w, scale=HEAD_DIM ** -0.5),
        out_shape=(jax.ShapeDtypeStruct((n, half_w), F32), jax.ShapeDtypeStruct((n, half_w), F32),
                   jax.ShapeDtypeStruct((n, 2 * half_w), F32), jax.ShapeDtypeStruct((n, 2 * half_w), F32)),
        grid=(n // tm,),
        in_specs=[pl.BlockSpec((tm, d), row), pl.BlockSpec((1, d), const),
                  pl.BlockSpec((d, 6 * half_w), const)],
        out_specs=(pl.BlockSpec((tm, half_w), row), pl.BlockSpec((tm, half_w), row),
                   pl.BlockSpec((tm, 2 * half_w), row), pl.BlockSpec((tm, 2 * half_w), row)),
        compiler_params=_params(("parallel",)),
        name="rmsnorm_qkv_proj_sample",
    )(x2d, g.reshape(1, d), w16)


def _bias_select(bucket, rb_ref, h, n_buckets):
    out = jnp.zeros(bucket.shape, F32)
    for b in range(n_buckets):
        out = jnp.where(bucket == b, rb_ref[b, h], out)
    return out


def _prompt_bias_kernel(rb_ref, bucket_ref, out_ref, *, n_buckets):
    h = pl.program_id(0)
    t = bucket_ref.shape[1]
    for d in range(bucket_ref.shape[0]):
        tile = _bias_select(bucket_ref[d], rb_ref, h, n_buckets) * LOG2_E
        if d == 0:
            key = lax.broadcasted_iota(jnp.int32, tile.shape, 0)
            qry = lax.broadcasted_iota(jnp.int32, tile.shape, 1)
            tile = jnp.where(key <= qry, tile, -jnp.inf)
        out_ref[0, (1 - d) * t:(2 - d) * t, :] = jnp.concatenate([tile, tile], axis=1)


def _prompt_bias_tiles(rel_bias, t):
    n_buckets, n_heads = rel_bias.shape
    key = np.arange(t)[:, None]
    qry = np.arange(t)[None, :]
    bucket = np.stack([_rel_bucket_np(key - qry - d * t, n_buckets) for d in range(2)])
    return pl.pallas_call(
        functools.partial(_prompt_bias_kernel, n_buckets=n_buckets),
        out_shape=jax.ShapeDtypeStruct((n_heads, 2 * t, 2 * t), F32),
        grid=(n_heads,),
        in_specs=[pl.BlockSpec(memory_space=pltpu.SMEM),
                  pl.BlockSpec((2, t, t), lambda h: (0, 0, 0))],
        out_specs=pl.BlockSpec((1, 2 * t, 2 * t), lambda h: (h, 0, 0)),
        compiler_params=_params(("parallel",)),
        name="prompt_rel_bias_tiles",
    )(rel_bias.astype(F32), jnp.asarray(bucket))


def _sample_bias_kernel(rb_ref, bucket_ref, out_ref, *, n_buckets, n_heads):
    rows = bucket_ref.shape[0] // n_heads
    for h in range(n_heads):
        sl = slice(h * rows, (h + 1) * rows)
        out_ref[sl, :] = _bias_select(bucket_ref[sl, :], rb_ref, h, n_buckets)


def _sample_bias_table(rel_bias, past_len, page, t_new):
    n_buckets, n_heads = rel_bias.shape
    rows = n_heads * 2 * t_new
    q_pos = past_len + (np.arange(rows) % t_new)[:, None]
    far = np.full((rows, page), n_buckets - 1, np.int32)
    last = _rel_bucket_np((past_len - page + np.arange(page))[None, :] - q_pos, n_buckets)
    new = _rel_bucket_np((past_len + np.arange(page))[None, :] - q_pos, n_buckets)
    bucket = np.concatenate([far, last, new], axis=1)
    return pl.pallas_call(
        functools.partial(_sample_bias_kernel, n_buckets=n_buckets, n_heads=n_heads),
        out_shape=jax.ShapeDtypeStruct(bucket.shape, F32),
        in_specs=[pl.BlockSpec(memory_space=pltpu.SMEM), pl.BlockSpec(memory_space=pltpu.VMEM)],
        out_specs=pl.BlockSpec(memory_space=pltpu.VMEM),
        name="sample_rel_bias_table",
    )(rel_bias.astype(F32), jnp.asarray(bucket))


def _split_heads_rows(q):
    lane = lax.broadcasted_iota(jnp.int32, q.shape, 1)
    zero = jnp.zeros_like(q)
    return jnp.concatenate([jnp.where(lane < HEAD_DIM, q, zero), jnp.where(lane >= HEAD_DIM, q, zero)], axis=0)


def _lambda_value(lq1_ref, lk1_ref, lq2_ref, lk2_ref, lam_init):
    a = jnp.sum(lq1_ref[...] * lk1_ref[...], axis=-1, keepdims=True)
    b = jnp.sum(lq2_ref[...] * lk2_ref[...], axis=-1, keepdims=True)
    return jnp.exp(a) - jnp.exp(b) + lam_init


def _diff_prompt_kernel(rb_ref, lq1_ref, lk1_ref, lq2_ref, lk2_ref, g_ref, q_ref, k_ref, vt_ref, bias_ref,
                        o_ref, m_ref, acc_ref, s_ref, *, t, far_bucket, lam_init):
    qi = pl.program_id(1)
    n_heads = q_ref.shape[1] // LANES
    head = lambda h: slice(h * LANES, (h + 1) * LANES)
    qq = [_split_heads_rows(q_ref[:, head(h)]) for h in range(n_heads)]
    keys = lambda ki, nb: pl.ds(pl.multiple_of(ki * t, t), nb * t)

    def scores(h, ki, nb):
        return lax.dot_general(k_ref[keys(ki, nb), head(h)], qq[h], NT_DIMS,
                               preferred_element_type=F32)

    def accumulate(h, ki, nb, s, bias, first=False):
        if callable(bias):
            m_blk = jnp.max(s + bias(), axis=0, keepdims=True)
        else:
            m_blk = jnp.max(s, axis=0, keepdims=True) + bias
        if first:
            m_new = m_blk
        else:
            m_old = m_ref[h]
            m_new = jnp.maximum(m_old, m_blk)
        p = jnp.exp2(s - (m_new - (bias() if callable(bias) else bias)))
        vt1 = jnp.concatenate([vt_ref[head(h), keys(ki, nb)], jnp.ones((SUM_ROWS, nb * t), BF16)], axis=0)
        pv = jnp.dot(vt1, p.astype(BF16), preferred_element_type=F32)
        acc_ref[h] = pv if first else jnp.exp2(m_old - m_new) * acc_ref[h] + pv
        m_ref[h] = m_new

    def run_heads(ki, nb, s_first, bias, first, prefetch):
        s_next = s_first
        for h in range(n_heads):
            s_cur = s_next
            s_next = scores(h + 1, ki, nb) if h + 1 < n_heads else (prefetch() if prefetch else None)
            accumulate(h, ki, nb, s_cur, bias(h), first)
        return s_next

    @pl.when(qi == 0)
    def _():
        run_heads(0, 1, scores(0, 0, 1), lambda h: lambda: bias_ref[h, t:, :], True, None)

    @pl.when(qi >= 1)
    def _():
        s_ref[...] = run_heads(qi - 1, 2, scores(0, qi - 1, 2), lambda h: lambda: bias_ref[h], True,
                               lambda: scores(0, 0, 2))

    far_bias = [rb_ref[far_bucket, h] * LOG2_E for h in range(n_heads)]
    n_far = jnp.maximum(qi - 1, 0)

    def far_pair(i, carry):
        s_ref[...] = run_heads(2 * i, 2, s_ref[...], lambda h: far_bias[h], False,
                               lambda: scores(0, 2 * i + 2, 2))
        return carry

    lax.fori_loop(0, n_far // 2, far_pair, 0)

    @pl.when(n_far % 2 == 1)
    def _():
        run_heads(n_far - 1, 1, s_ref[:t, :], lambda h: far_bias[h], False, None)

    lam = _lambda_value(lq1_ref, lk1_ref, lq2_ref, lk2_ref, lam_init)
    for h in range(n_heads):
        o = acc_ref[h, :LANES, :] / acc_ref[h, LANES:LANES + 1, :]
        od = o[:, :t] - lam * o[:, t:]
        od = od * lax.rsqrt(jnp.mean(od * od, axis=0, keepdims=True) + SUBLN_EPS)
        od = od * g_ref[...] * (1.0 - lam_init)
        o_ref[:, head(h)] = od.T.astype(o_ref.dtype)


def _diff_prompt_attention(qd16, kd16, vdt16, bias_tiles, rel_bias, lam_params, subln_g, *, batch, seq, lam_init):
    n, width = qd16.shape
    n_heads = width // LANES
    t = DIFF_TILE
    nq = seq // t
    n_buckets = rel_bias.shape[0]
    assert int(_rel_bucket_np(np.array([-(t + 1)]), n_buckets)[0]) == n_buckets - 1
    small = lambda b, i: (0, 0)
    return pl.pallas_call(
        functools.partial(_diff_prompt_kernel, t=t, far_bucket=n_buckets - 1, lam_init=lam_init),
        out_shape=jax.ShapeDtypeStruct((n, width), BF16),
        grid=(batch, nq),
        in_specs=[pl.BlockSpec(memory_space=pltpu.SMEM)]
                 + [pl.BlockSpec((1, HEAD_DIM), small)] * 4
                 + [pl.BlockSpec((LANES, 1), small),
                    pl.BlockSpec((t, width), lambda b, i: (b * nq + i, 0)),
                    pl.BlockSpec((seq, width), lambda b, i: (b, 0)),
                    pl.BlockSpec((width, seq), lambda b, i: (0, b)),
                    pl.BlockSpec((n_heads, 2 * t, 2 * t), lambda b, i: (0, 0, 0), pipeline_mode=pl.Buffered(1))],
        out_specs=pl.BlockSpec((t, width), lambda b, i: (b * nq + i, 0)),
        scratch_shapes=[pltpu.VMEM((n_heads, 1, 2 * t), F32),
                        pltpu.VMEM((n_heads, LANES + SUM_ROWS, 2 * t), F32), pltpu.VMEM((2 * t, 2 * t), F32)],
        compiler_params=_params(("parallel", "arbitrary")),
        name="diff_attention_prompt",
    )(rel_bias.astype(F32), *lam_params, subln_g.reshape(LANES, 1), qd16, kd16, vdt16, bias_tiles)


def _log_sigmoid_pair(z):
    soft = jnp.log(1.0 + jnp.exp(-jnp.abs(z)))
    return jnp.minimum(z, 0.0) - soft, jnp.minimum(-z, 0.0) - soft


def _split_hi_lo(x):
    hi = x.astype(BF16)
    lo = (x - hi.astype(F32)).astype(BF16)
    return hi, lo


def _sb_prompt_kernel(q_ref, k_ref, vt_ref, o_ref, used_ref, acc_ref, *, t):
    qi = pl.program_id(1)
    n_pairs = q_ref.shape[1] // LANES
    pair = lambda p: slice(p * LANES, (p + 1) * LANES)
    qq = [_split_heads_rows(q_ref[:, pair(p)]) for p in range(n_pairs)]
    row = lax.broadcasted_iota(jnp.int32, (t, t), 0)
    colk = lax.broadcasted_iota(jnp.int32, (t, t), 1)
    later = (colk > row).astype(BF16)
    later2 = jnp.concatenate([later, later], axis=1)

    def logits(p, ki):
        k = k_ref[pl.ds(pl.multiple_of(ki * t, t), t), pair(p)]
        return lax.dot_general(k, qq[p], NT_DIMS, preferred_element_type=F32)

    def stage_a(z, mask):
        soft = jnp.maximum(z, 0.0) + jnp.log2(1.0 + jnp.exp2(-jnp.abs(z)))
        log_beta = z - soft
        if mask is not None:
            soft = jnp.where(mask, soft, 0.0)
        hi, lo = _split_hi_lo(soft)
        after = jnp.dot(later2, jnp.concatenate([hi, lo], axis=0), preferred_element_type=F32)
        return log_beta, after, after[0:1, :] + soft[0:1, :]

    def stage_b(p, ki, mask, log_beta, after, total):
        used = used_ref[p]
        a = jnp.exp2(log_beta - after)
        if mask is not None:
            a = jnp.where(mask, a, 0.0)
        vt = vt_ref[pair(p), pl.ds(pl.multiple_of(ki * t, t), t)]
        acc_ref[p] += jnp.dot(vt, a.astype(BF16), preferred_element_type=F32) * jnp.exp2(-used)
        used_ref[p] = used + total

    def run_steps(steps):
        z_next = logits(steps[0][0], steps[0][1])
        pending = None
        for i, (p, ki, mask) in enumerate(steps):
            z_cur = z_next
            if i + 1 < len(steps):
                z_next = logits(steps[i + 1][0], steps[i + 1][1])
            staged = stage_a(z_cur, mask)
            if pending is not None:
                stage_b(*pending)
            pending = (p, ki, mask) + staged
        stage_b(*pending)

    used_ref[...] = jnp.zeros_like(used_ref)
    acc_ref[...] = jnp.zeros_like(acc_ref)
    key = lax.broadcasted_iota(jnp.int32, (t, 2 * t), 0)
    col = lax.broadcasted_iota(jnp.int32, (t, 2 * t), 1)
    causal = key < jnp.where(col >= t, col - t, col)
    diagonal = [(p, qi, causal) for p in range(n_pairs)]

    @pl.when(qi == 0)
    def _():
        run_steps(diagonal)

    @pl.when(qi >= 1)
    def _():
        run_steps(diagonal + [(p, qi - 1, None) for p in range(n_pairs)])

    def cond(c):
        ki, least_used = c
        return jnp.logical_and(ki >= 0, least_used < -SB_SKIP_LOG * LOG2_E)

    def body(c):
        ki, _ = c
        run_steps([(p, ki, None) for p in range(n_pairs)])
        return ki - 1, jnp.min(used_ref[...])

    lax.while_loop(cond, body, (qi - 2, jnp.min(used_ref[...])))

    for p in range(n_pairs):
        o = jnp.concatenate([acc_ref[p, :HEAD_DIM, :t], acc_ref[p, HEAD_DIM:, t:]], axis=0)
        o_ref[:, pair(p)] = o.T.astype(o_ref.dtype)


def _sb_prompt_attention(qs16, ks16, vst16, *, batch, seq):
    n, width = qs16.shape
    n_pairs = width // LANES
    t = SB_TILE
    nq = seq // t
    return pl.pallas_call(
        functools.partial(_sb_prompt_kernel, t=t),
        out_shape=jax.ShapeDtypeStruct((n, width), BF16),
        grid=(batch, nq),
        in_specs=[pl.BlockSpec((t, width), lambda b, i: (b * nq + i, 0)),
                  pl.BlockSpec((seq, width), lambda b, i: (b, 0)),
                  pl.BlockSpec((width, seq), lambda b, i: (0, b))],
        out_specs=pl.BlockSpec((t, width), lambda b, i: (b * nq + i, 0)),
        scratch_shapes=[pltpu.VMEM((n_pairs, 1, 2 * t), F32), pltpu.VMEM((n_pairs, LANES, 2 * t), F32)],
        compiler_params=_params(("parallel", "arbitrary")),
        name="stick_breaking_attention_prompt",
    )(qs16, ks16, vst16)


def _block_rows(q, group_w):
    t, w = q.shape
    n_groups = w // group_w
    qt = jnp.concatenate([q] * n_groups, axis=0)
    row_g = lax.broadcasted_iota(jnp.int32, qt.shape, 0) // t
    col_g = lax.broadcasted_iota(jnp.int32, qt.shape, 1) // group_w
    return jnp.where(row_g == col_g, qt, 0.0).astype(BF16)


def _pad_rows(x, rows):
    return jnp.concatenate([x, jnp.zeros((rows - x.shape[0], x.shape[1]), x.dtype)], axis=0)


def _gather_head_lanes(acc, n_groups, t):
    w = acc.shape[1]
    col_g = lax.broadcasted_iota(jnp.int32, (t, w), 1) // (w // n_groups)
    out = jnp.zeros((t, w), acc.dtype)
    for g in range(n_groups):
        out = jnp.where(col_g == g, acc[g * t:(g + 1) * t, :], out)
    return out


def _diff_sample_kernel(pt_ref, lq1_ref, lk1_ref, lq2_ref, lk2_ref, g_ref, q_ref, kvn_ref, bias_ref, *rest,
                        n_pages_step, page, half_w, t_new, lam_init, row0, slot_rows):
    page_refs = rest[:n_pages_step]
    o_ref, qb_ref, m_ref, l_ref, acc_ref = rest[n_pages_step:]
    j = pl.program_id(1)
    n_steps = pl.num_programs(1)
    n_heads = half_w // LANES

    def accumulate(k, v, bias, mask):
        s = lax.dot_general(qb_ref[...], k, NT_DIMS, preferred_element_type=F32) + bias
        if mask is not None:
            s = jnp.where(mask, s, -jnp.inf)
        m_old = m_ref[...]
        m_new = jnp.maximum(m_old, jnp.max(s, axis=1, keepdims=True))
        alpha = jnp.exp(m_old - m_new)
        p = jnp.exp(s - m_new)
        l_ref[...] = alpha * l_ref[...] + jnp.sum(p, axis=1, keepdims=True)
        acc_ref[...] = alpha * acc_ref[...] + jnp.dot(p.astype(BF16), v, preferred_element_type=F32)
        m_ref[...] = m_new

    @pl.when(j == 0)
    def _():
        qb_ref[...] = _block_rows(q_ref[...], HEAD_DIM)
        rows = qb_ref.shape[0]
        m_ref[...] = jnp.full(m_ref.shape, -jnp.inf, F32)
        l_ref[...] = jnp.zeros_like(l_ref)
        acc_ref[...] = jnp.zeros_like(acc_ref)
        kvn = _pad_rows(kvn_ref[...], page)
        tq = lax.broadcasted_iota(jnp.int32, (rows, page), 0) % t_new
        tk = lax.broadcasted_iota(jnp.int32, (rows, page), 1)
        accumulate(kvn[:, :half_w].astype(BF16), kvn[:, half_w:].astype(BF16),
                   bias_ref[:, 2 * page:3 * page], tk <= tq)

    def heads(ref, first):
        parts = [ref[0, pl.ds(row0 + first + h, page, stride=slot_rows), :] for h in range(n_heads)]
        return jnp.concatenate(parts, axis=1).astype(BF16)

    far = bias_ref[:, 0:page]
    last = jnp.where(j == n_steps - 1, bias_ref[:, page:2 * page], far)
    accumulate(jnp.concatenate([heads(r, 0) for r in page_refs], axis=0),
               jnp.concatenate([heads(r, n_heads) for r in page_refs], axis=0),
               jnp.concatenate([far] * (n_pages_step - 1) + [last], axis=1), None)

    @pl.when(j == n_steps - 1)
    def _():
        o = acc_ref[...] / l_ref[...]
        o = _gather_head_lanes(o, n_heads, 2 * t_new)
        lam = _lambda_value(lq1_ref, lk1_ref, lq2_ref, lk2_ref, lam_init)
        od = o[:t_new] - lam * o[t_new:]
        outs = []
        for h in range(n_heads):
            oh = od[:, h * LANES:(h + 1) * LANES]
            oh = oh * lax.rsqrt(jnp.mean(oh * oh, axis=-1, keepdims=True) + SUBLN_EPS)
            outs.append(oh * g_ref[...] * (1.0 - lam_init))
        o_ref[...] = jnp.concatenate(outs, axis=1).astype(o_ref.dtype)


def _diff_sample_attention(q32, kvn32, cache_rows, layer, depth, page_table, bias_tab, lam_params, subln_g, *,
                           t_new, lam_init):
    n, half_w = q32.shape
    n_seq, n_pages = page_table.shape
    n_heads = half_w // LANES
    slot_rows = depth * 2 * n_heads
    page = cache_rows.shape[1] // slot_rows
    gp = PAGES_PER_STEP
    rows = n_heads * 2 * t_new
    small = lambda b, j, pt: (0, 0)
    seq_row = lambda b, j, pt: (b, 0)

    def page_spec(g):
        return pl.BlockSpec((1, page * slot_rows, LANES), lambda b, j, pt: (pt[b, j * gp + g], 0, 0))

    return pl.pallas_call(
        functools.partial(_diff_sample_kernel, n_pages_step=gp, page=page, half_w=half_w, t_new=t_new,
                          lam_init=lam_init, row0=layer * 2 * n_heads, slot_rows=slot_rows),
        out_shape=jax.ShapeDtypeStruct((n, half_w), F32),
        grid_spec=pltpu.PrefetchScalarGridSpec(
            num_scalar_prefetch=1,
            grid=(n_seq, n_pages // gp),
            in_specs=[pl.BlockSpec((1, HEAD_DIM), small)] * 4
                     + [pl.BlockSpec((1, LANES), small),
                        pl.BlockSpec((t_new, half_w), seq_row),
                        pl.BlockSpec((t_new, 2 * half_w), seq_row),
                        pl.BlockSpec((rows, 3 * page), small)]
                     + [page_spec(g) for g in range(gp)],
            out_specs=pl.BlockSpec((t_new, half_w), seq_row),
            scratch_shapes=[pltpu.VMEM((rows, half_w), BF16), pltpu.VMEM((rows, 1), F32),
                            pltpu.VMEM((rows, 1), F32), pltpu.VMEM((rows, half_w), F32)]),
        compiler_params=_params(("parallel", "arbitrary")),
        name="diff_attention_sample",
    )(page_table, *lam_params, subln_g.reshape(1, LANES), q32, kvn32, bias_tab, *([cache_rows] * gp))


def _sb_sample_kernel(pt_ref, q_ref, kvn_ref, cache_ref, o_ref, qb_ref, run_ref, acc_ref, buf_ref, sem_ref, *,
                      n_pages, page, half_w, t_new, row0):
    b = pl.program_id(0)

    def page_copy(j, slot):
        return pltpu.make_async_copy(cache_ref.at[pt_ref[b, j], pl.ds(row0, 2 * half_w), :],
                                     buf_ref.at[slot], sem_ref.at[slot])

    slot_of = lambda j: (n_pages - 1 - j) & 1
    page_copy(n_pages - 1, 0).start()
    row = lax.broadcasted_iota(jnp.int32, (page, page), 0)
    colk = lax.broadcasted_iota(jnp.int32, (page, page), 1)
    later = (row > colk).astype(BF16)
    later2 = jnp.concatenate([later, later], axis=0)

    def accumulate(z, pv, mask):
        log_beta, log_rest = _log_sigmoid_pair(z)
        if mask is not None:
            log_rest = jnp.where(mask, log_rest, 0.0)
        hi, lo = _split_hi_lo(log_rest)
        suffix = jnp.dot(jnp.concatenate([hi, lo], axis=1), later2, preferred_element_type=F32)
        a = jnp.exp(log_beta + suffix + run_ref[...])
        if mask is not None:
            a = jnp.where(mask, a, 0.0)
        acc_ref[...] += pv(a.astype(BF16))
        run_ref[...] += jnp.sum(log_rest, axis=1, keepdims=True)

    qb_ref[...] = _block_rows(q_ref[...], HEAD_DIM)
    rows = qb_ref.shape[0]
    run_ref[...] = jnp.zeros_like(run_ref)
    acc_ref[...] = jnp.zeros_like(acc_ref)
    kvn = _pad_rows(kvn_ref[...], page)
    k_new = kvn[:, :half_w].astype(BF16)
    v_new = kvn[:, half_w:].astype(BF16)
    tq = lax.broadcasted_iota(jnp.int32, (rows, page), 0) % t_new
    tk = lax.broadcasted_iota(jnp.int32, (rows, page), 1)
    accumulate(lax.dot_general(qb_ref[...], k_new, NT_DIMS, preferred_element_type=F32),
               lambda a: jnp.dot(a, v_new, preferred_element_type=F32), tk < tq)

    def cond(c):
        j, live = c
        return jnp.logical_and(j >= 0, live > SB_SKIP_LOG)

    def body(c):
        j, _ = c
        slot = slot_of(j)
        page_copy(j, slot).wait()

        @pl.when(j >= 1)
        def _():
            page_copy(j - 1, 1 - slot).start()

        kt = buf_ref[slot, :half_w, :].astype(BF16)
        vt = buf_ref[slot, half_w:, :].astype(BF16)
        accumulate(jnp.dot(qb_ref[...], kt, preferred_element_type=F32),
                   lambda a: lax.dot_general(a, vt, NT_DIMS, preferred_element_type=F32), None)
        return j - 1, jnp.max(run_ref[...])

    j_end, _ = lax.while_loop(cond, body, (n_pages - 1, jnp.max(run_ref[...])))

    @pl.when(j_end >= 0)
    def _():
        page_copy(j_end, slot_of(j_end)).wait()

    n_heads = half_w // HEAD_DIM
    o_ref[...] = _gather_head_lanes(acc_ref[...], n_heads, t_new).astype(o_ref.dtype)


def _sb_sample_attention(q32, kvn32, cache_t, layer, page_table, *, t_new):
    n, half_w = q32.shape
    n_seq, n_pages = page_table.shape
    page = cache_t.shape[2]
    rows = (half_w // HEAD_DIM) * t_new
    seq_row = lambda b, pt: (b, 0)
    return pl.pallas_call(
        functools.partial(_sb_sample_kernel, n_pages=n_pages, page=page, half_w=half_w, t_new=t_new,
                          row0=layer * 2 * half_w),
        out_shape=jax.ShapeDtypeStruct((n, half_w), F32),
        grid_spec=pltpu.PrefetchScalarGridSpec(
            num_scalar_prefetch=1,
            grid=(n_seq,),
            in_specs=[pl.BlockSpec((t_new, half_w), seq_row),
                      pl.BlockSpec((t_new, 2 * half_w), seq_row),
                      pl.BlockSpec(memory_space=pl.ANY)],
            out_specs=pl.BlockSpec((t_new, half_w), seq_row),
            scratch_shapes=[pltpu.VMEM((rows, half_w), BF16), pltpu.VMEM((rows, 1), F32),
                            pltpu.VMEM((rows, half_w), F32), pltpu.VMEM((2, 2 * half_w, page), F32),
                            pltpu.SemaphoreType.DMA((2,))]),
        compiler_params=_params(("arbitrary",)),
        name="stick_breaking_attention_sample",
    )(page_table, q32, kvn32, cache_t)


def _router_gates(logits, n_groups, n_experts):
    lane = lax.broadcasted_iota(jnp.int32, logits.shape, 1).astype(F32)
    neg = -jnp.inf
    big = float(LANES)
    is_group = lane < n_groups
    gl = jnp.where(is_group, logits, neg)
    gmax = jnp.max(gl, axis=1, keepdims=True)
    pg_sel = 1.0 / jnp.sum(jnp.where(is_group, jnp.exp(gl - gmax), 0.0), axis=1, keepdims=True)
    g = jnp.min(jnp.where(gl == gmax, lane, big), axis=1, keepdims=True)
    lo = n_groups + g * n_experts
    in_group = jnp.logical_and(lane >= lo, lane < lo + n_experts)
    el = jnp.where(in_group, logits, neg)
    v1 = jnp.max(el, axis=1, keepdims=True)
    i1 = jnp.min(jnp.where(el == v1, lane, big), axis=1, keepdims=True)
    el2 = jnp.where(lane == i1, neg, el)
    v2 = jnp.max(el2, axis=1, keepdims=True)
    i2 = jnp.min(jnp.where(el2 == v2, lane, big), axis=1, keepdims=True)
    e2 = jnp.exp(v2 - v1)
    w1 = 1.0 / (1.0 + e2)
    w2 = e2 * w1
    return pg_sel * (jnp.where(lane == i1, w1, 0.0) + jnp.where(lane == i2, w2, 0.0))


def _post_kernel(x_ref, od_ref, os_ref, wo_ref, gf_ref, wr_ref, br_ref, wg_ref, wu_ref, wd_ref, gfin_ref,
                 y_ref, act_ref, *, n_groups, n_experts, final_norm):
    half = od_ref.shape[1]
    attn = (jnp.dot(od_ref[...], wo_ref[:half, :], preferred_element_type=F32)
            + jnp.dot(os_ref[...], wo_ref[half:, :], preferred_element_type=F32))
    x2 = x_ref[...] + attn
    hf = x2 * lax.rsqrt(jnp.mean(x2 * x2, axis=-1, keepdims=True) + NORM_EPS) * gf_ref[...]
    h, h_lo = _split_hi_lo(hf)
    both = jnp.dot(h, wr_ref[...], preferred_element_type=F32)
    logits = (both[:, :LANES] + both[:, LANES:]
              + jnp.dot(h_lo, wr_ref[:, :LANES], preferred_element_type=F32) + br_ref[...])
    gates = _router_gates(logits, n_groups, n_experts)

    n_total, _, d_exp = wg_ref.shape
    hidden = lambda e: (jnp.dot(h, wg_ref[e], preferred_element_type=F32),
                        jnp.dot(h, wu_ref[e], preferred_element_type=F32))
    nxt = hidden(0)
    for e in range(n_total):
        hg, hu = nxt
        if e + 1 < n_total:
            nxt = hidden(e + 1)
        gate = gates[:, n_groups + e:n_groups + e + 1]
        act_ref[:, e * d_exp:(e + 1) * d_exp] = (hg * (1.0 / (1.0 + jnp.exp(-hg))) * hu * gate).astype(BF16)
    x3 = x2 + jnp.dot(act_ref[...], wd_ref[...], preferred_element_type=F32)
    if final_norm:
        x3 = x3 * lax.rsqrt(jnp.mean(x3 * x3, axis=-1, keepdims=True) + NORM_EPS) * gfin_ref[...]
    y_ref[...] = x3


def _post_attention(x2d, od16, os16, wo16, g_ffn, wr16, b_router, wg16, wu16, wd16, g_final, *,
                    tm, n_groups, n_experts, final_norm):
    n, d = x2d.shape
    half = od16.shape[1]
    n_total, _, d_exp = wg16.shape
    row = lambda i: (i, 0)
    once = dict(pipeline_mode=pl.Buffered(1))
    const2 = lambda i: (0, 0)
    const3 = lambda i: (0, 0, 0)
    return pl.pallas_call(
        functools.partial(_post_kernel, n_groups=n_groups, n_experts=n_experts, final_norm=final_norm),
        out_shape=jax.ShapeDtypeStruct((n, d), F32),
        grid=(n // tm,),
        in_specs=[pl.BlockSpec((tm, d), row), pl.BlockSpec((tm, half), row), pl.BlockSpec((tm, half), row),
                  pl.BlockSpec((d, d), const2, **once), pl.BlockSpec((1, d), const2),
                  pl.BlockSpec((d, 2 * LANES), const2, **once), pl.BlockSpec((1, LANES), const2),
                  pl.BlockSpec((n_total, d, d_exp), const3, **once),
                  pl.BlockSpec((n_total, d, d_exp), const3, **once),
                  pl.BlockSpec((n_total * d_exp, d), const2, **once),
                  pl.BlockSpec((1, d), const2)],
        out_specs=pl.BlockSpec((tm, d), row),
        scratch_shapes=[pltpu.VMEM((tm, n_total * d_exp), BF16)],
        compiler_params=_params(("parallel",)),
        name="out_proj_hmoe",
    )(x2d, od16, os16, wo16, g_ffn.reshape(1, d), wr16, b_router, wg16, wu16, wd16, g_final.reshape(1, d))


def _row_tile(n, pref):
    tm = min(n, pref)
    while n % tm:
        tm //= 2
    return tm


def kernel(x_prompt, x_sample, cache_kv_diff, cache_kv_sb, page_table, rel_bias, g_mix, w_in, lambda_q1,
           lambda_k1, lambda_q2, lambda_k2, subln_g, w_out, g_ffn, w_group, b_group, w_erouter, b_erouter,
           w_gate, w_up, w_down, g_final):
    batch, seq, d = x_prompt.shape
    n_seq, t_new, _ = x_sample.shape
    n_pool, page, depth, _, h_diff, diff_hw = cache_kv_diff.shape
    _, _, _, _, h_sb, sb_hw = cache_kv_sb.shape
    n_pages = page_table.shape[1]
    past_len = n_pages * page
    n_groups, n_experts = w_erouter.shape[2], w_erouter.shape[3]
    half_w = h_diff * diff_hw
    assert diff_hw == 2 * HEAD_DIM == LANES and sb_hw == HEAD_DIM and h_sb * sb_hw == half_w
    assert seq % DIFF_TILE == 0 and seq % SB_TILE == 0 and n_pages % PAGES_PER_STEP == 0 and page == LANES
    assert n_groups + n_groups * n_experts <= LANES

    xp = x_prompt.reshape(batch * seq, d)
    xs = x_sample.reshape(n_seq * t_new, d)
    cache_d = cache_kv_diff.reshape(n_pool, page * depth * 2 * h_diff, diff_hw)
    cache_s = jnp.transpose(cache_kv_sb, (0, 2, 3, 4, 5, 1)).reshape(n_pool, depth * 2 * half_w, page)
    bias_tiles = _prompt_bias_tiles(rel_bias, DIFF_TILE)
    bias_tab = _sample_bias_table(rel_bias, past_len, page, t_new)
    tm_p = _row_tile(seq, 1024)
    tm_s = _row_tile(n_seq * t_new, 512)
    tm_moe_p = _row_tile(batch * seq, 512)

    kvd_p, kvs_p, kvd_s, kvs_s = [], [], [], []
    for l in range(depth):
        li = _lambda_init(l)
        lam_params = [a[l].reshape(1, HEAD_DIM).astype(F32) for a in (lambda_q1, lambda_k1, lambda_q2, lambda_k2)]
        w16 = w_in[l].astype(BF16)
        wo16 = w_out[l].astype(BF16)
        wr = jnp.concatenate([w_group[l].astype(F32), w_erouter[l].astype(F32).reshape(d, n_groups * n_experts)],
                             axis=1)
        wr = jnp.pad(wr, ((0, 0), (0, LANES - wr.shape[1])))
        wr_hi = wr.astype(BF16)
        wr16 = jnp.concatenate([wr_hi, (wr - wr_hi.astype(F32)).astype(BF16)], axis=1)
        br = jnp.concatenate([b_group[l].astype(F32), b_erouter[l].astype(F32).reshape(-1)])
        br = jnp.pad(br, (0, LANES - br.shape[0])).reshape(1, LANES)
        d_exp = w_gate.shape[-1]
        wg16 = w_gate[l].astype(BF16).reshape(n_groups * n_experts, d, d_exp)
        wu16 = w_up[l].astype(BF16).reshape(n_groups * n_experts, d, d_exp)
        wd16 = w_down[l].astype(BF16).reshape(n_groups * n_experts * d_exp, d)
        last = l == depth - 1

        qd16, qs16, kvd, kvst, kd16, vdt16, ks16, vst16 = _project_prompt(xp, g_mix[l], w16, tm=tm_p, batch=batch,
                                                                         seq=seq)
        od16 = _diff_prompt_attention(qd16, kd16, vdt16, bias_tiles, rel_bias, lam_params, subln_g[l],
                                      batch=batch, seq=seq, lam_init=li)
        os16 = _sb_prompt_attention(qs16, ks16, vst16, batch=batch, seq=seq)
        xp = _post_attention(xp, od16, os16, wo16, g_ffn[l], wr16, br, wg16, wu16, wd16, g_final,
                             tm=tm_moe_p, n_groups=n_groups, n_experts=n_experts, final_norm=last)
        kvd_p.append(kvd.reshape(batch, seq, 2, h_diff, diff_hw))
        kvs_p.append(jnp.transpose(kvst.reshape(batch, 2, h_sb, sb_hw, seq), (0, 4, 1, 2, 3)))

        qd32, qs32, kvd, kvs = _project_sample(xs, g_mix[l], w16, tm=tm_s)
        od = _diff_sample_attention(qd32, kvd, cache_d, l, depth, page_table, bias_tab, lam_params, subln_g[l],
                                    t_new=t_new, lam_init=li)
        osb = _sb_sample_attention(qs32, kvs, cache_s, l, page_table, t_new=t_new)
        xs = _post_attention(xs, od.astype(BF16), osb.astype(BF16), wo16, g_ffn[l], wr16, br, wg16, wu16,
                             wd16, g_final, tm=tm_s, n_groups=n_groups, n_experts=n_experts, final_norm=last)
        kvd_s.append(kvd.reshape(n_seq, t_new, 2, h_diff, diff_hw))
        kvs_s.append(kvs.reshape(n_seq, t_new, 2, h_sb, sb_hw))

    y_prompt = xp.reshape(batch, seq, d)
    y_sample = xs.reshape(n_seq, t_new, d)
    return (y_prompt, y_sample, jnp.stack(kvd_p, axis=2), jnp.stack(kvs_p, axis=2),
            jnp.stack(kvd_s, axis=2), jnp.stack(kvs_s, axis=2))
```

```python
import functools
import math

import numpy as np
import jax
import jax.numpy as jnp
from jax import lax
from jax.experimental import pallas as pl
from jax.experimental.pallas import tpu as pltpu

HEAD_DIM = 64
MAX_DISTANCE = 128
NORM_EPS = 1e-6
SUBLN_EPS = 1e-5
TOP_K_INNER = 2

LANES = 128
VMEM_LIMIT_BYTES = 56 * 1024 * 1024

DIFF_TILE = 256
SB_TILE = 256
SUM_ROWS = 16
SB_SKIP_LOG = -90.0
PAGES_PER_STEP = 32

F32 = jnp.float32
BF16 = jnp.bfloat16
NT_DIMS = (((1,), (1,)), ((), ()))
LOG2_E = math.log2(math.e)


def _lambda_init(layer):
    return 0.8 - 0.6 * math.exp(-0.3 * layer)


def _rel_bucket_np(rel, n_buckets):
    n = np.maximum(-rel, 0)
    max_exact = n_buckets // 2
    nf = np.maximum(n, 1).astype(np.float32)
    large = max_exact + (np.log(nf / np.float32(max_exact)) / np.float32(math.log(MAX_DISTANCE / max_exact))
                         * np.float32(n_buckets - max_exact)).astype(np.int32)
    large = np.minimum(large, n_buckets - 1)
    return np.where(n < max_exact, n, large).astype(np.int32)


def _params(sem):
    return pltpu.CompilerParams(dimension_semantics=sem, vmem_limit_bytes=VMEM_LIMIT_BYTES)


def _normed_proj_cols(x_ref, g_ref, w_ref, half_w):
    x = x_ref[...]
    ms = jnp.mean(x * x, axis=-1, keepdims=True)
    h = (x * lax.rsqrt(ms + NORM_EPS) * g_ref[...]).astype(BF16)
    return lambda i: jnp.dot(h, w_ref[:, i * half_w:(i + 1) * half_w], preferred_element_type=F32)


def _proj_prompt_kernel(x_ref, g_ref, w_ref, qd_ref, qs_ref, kvd_ref, kvst_ref, kd16_ref, vdt16_ref, ks16_ref,
                        vst16_ref, *, half_w, scale):
    cols = _normed_proj_cols(x_ref, g_ref, w_ref, half_w)
    tm = x_ref.shape[0]
    n_rows = 2 * half_w // LANES

    def store_diff_rows(first, val):
        for r in range(n_rows // 2):
            kvd_ref[pl.ds(first + r, tm, stride=n_rows), :] = val[:, r * LANES:(r + 1) * LANES]

    qd = cols(0)
    kd = cols(1)
    qd_ref[...] = (qd * (scale * LOG2_E)).astype(BF16)
    vd = cols(2)
    store_diff_rows(0, kd)
    kd16_ref[...] = kd.astype(BF16)
    qs = cols(3)
    store_diff_rows(n_rows // 2, vd)
    vdt16_ref[...] = vd.T.astype(BF16)
    ks = cols(4)
    qs_ref[...] = (qs * (scale * LOG2_E)).astype(BF16)
    vs = cols(5)
    ks16_ref[...] = ks.astype(BF16)
    kvst_ref[0, :half_w, :] = ks.T
    vst = vs.T
    kvst_ref[0, half_w:, :] = vst
    vst16_ref[...] = vst.astype(BF16)


def _project_prompt(x2d, g, w16, *, tm, batch, seq):
    n, d = x2d.shape
    half_w = w16.shape[1] // 6
    n_rows = 2 * half_w // LANES
    nbs = seq // tm
    row = lambda i: (i, 0)
    const = lambda i: (0, 0)
    colblk = lambda i: (0, i)
    out_shape = (jax.ShapeDtypeStruct((n, half_w), BF16), jax.ShapeDtypeStruct((n, half_w), BF16),
                 jax.ShapeDtypeStruct((n * n_rows, LANES), F32),
                 jax.ShapeDtypeStruct((batch, 2 * half_w, seq), F32),
                 jax.ShapeDtypeStruct((n, half_w), BF16), jax.ShapeDtypeStruct((half_w, n), BF16),
                 jax.ShapeDtypeStruct((n, half_w), BF16), jax.ShapeDtypeStruct((half_w, n), BF16))
    out_specs = (pl.BlockSpec((tm, half_w), row), pl.BlockSpec((tm, half_w), row),
                 pl.BlockSpec((tm * n_rows, LANES), row),
                 pl.BlockSpec((1, 2 * half_w, tm), lambda i: (i // nbs, 0, i % nbs)),
                 pl.BlockSpec((tm, half_w), row), pl.BlockSpec((half_w, tm), colblk),
                 pl.BlockSpec((tm, half_w), row), pl.BlockSpec((half_w, tm), colblk))
    return pl.pallas_call(
        functools.partial(_proj_prompt_kernel, half_w=half_w, scale=HEAD_DIM ** -0.5),
        out_shape=out_shape,
        grid=(n // tm,),
        in_specs=[pl.BlockSpec((tm, d), row), pl.BlockSpec((1, d), const),
                  pl.BlockSpec((d, 6 * half_w), const)],
        out_specs=out_specs,
        compiler_params=_params(("parallel",)),
        name="rmsnorm_qkv_proj_prompt",
    )(x2d, g.reshape(1, d), w16)


def _proj_sample_kernel(x_ref, g_ref, w_ref, qd_ref, qs_ref, kvd_ref, kvs_ref, *, half_w, scale):
    cols = _normed_proj_cols(x_ref, g_ref, w_ref, half_w)
    qd_ref[...] = cols(0) * scale
    qs_ref[...] = cols(3) * scale
    kvd_ref[:, :half_w] = cols(1)
    kvd_ref[:, half_w:] = cols(2)
    kvs_ref[:, :half_w] = cols(4)
    kvs_ref[:, half_w:] = cols(5)


def _project_sample(x2d, g, w16, *, tm):
    n, d = x2d.shape
    half_w = w16.shape[1] // 6
    row = lambda i: (i, 0)
    const = lambda i: (0, 0)
    return pl.pallas_call(
        functools.partial(_proj_sample_kernel, half_w=half_w, scale=HEAD_DIM ** -0.5),
        out_shape=(jax.ShapeDtypeStruct((n, half_w), F32), jax.ShapeDtypeStruct((n, half_w), F32),
                   jax.ShapeDtypeStruct((n, 2 * half_w), F32), jax.ShapeDtypeStruct((n, 2 * half_w), F32)),
        grid=(n // tm,),
        in_specs=[pl.BlockSpec((tm, d), row), pl.BlockSpec((1, d), const),
                  pl.BlockSpec((d, 6 * half_w), const)],
        out_specs=(pl.BlockSpec((tm, half_w), row), pl.BlockSpec((tm, half_w), row),
                   pl.BlockSpec((tm, 2 * half_w), row), pl.BlockSpec((tm, 2 * half_w), row)),
        compiler_params=_params(("parallel",)),
        name="rmsnorm_qkv_proj_sample",
    )(x2d, g.reshape(1, d), w16)


def _bias_select(bucket, rb_ref, h, n_buckets):
    out = jnp.zeros(bucket.shape, F32)
    for b in range(n_buckets):
        out = jnp.where(bucket == b, rb_ref[b, h], out)
    return out


def _prompt_bias_kernel(rb_ref, bucket_ref, out_ref, *, n_buckets):
    h = pl.program_id(0)
    t = bucket_ref.shape[1]
    for d in range(bucket_ref.shape[0]):
        tile = _bias_select(bucket_ref[d], rb_ref, h, n_buckets) * LOG2_E
        if d == 0:
            key = lax.broadcasted_iota(jnp.int32, tile.shape, 0)
            qry = lax.broadcasted_iota(jnp.int32, tile.shape, 1)
            tile = jnp.where(key <= qry, tile, -jnp.inf)
        out_ref[0, (1 - d) * t:(2 - d) * t, :] = jnp.concatenate([tile, tile], axis=1)


def _prompt_bias_tiles(rel_bias, t):
    n_buckets, n_heads = rel_bias.shape
    key = np.arange(t)[:, None]
    qry = np.arange(t)[None, :]
    bucket = np.stack([_rel_bucket_np(key - qry - d * t, n_buckets) for d in range(2)])
    return pl.pallas_call(
        functools.partial(_prompt_bias_kernel, n_buckets=n_buckets),
        out_shape=jax.ShapeDtypeStruct((n_heads, 2 * t, 2 * t), F32),
        grid=(n_heads,),
        in_specs=[pl.BlockSpec(memory_space=pltpu.SMEM),
                  pl.BlockSpec((2, t, t), lambda h: (0, 0, 0))],
        out_specs=pl.BlockSpec((1, 2 * t, 2 * t), lambda h: (h, 0, 0)),
        compiler_params=_params(("parallel",)),
        name="prompt_rel_bias_tiles",
    )(rel_bias.astype(F32), jnp.asarray(bucket))


def _sample_bias_kernel(rb_ref, bucket_ref, out_ref, *, n_buckets, n_heads):
    rows = bucket_ref.shape[0] // n_heads
    for h in range(n_heads):
        sl = slice(h * rows, (h + 1) * rows)
        out_ref[sl, :] = _bias_select(bucket_ref[sl, :], rb_ref, h, n_buckets)


def _sample_bias_table(rel_bias, past_len, page, t_new):
    n_buckets, n_heads = rel_bias.shape
    rows = n_heads * 2 * t_new
    q_pos = past_len + (np.arange(rows) % t_new)[:, None]
    far = np.full((rows, page), n_buckets - 1, np.int32)
    last = _rel_bucket_np((past_len - page + np.arange(page))[None, :] - q_pos, n_buckets)
    new = _rel_bucket_np((past_len + np.arange(page))[None, :] - q_pos, n_buckets)
    bucket = np.concatenate([far, last, new], axis=1)
    return pl.pallas_call(
        functools.partial(_sample_bias_kernel, n_buckets=n_buckets, n_heads=n_heads),
        out_shape=jax.ShapeDtypeStruct(bucket.shape, F32),
        in_specs=[pl.BlockSpec(memory_space=pltpu.SMEM), pl.BlockSpec(memory_space=pltpu.VMEM)],
        out_specs=pl.BlockSpec(memory_space=pltpu.VMEM),
        name="sample_rel_bias_table",
    )(rel_bias.astype(F32), jnp.asarray(bucket))


def _split_heads_rows(q):
    lane = lax.broadcasted_iota(jnp.int32, q.shape, 1)
    zero = jnp.zeros_like(q)
    return jnp.concatenate([jnp.where(lane < HEAD_DIM, q, zero), jnp.where(lane >= HEAD_DIM, q, zero)], axis=0)


def _lambda_value(lq1_ref, lk1_ref, lq2_ref, lk2_ref, lam_init):
    a = jnp.sum(lq1_ref[...] * lk1_ref[...], axis=-1, keepdims=True)
    b = jnp.sum(lq2_ref[...] * lk2_ref[...], axis=-1, keepdims=True)
    return jnp.exp(a) - jnp.exp(b) + lam_init


def _diff_prompt_kernel(rb_ref, lq1_ref, lk1_ref, lq2_ref, lk2_ref, g_ref, q_ref, k_ref, vt_ref, bias_ref,
                        o_ref, m_ref, acc_ref, s_ref, *, t, far_bucket, lam_init):
    qi = pl.program_id(1)
    n_heads = q_ref.shape[1] // LANES
    head = lambda h: slice(h * LANES, (h + 1) * LANES)
    qq = [_split_heads_rows(q_ref[:, head(h)]) for h in range(n_heads)]
    keys = lambda ki, nb: pl.ds(pl.multiple_of(ki * t, t), nb * t)

    def scores(h, ki, nb):
        return lax.dot_general(k_ref[keys(ki, nb), head(h)], qq[h], NT_DIMS,
                               preferred_element_type=F32)

    def accumulate(h, ki, nb, s, bias, first=False):
        if callable(bias):
            m_blk = jnp.max(s + bias(), axis=0, keepdims=True)
        else:
            m_blk = jnp.max(s, axis=0, keepdims=True) + bias
        if first:
            m_new = m_blk
        else:
            m_old = m_ref[h]
            m_new = jnp.maximum(m_old, m_blk)
        p = jnp.exp2(s - (m_new - (bias() if callable(bias) else bias)))
        vt1 = jnp.concatenate([vt_ref[head(h), keys(ki, nb)], jnp.ones((SUM_ROWS, nb * t), BF16)], axis=0)
        pv = jnp.dot(vt1, p.astype(BF16), preferred_element_type=F32)
        acc_ref[h] = pv if first else jnp.exp2(m_old - m_new) * acc_ref[h] + pv
        m_ref[h] = m_new

    def run_heads(ki, nb, s_first, bias, first, prefetch):
        s_next = s_first
        for h in range(n_heads):
            s_cur = s_next
            s_next = scores(h + 1, ki, nb) if h + 1 < n_heads else (prefetch() if prefetch else None)
            accumulate(h, ki, nb, s_cur, bias(h), first)
        return s_next

    @pl.when(qi == 0)
    def _():
        run_heads(0, 1, scores(0, 0, 1), lambda h: lambda: bias_ref[h, t:, :], True, None)

    @pl.when(qi >= 1)
    def _():
        s_ref[...] = run_heads(qi - 1, 2, scores(0, qi - 1, 2), lambda h: lambda: bias_ref[h], True,
                               lambda: scores(0, 0, 2))

    far_bias = [rb_ref[far_bucket, h] * LOG2_E for h in range(n_heads)]
    n_far = jnp.maximum(qi - 1, 0)

    far = lambda h: far_bias[h]
    n_pairs = n_far // 2
    odd = n_far % 2 == 1
    last = 2 * (n_pairs - 1)

    def far_pair(i, carry):
        s_ref[...] = run_heads(2 * i, 2, s_ref[...], far, False, lambda: scores(0, 2 * i + 2, 2))
        return carry

    lax.fori_loop(0, n_pairs - 1, far_pair, 0)

    @pl.when(jnp.logical_and(n_pairs >= 1, odd))
    def _():
        s_one = run_heads(last, 2, s_ref[...], far, False, lambda: scores(0, last + 2, 1))
        run_heads(last + 2, 1, s_one, far, False, None)

    @pl.when(jnp.logical_and(n_pairs >= 1, jnp.logical_not(odd)))
    def _():
        run_heads(last, 2, s_ref[...], far, False, None)

    @pl.when(n_far == 1)
    def _():
        run_heads(0, 1, s_ref[:t, :], far, False, None)

    lam = _lambda_value(lq1_ref, lk1_ref, lq2_ref, lk2_ref, lam_init)
    for h in range(n_heads):
        o = acc_ref[h, :LANES, :] / acc_ref[h, LANES:LANES + 1, :]
        od = o[:, :t] - lam * o[:, t:]
        od = od * lax.rsqrt(jnp.mean(od * od, axis=0, keepdims=True) + SUBLN_EPS)
        od = od * g_ref[...] * (1.0 - lam_init)
        o_ref[:, head(h)] = od.T.astype(o_ref.dtype)


def _diff_prompt_attention(qd16, kd16, vdt16, bias_tiles, rel_bias, lam_params, subln_g, *, batch, seq, lam_init):
    n, width = qd16.shape
    n_heads = width // LANES
    t = DIFF_TILE
    nq = seq // t
    n_buckets = rel_bias.shape[0]
    assert int(_rel_bucket_np(np.array([-(t + 1)]), n_buckets)[0]) == n_buckets - 1
    small = lambda b, i: (0, 0)
    return pl.pallas_call(
        functools.partial(_diff_prompt_kernel, t=t, far_bucket=n_buckets - 1, lam_init=lam_init),
        out_shape=jax.ShapeDtypeStruct((n, width), BF16),
        grid=(batch, nq),
        in_specs=[pl.BlockSpec(memory_space=pltpu.SMEM)]
                 + [pl.BlockSpec((1, HEAD_DIM), small)] * 4
                 + [pl.BlockSpec((LANES, 1), small),
                    pl.BlockSpec((t, width), lambda b, i: (b * nq + i, 0)),
                    pl.BlockSpec((seq, width), lambda b, i: (b, 0)),
                    pl.BlockSpec((width, seq), lambda b, i: (0, b)),
                    pl.BlockSpec((n_heads, 2 * t, 2 * t), lambda b, i: (0, 0, 0), pipeline_mode=pl.Buffered(1))],
        out_specs=pl.BlockSpec((t, width), lambda b, i: (b * nq + i, 0)),
        scratch_shapes=[pltpu.VMEM((n_heads, 1, 2 * t), F32),
                        pltpu.VMEM((n_heads, LANES + SUM_ROWS, 2 * t), F32), pltpu.VMEM((2 * t, 2 * t), F32)],
        compiler_params=_params(("parallel", "arbitrary")),
        name="diff_attention_prompt",
    )(rel_bias.astype(F32), *lam_params, subln_g.reshape(LANES, 1), qd16, kd16, vdt16, bias_tiles)


def _log_sigmoid_pair(z):
    soft = jnp.log(1.0 + jnp.exp(-jnp.abs(z)))
    return jnp.minimum(z, 0.0) - soft, jnp.minimum(-z, 0.0) - soft


def _split_hi_lo(x):
    hi = x.astype(BF16)
    lo = (x - hi.astype(F32)).astype(BF16)
    return hi, lo


def _sb_prompt_kernel(q_ref, k_ref, vt_ref, o_ref, used_ref, acc_ref, *, t):
    qi = pl.program_id(1)
    n_pairs = q_ref.shape[1] // LANES
    pair = lambda p: slice(p * LANES, (p + 1) * LANES)
    qq = [_split_heads_rows(q_ref[:, pair(p)]) for p in range(n_pairs)]
    row = lax.broadcasted_iota(jnp.int32, (t, t), 0)
    colk = lax.broadcasted_iota(jnp.int32, (t, t), 1)
    later = (colk > row).astype(BF16)
    later2 = jnp.concatenate([later, later], axis=1)

    def logits(p, ki):
        k = k_ref[pl.ds(pl.multiple_of(ki * t, t), t), pair(p)]
        return lax.dot_general(k, qq[p], NT_DIMS, preferred_element_type=F32)

    def stage_a(z, mask):
        soft = jnp.maximum(z, 0.0) + jnp.log2(1.0 + jnp.exp2(-jnp.abs(z)))
        log_beta = z - soft
        if mask is not None:
            soft = jnp.where(mask, soft, 0.0)
        hi, lo = _split_hi_lo(soft)
        after = jnp.dot(later2, jnp.concatenate([hi, lo], axis=0), preferred_element_type=F32)
        return log_beta, after, after[0:1, :] + soft[0:1, :]

    def stage_b(p, ki, mask, log_beta, after, total):
        used = used_ref[p]
        a = jnp.exp2(log_beta - after)
        if mask is not None:
            a = jnp.where(mask, a, 0.0)
        vt = vt_ref[pair(p), pl.ds(pl.multiple_of(ki * t, t), t)]
        acc_ref[p] += jnp.dot(vt, a.astype(BF16), preferred_element_type=F32) * jnp.exp2(-used)
        used_ref[p] = used + total

    def run_steps(steps):
        z_next = logits(steps[0][0], steps[0][1])
        pending = None
        for i, (p, ki, mask) in enumerate(steps):
            z_cur = z_next
            if i + 1 < len(steps):
                z_next = logits(steps[i + 1][0], steps[i + 1][1])
            staged = stage_a(z_cur, mask)
            if pending is not None:
                stage_b(*pending)
            pending = (p, ki, mask) + staged
        stage_b(*pending)

    used_ref[...] = jnp.zeros_like(used_ref)
    acc_ref[...] = jnp.zeros_like(acc_ref)
    key = lax.broadcasted_iota(jnp.int32, (t, 2 * t), 0)
    col = lax.broadcasted_iota(jnp.int32, (t, 2 * t), 1)
    causal = key < jnp.where(col >= t, col - t, col)
    diagonal = [(p, qi, causal) for p in range(n_pairs)]

    @pl.when(qi == 0)
    def _():
        run_steps(diagonal)

    @pl.when(qi >= 1)
    def _():
        run_steps(diagonal + [(p, qi - 1, None) for p in range(n_pairs)])

    def cond(c):
        ki, least_used = c
        return jnp.logical_and(ki >= 0, least_used < -SB_SKIP_LOG * LOG2_E)

    def body(c):
        ki, _ = c
        run_steps([(p, ki, None) for p in range(n_pairs)])
        return ki - 1, jnp.min(used_ref[...])

    lax.while_loop(cond, body, (qi - 2, jnp.min(used_ref[...])))

    for p in range(n_pairs):
        o = jnp.concatenate([acc_ref[p, :HEAD_DIM, :t], acc_ref[p, HEAD_DIM:, t:]], axis=0)
        o_ref[:, pair(p)] = o.T.astype(o_ref.dtype)


def _sb_prompt_attention(qs16, ks16, vst16, *, batch, seq):
    n, width = qs16.shape
    n_pairs = width // LANES
    t = SB_TILE
    nq = seq // t
    return pl.pallas_call(
        functools.partial(_sb_prompt_kernel, t=t),
        out_shape=jax.ShapeDtypeStruct((n, width), BF16),
        grid=(batch, nq),
        in_specs=[pl.BlockSpec((t, width), lambda b, i: (b * nq + i, 0)),
                  pl.BlockSpec((seq, width), lambda b, i: (b, 0)),
                  pl.BlockSpec((width, seq), lambda b, i: (0, b))],
        out_specs=pl.BlockSpec((t, width), lambda b, i: (b * nq + i, 0)),
        scratch_shapes=[pltpu.VMEM((n_pairs, 1, 2 * t), F32), pltpu.VMEM((n_pairs, LANES, 2 * t), F32)],
        compiler_params=_params(("parallel", "arbitrary")),
        name="stick_breaking_attention_prompt",
    )(qs16, ks16, vst16)


def _block_rows(q, group_w):
    t, w = q.shape
    n_groups = w // group_w
    qt = jnp.concatenate([q] * n_groups, axis=0)
    row_g = lax.broadcasted_iota(jnp.int32, qt.shape, 0) // t
    col_g = lax.broadcasted_iota(jnp.int32, qt.shape, 1) // group_w
    return jnp.where(row_g == col_g, qt, 0.0).astype(BF16)


def _pad_rows(x, rows):
    return jnp.concatenate([x, jnp.zeros((rows - x.shape[0], x.shape[1]), x.dtype)], axis=0)


def _gather_head_lanes(acc, n_groups, t):
    w = acc.shape[1]
    col_g = lax.broadcasted_iota(jnp.int32, (t, w), 1) // (w // n_groups)
    out = jnp.zeros((t, w), acc.dtype)
    for g in range(n_groups):
        out = jnp.where(col_g == g, acc[g * t:(g + 1) * t, :], out)
    return out


def _diff_sample_kernel(pt_ref, lq1_ref, lk1_ref, lq2_ref, lk2_ref, g_ref, q_ref, kvn_ref, bias_ref, *rest,
                        n_pages_step, page, half_w, t_new, lam_init, row0, slot_rows):
    page_refs = rest[:n_pages_step]
    o_ref, qb_ref, m_ref, l_ref, acc_ref = rest[n_pages_step:]
    j = pl.program_id(1)
    n_steps = pl.num_programs(1)
    n_heads = half_w // LANES

    def accumulate(k, v, bias, mask):
        s = lax.dot_general(qb_ref[...], k, NT_DIMS, preferred_element_type=F32) + bias
        if mask is not None:
            s = jnp.where(mask, s, -jnp.inf)
        m_old = m_ref[...]
        m_new = jnp.maximum(m_old, jnp.max(s, axis=1, keepdims=True))
        alpha = jnp.exp(m_old - m_new)
        p = jnp.exp(s - m_new)
        l_ref[...] = alpha * l_ref[...] + jnp.sum(p, axis=1, keepdims=True)
        acc_ref[...] = alpha * acc_ref[...] + jnp.dot(p.astype(BF16), v, preferred_element_type=F32)
        m_ref[...] = m_new

    @pl.when(j == 0)
    def _():
        qb_ref[...] = _block_rows(q_ref[...], HEAD_DIM)
        rows = qb_ref.shape[0]
        m_ref[...] = jnp.full(m_ref.shape, -jnp.inf, F32)
        l_ref[...] = jnp.zeros_like(l_ref)
        acc_ref[...] = jnp.zeros_like(acc_ref)
        kvn = _pad_rows(kvn_ref[...], page)
        tq = lax.broadcasted_iota(jnp.int32, (rows, page), 0) % t_new
        tk = lax.broadcasted_iota(jnp.int32, (rows, page), 1)
        accumulate(kvn[:, :half_w].astype(BF16), kvn[:, half_w:].astype(BF16),
                   bias_ref[:, 2 * page:3 * page], tk <= tq)

    def heads(ref, first):
        parts = [ref[0, pl.ds(row0 + first + h, page, stride=slot_rows), :] for h in range(n_heads)]
        return jnp.concatenate(parts, axis=1).astype(BF16)

    far = bias_ref[:, 0:page]
    last = jnp.where(j == n_steps - 1, bias_ref[:, page:2 * page], far)
    accumulate(jnp.concatenate([heads(r, 0) for r in page_refs], axis=0),
               jnp.concatenate([heads(r, n_heads) for r in page_refs], axis=0),
               jnp.concatenate([far] * (n_pages_step - 1) + [last], axis=1), None)

    @pl.when(j == n_steps - 1)
    def _():
        o = acc_ref[...] / l_ref[...]
        o = _gather_head_lanes(o, n_heads, 2 * t_new)
        lam = _lambda_value(lq1_ref, lk1_ref, lq2_ref, lk2_ref, lam_init)
        od = o[:t_new] - lam * o[t_new:]
        outs = []
        for h in range(n_heads):
            oh = od[:, h * LANES:(h + 1) * LANES]
            oh = oh * lax.rsqrt(jnp.mean(oh * oh, axis=-1, keepdims=True) + SUBLN_EPS)
            outs.append(oh * g_ref[...] * (1.0 - lam_init))
        o_ref[...] = jnp.concatenate(outs, axis=1).astype(o_ref.dtype)


def _diff_sample_attention(q32, kvn32, cache_rows, layer, depth, page_table, bias_tab, lam_params, subln_g, *,
                           t_new, lam_init):
    n, half_w = q32.shape
    n_seq, n_pages = page_table.shape
    n_heads = half_w // LANES
    slot_rows = depth * 2 * n_heads
    page = cache_rows.shape[1] // slot_rows
    gp = PAGES_PER_STEP
    rows = n_heads * 2 * t_new
    small = lambda b, j, pt: (0, 0)
    seq_row = lambda b, j, pt: (b, 0)

    def page_spec(g):
        return pl.BlockSpec((1, page * slot_rows, LANES), lambda b, j, pt: (pt[b, j * gp + g], 0, 0))

    return pl.pallas_call(
        functools.partial(_diff_sample_kernel, n_pages_step=gp, page=page, half_w=half_w, t_new=t_new,
                          lam_init=lam_init, row0=layer * 2 * n_heads, slot_rows=slot_rows),
        out_shape=jax.ShapeDtypeStruct((n, half_w), F32),
        grid_spec=pltpu.PrefetchScalarGridSpec(
            num_scalar_prefetch=1,
            grid=(n_seq, n_pages // gp),
            in_specs=[pl.BlockSpec((1, HEAD_DIM), small)] * 4
                     + [pl.BlockSpec((1, LANES), small),
                        pl.BlockSpec((t_new, half_w), seq_row),
                        pl.BlockSpec((t_new, 2 * half_w), seq_row),
                        pl.BlockSpec((rows, 3 * page), small)]
                     + [page_spec(g) for g in range(gp)],
            out_specs=pl.BlockSpec((t_new, half_w), seq_row),
            scratch_shapes=[pltpu.VMEM((rows, half_w), BF16), pltpu.VMEM((rows, 1), F32),
                            pltpu.VMEM((rows, 1), F32), pltpu.VMEM((rows, half_w), F32)]),
        compiler_params=_params(("parallel", "arbitrary")),
        name="diff_attention_sample",
    )(page_table, *lam_params, subln_g.reshape(1, LANES), q32, kvn32, bias_tab, *([cache_rows] * gp))


def _sb_sample_kernel(pt_ref, q_ref, kvn_ref, cache_ref, o_ref, qb_ref, run_ref, acc_ref, buf_ref, sem_ref, *,
                      n_pages, page, half_w, t_new, row0):
    b = pl.program_id(0)

    def page_copy(j, slot):
        return pltpu.make_async_copy(cache_ref.at[pt_ref[b, j], pl.ds(row0, 2 * half_w), :],
                                     buf_ref.at[slot], sem_ref.at[slot])

    slot_of = lambda j: (n_pages - 1 - j) & 1
    page_copy(n_pages - 1, 0).start()
    row = lax.broadcasted_iota(jnp.int32, (page, page), 0)
    colk = lax.broadcasted_iota(jnp.int32, (page, page), 1)
    later = (row > colk).astype(BF16)
    later2 = jnp.concatenate([later, later], axis=0)

    def accumulate(z, pv, mask):
        log_beta, log_rest = _log_sigmoid_pair(z)
        if mask is not None:
            log_rest = jnp.where(mask, log_rest, 0.0)
        hi, lo = _split_hi_lo(log_rest)
        suffix = jnp.dot(jnp.concatenate([hi, lo], axis=1), later2, preferred_element_type=F32)
        a = jnp.exp(log_beta + suffix + run_ref[...])
        if mask is not None:
            a = jnp.where(mask, a, 0.0)
        acc_ref[...] += pv(a.astype(BF16))
        run_ref[...] += jnp.sum(log_rest, axis=1, keepdims=True)

    qb_ref[...] = _block_rows(q_ref[...], HEAD_DIM)
    rows = qb_ref.shape[0]
    run_ref[...] = jnp.zeros_like(run_ref)
    acc_ref[...] = jnp.zeros_like(acc_ref)
    kvn = _pad_rows(kvn_ref[...], page)
    k_new = kvn[:, :half_w].astype(BF16)
    v_new = kvn[:, half_w:].astype(BF16)
    tq = lax.broadcasted_iota(jnp.int32, (rows, page), 0) % t_new
    tk = lax.broadcasted_iota(jnp.int32, (rows, page), 1)
    accumulate(lax.dot_general(qb_ref[...], k_new, NT_DIMS, preferred_element_type=F32),
               lambda a: jnp.dot(a, v_new, preferred_element_type=F32), tk < tq)

    def cond(c):
        j, live = c
        return jnp.logical_and(j >= 0, live > SB_SKIP_LOG)

    def body(c):
        j, _ = c
        slot = slot_of(j)
        page_copy(j, slot).wait()

        @pl.when(j >= 1)
        def _():
            page_copy(j - 1, 1 - slot).start()

        kt = buf_ref[slot, :half_w, :].astype(BF16)
        vt = buf_ref[slot, half_w:, :].astype(BF16)
        accumulate(jnp.dot(qb_ref[...], kt, preferred_element_type=F32),
                   lambda a: lax.dot_general(a, vt, NT_DIMS, preferred_element_type=F32), None)
        return j - 1, jnp.max(run_ref[...])

    j_end, _ = lax.while_loop(cond, body, (n_pages - 1, jnp.max(run_ref[...])))

    @pl.when(j_end >= 0)
    def _():
        page_copy(j_end, slot_of(j_end)).wait()

    n_heads = half_w // HEAD_DIM
    o_ref[...] = _gather_head_lanes(acc_ref[...], n_heads, t_new).astype(o_ref.dtype)


def _sb_sample_attention(q32, kvn32, cache_t, layer, page_table, *, t_new):
    n, half_w = q32.shape
    n_seq, n_pages = page_table.shape
    page = cache_t.shape[2]
    rows = (half_w // HEAD_DIM) * t_new
    seq_row = lambda b, pt: (b, 0)
    return pl.pallas_call(
        functools.partial(_sb_sample_kernel, n_pages=n_pages, page=page, half_w=half_w, t_new=t_new,
                          row0=layer * 2 * half_w),
        out_shape=jax.ShapeDtypeStruct((n, half_w), F32),
        grid_spec=pltpu.PrefetchScalarGridSpec(
            num_scalar_prefetch=1,
            grid=(n_seq,),
            in_specs=[pl.BlockSpec((t_new, half_w), seq_row),
                      pl.BlockSpec((t_new, 2 * half_w), seq_row),
                      pl.BlockSpec(memory_space=pl.ANY)],
            out_specs=pl.BlockSpec((t_new, half_w), seq_row),
            scratch_shapes=[pltpu.VMEM((rows, half_w), BF16), pltpu.VMEM((rows, 1), F32),
                            pltpu.VMEM((rows, half_w), F32), pltpu.VMEM((2, 2 * half_w, page), F32),
                            pltpu.SemaphoreType.DMA((2,))]),
        compiler_params=_params(("arbitrary",)),
        name="stick_breaking_attention_sample",
    )(page_table, q32, kvn32, cache_t)


def _router_gates(logits, n_groups, n_experts):
    lane = lax.broadcasted_iota(jnp.int32, logits.shape, 1).astype(F32)
    neg = -jnp.inf
    big = float(LANES)
    is_group = lane < n_groups
    gl = jnp.where(is_group, logits, neg)
    gmax = jnp.max(gl, axis=1, keepdims=True)
    pg_sel = 1.0 / jnp.sum(jnp.where(is_group, jnp.exp(gl - gmax), 0.0), axis=1, keepdims=True)
    g = jnp.min(jnp.where(gl == gmax, lane, big), axis=1, keepdims=True)
    lo = n_groups + g * n_experts
    in_group = jnp.logical_and(lane >= lo, lane < lo + n_experts)
    el = jnp.where(in_group, logits, neg)
    v1 = jnp.max(el, axis=1, keepdims=True)
    i1 = jnp.min(jnp.where(el == v1, lane, big), axis=1, keepdims=True)
    el2 = jnp.where(lane == i1, neg, el)
    v2 = jnp.max(el2, axis=1, keepdims=True)
    i2 = jnp.min(jnp.where(el2 == v2, lane, big), axis=1, keepdims=True)
    e2 = jnp.exp(v2 - v1)
    w1 = 1.0 / (1.0 + e2)
    w2 = e2 * w1
    return pg_sel * (jnp.where(lane == i1, w1, 0.0) + jnp.where(lane == i2, w2, 0.0))


def _post_kernel(x_ref, od_ref, os_ref, wo_ref, gf_ref, wr_ref, br_ref, wg_ref, wu_ref, wd_ref, gfin_ref,
                 y_ref, act_ref, *, n_groups, n_experts, final_norm):
    half = od_ref.shape[1]
    attn = (jnp.dot(od_ref[...], wo_ref[:half, :], preferred_element_type=F32)
            + jnp.dot(os_ref[...], wo_ref[half:, :], preferred_element_type=F32))
    x2 = x_ref[...] + attn
    hf = x2 * lax.rsqrt(jnp.mean(x2 * x2, axis=-1, keepdims=True) + NORM_EPS) * gf_ref[...]
    h, h_lo = _split_hi_lo(hf)
    both = jnp.dot(h, wr_ref[...], preferred_element_type=F32)
    logits = (both[:, :LANES] + both[:, LANES:]
              + jnp.dot(h_lo, wr_ref[:, :LANES], preferred_element_type=F32) + br_ref[...])
    gates = _router_gates(logits, n_groups, n_experts)

    n_total, _, d_exp = wg_ref.shape
    hidden = lambda e: (jnp.dot(h, wg_ref[e], preferred_element_type=F32),
                        jnp.dot(h, wu_ref[e], preferred_element_type=F32))
    nxt = hidden(0)
    for e in range(n_total):
        hg, hu = nxt
        if e + 1 < n_total:
            nxt = hidden(e + 1)
        gate = gates[:, n_groups + e:n_groups + e + 1]
        act_ref[:, e * d_exp:(e + 1) * d_exp] = (hg * (1.0 / (1.0 + jnp.exp(-hg))) * hu * gate).astype(BF16)
    x3 = x2 + jnp.dot(act_ref[...], wd_ref[...], preferred_element_type=F32)
    if final_norm:
        x3 = x3 * lax.rsqrt(jnp.mean(x3 * x3, axis=-1, keepdims=True) + NORM_EPS) * gfin_ref[...]
    y_ref[...] = x3


def _post_attention(x2d, od16, os16, wo16, g_ffn, wr16, b_router, wg16, wu16, wd16, g_final, *,
                    tm, n_groups, n_experts, final_norm):
    n, d = x2d.shape
    half = od16.shape[1]
    n_total, _, d_exp = wg16.shape
    row = lambda i: (i, 0)
    once = dict(pipeline_mode=pl.Buffered(1))
    const2 = lambda i: (0, 0)
    const3 = lambda i: (0, 0, 0)
    return pl.pallas_call(
        functools.partial(_post_kernel, n_groups=n_groups, n_experts=n_experts, final_norm=final_norm),
        out_shape=jax.ShapeDtypeStruct((n, d), F32),
        grid=(n // tm,),
        in_specs=[pl.BlockSpec((tm, d), row), pl.BlockSpec((tm, half), row), pl.BlockSpec((tm, half), row),
                  pl.BlockSpec((d, d), const2, **once), pl.BlockSpec((1, d), const2),
                  pl.BlockSpec((d, 2 * LANES), const2, **once), pl.BlockSpec((1, LANES), const2),
                  pl.BlockSpec((n_total, d, d_exp), const3, **once),
                  pl.BlockSpec((n_total, d, d_exp), const3, **once),
                  pl.BlockSpec((n_total * d_exp, d), const2, **once),
                  pl.BlockSpec((1, d), const2)],
        out_specs=pl.BlockSpec((tm, d), row),
        scratch_shapes=[pltpu.VMEM((tm, n_total * d_exp), BF16)],
        compiler_params=_params(("parallel",)),
        name="out_proj_hmoe",
    )(x2d, od16, os16, wo16, g_ffn.reshape(1, d), wr16, b_router, wg16, wu16, wd16, g_final.reshape(1, d))


def _row_tile(n, pref):
    tm = min(n, pref)
    while n % tm:
        tm //= 2
    return tm


def kernel(x_prompt, x_sample, cache_kv_diff, cache_kv_sb, page_table, rel_bias, g_mix, w_in, lambda_q1,
           lambda_k1, lambda_q2, lambda_k2, subln_g, w_out, g_ffn, w_group, b_group, w_erouter, b_erouter,
           w_gate, w_up, w_down, g_final):
    batch, seq, d = x_prompt.shape
    n_seq, t_new, _ = x_sample.shape
    n_pool, page, depth, _, h_diff, diff_hw = cache_kv_diff.shape
    _, _, _, _, h_sb, sb_hw = cache_kv_sb.shape
    n_pages = page_table.shape[1]
    past_len = n_pages * page
    n_groups, n_experts = w_erouter.shape[2], w_erouter.shape[3]
    half_w = h_diff * diff_hw
    assert diff_hw == 2 * HEAD_DIM == LANES and sb_hw == HEAD_DIM and h_sb * sb_hw == half_w
    assert seq % DIFF_TILE == 0 and seq % SB_TILE == 0 and n_pages % PAGES_PER_STEP == 0 and page == LANES
    assert n_groups + n_groups * n_experts <= LANES

    xp = x_prompt.reshape(batch * seq, d)
    xs = x_sample.reshape(n_seq * t_new, d)
    cache_d = cache_kv_diff.reshape(n_pool, page * depth * 2 * h_diff, diff_hw)
    cache_s = jnp.transpose(cache_kv_sb, (0, 2, 3, 4, 5, 1)).reshape(n_pool, depth * 2 * half_w, page)
    bias_tiles = _prompt_bias_tiles(rel_bias, DIFF_TILE)
    bias_tab = _sample_bias_table(rel_bias, past_len, page, t_new)
    tm_p = _row_tile(seq, 1024)
    tm_s = _row_tile(n_seq * t_new, 512)
    tm_moe_p = _row_tile(batch * seq, 512)

    kvd_p, kvs_p, kvd_s, kvs_s = [], [], [], []
    for l in range(depth):
        li = _lambda_init(l)
        lam_params = [a[l].reshape(1, HEAD_DIM).astype(F32) for a in (lambda_q1, lambda_k1, lambda_q2, lambda_k2)]
        w16 = w_in[l].astype(BF16)
        wo16 = w_out[l].astype(BF16)
        wr = jnp.concatenate([w_group[l].astype(F32), w_erouter[l].astype(F32).reshape(d, n_groups * n_experts)],
                             axis=1)
        wr = jnp.pad(wr, ((0, 0), (0, LANES - wr.shape[1])))
        wr_hi = wr.astype(BF16)
        wr16 = jnp.concatenate([wr_hi, (wr - wr_hi.astype(F32)).astype(BF16)], axis=1)
        br = jnp.concatenate([b_group[l].astype(F32), b_erouter[l].astype(F32).reshape(-1)])
        br = jnp.pad(br, (0, LANES - br.shape[0])).reshape(1, LANES)
        d_exp = w_gate.shape[-1]
        wg16 = w_gate[l].astype(BF16).reshape(n_groups * n_experts, d, d_exp)
        wu16 = w_up[l].astype(BF16).reshape(n_groups * n_experts, d, d_exp)
        wd16 = w_down[l].astype(BF16).reshape(n_groups * n_experts * d_exp, d)
        last = l == depth - 1

        qd16, qs16, kvd, kvst, kd16, vdt16, ks16, vst16 = _project_prompt(xp, g_mix[l], w16, tm=tm_p, batch=batch,
                                                                         seq=seq)
        od16 = _diff_prompt_attention(qd16, kd16, vdt16, bias_tiles, rel_bias, lam_params, subln_g[l],
                                      batch=batch, seq=seq, lam_init=li)
        os16 = _sb_prompt_attention(qs16, ks16, vst16, batch=batch, seq=seq)
        xp = _post_attention(xp, od16, os16, wo16, g_ffn[l], wr16, br, wg16, wu16, wd16, g_final,
                             tm=tm_moe_p, n_groups=n_groups, n_experts=n_experts, final_norm=last)
        kvd_p.append(kvd.reshape(batch, seq, 2, h_diff, diff_hw))
        kvs_p.append(jnp.transpose(kvst.reshape(batch, 2, h_sb, sb_hw, seq), (0, 4, 1, 2, 3)))

        qd32, qs32, kvd, kvs = _project_sample(xs, g_mix[l], w16, tm=tm_s)
        od = _diff_sample_attention(qd32, kvd, cache_d, l, depth, page_table, bias_tab, lam_params, subln_g[l],
                                    t_new=t_new, lam_init=li)
        osb = _sb_sample_attention(qs32, kvs, cache_s, l, page_table, t_new=t_new)
        xs = _post_attention(xs, od.astype(BF16), osb.astype(BF16), wo16, g_ffn[l], wr16, br, wg16, wu16,
                             wd16, g_final, tm=tm_s, n_groups=n_groups, n_experts=n_experts, final_norm=last)
        kvd_s.append(kvd.reshape(n_seq, t_new, 2, h_diff, diff_hw))
        kvs_s.append(kvs.reshape(n_seq, t_new, 2, h_sb, sb_hw))

    y_prompt = xp.reshape(batch, seq, d)
    y_sample = xs.reshape(n_seq, t_new, d)
    return (y_prompt, y_sample, jnp.stack(kvd_p, axis=2), jnp.stack(kvs_p, axis=2),
            jnp.stack(kvd_s, axis=2), jnp.stack(kvs_s, axis=2))
```
